```python
import jax
import jax.numpy as jnp
from jax import lax
import numpy as np

D_MODEL = 1024
BATCH = 32
SEQ = 2048
DEPTH = 2

NSA_HEADS = 8
NSA_GROUPS = 2
NSA_REP = NSA_HEADS // NSA_GROUPS
NSA_DH = 64
NSA_WIDTH = NSA_HEADS * NSA_DH
NSA_KV_WIDTH = NSA_GROUPS * NSA_DH
CMP_LEN = 32
CMP_STRIDE = 16
CMP_HIDDEN = 2 * NSA_DH
SEL_LEN = 64
SEL_TOPK = 8
FORCE_SCORE = 1.0e4
WINDOW = 512
Q_BLOCK = 128
HGRN_HEADS = 4
HGRN_DK = 128
HGRN_DV = 128
HGRN_WIDTH = HGRN_HEADS * HGRN_DK
HGRN_VWIDTH = HGRN_HEADS * HGRN_DV
HGRN_CHUNK = 64
MLP_HIDDEN = 4 * D_MODEL
ROPE_THETA = 10000.0
LN_EPS = 1e-5
RMS_EPS = 1e-6
DEEPNORM_ALPHA = (2 * DEPTH) ** 0.25
DEEPNORM_BETA = (8 * DEPTH) ** -0.25
IN_SIZES = (NSA_WIDTH,) + (NSA_KV_WIDTH,) * 6 + (3 * NSA_HEADS,) + (HGRN_WIDTH, HGRN_WIDTH, HGRN_VWIDTH, HGRN_VWIDTH) + (D_MODEL, D_MODEL)
IN_OFFSETS = [int(v) for v in np.cumsum(IN_SIZES)[:-1]]
N_IN = int(sum(IN_SIZES))

kernel_name = 'hybrid_nsa_hgrn2_deepnorm_adaln'


def layer_norm(x, g, b):
    xf = x.astype(jnp.float32)
    mu = jnp.mean(xf, axis=-1, keepdims=True)
    var = jnp.mean(jnp.square(xf - mu), axis=-1, keepdims=True)
    y = (xf - mu) * lax.rsqrt(var + LN_EPS) * g.astype(jnp.float32) + b.astype(jnp.float32)
    return y.astype(x.dtype)


def masked_softmax(s, mask):
    s = jnp.where(mask, s.astype(jnp.float32), -jnp.inf)
    m = jnp.max(s, axis=-1, keepdims=True)
    m = jnp.where(jnp.isfinite(m), m, 0.0)
    e = jnp.where(mask, jnp.exp(s - m), 0.0)
    return e / jnp.maximum(jnp.sum(e, axis=-1, keepdims=True), 1e-30)


def rope_tables(S, dim, dtype):
    inv = 1.0 / (ROPE_THETA ** (jnp.arange(0, dim, 2, dtype=jnp.float32) / dim))
    ang = jnp.arange(S, dtype=jnp.float32)[:, None] * inv[None, :]
    return jnp.cos(ang).astype(dtype), jnp.sin(ang).astype(dtype)


def apply_rope(t, cos, sin):
    half = t.shape[-1] // 2
    t1, t2 = t[..., :half], t[..., half:]
    return jnp.concatenate([t1 * cos - t2 * sin, t2 * cos + t1 * sin], axis=-1)


def nsa_mixer(q, k_c, v_c, k_s, v_s, k_w, v_w, gate_logits, pe_k, pe_v, wk1, wk2, wv1, wv2):
    B, S, _ = q.shape
    dt = q.dtype
    G, R, dh = NSA_GROUPS, NSA_REP, NSA_DH
    scale = dh ** -0.5
    cos, sin = rope_tables(S, dh, dt)
    qh = apply_rope(q.reshape(B, S, G, R, dh).transpose(0, 2, 3, 1, 4), cos, sin)

    def kv_heads(t):
        return t.reshape(B, S, G, dh).transpose(0, 2, 1, 3)

    kc = apply_rope(kv_heads(k_c), cos, sin)
    vc = kv_heads(v_c)
    ks = apply_rope(kv_heads(k_s), cos, sin)
    vs = kv_heads(v_s)
    kw = apply_rope(kv_heads(k_w), cos, sin)
    vw = kv_heads(v_w)
    pos = np.arange(S)

    nc = (S - CMP_LEN) // CMP_STRIDE + 1
    cstart = np.arange(nc) * CMP_STRIDE
    cidx = cstart[:, None] + np.arange(CMP_LEN)[None, :]

    def compress(t, pe, w1, w2):
        blk = (t[:, :, cidx] + pe).reshape(B, G, nc, CMP_LEN * dh)
        return jax.nn.silu(blk @ w1) @ w2

    kcc = compress(kc, pe_k, wk1, wk2)
    vcc = compress(vc, pe_v, wv1, wv2)
    mask_c = jnp.asarray(cstart[None, :] + CMP_LEN - 1 <= pos[:, None])
    p_c = masked_softmax(jnp.einsum('bgrtd,bgnd->bgrtn', qh, kcc) * scale, mask_c)
    o_c = jnp.einsum('bgrtn,bgnd->bgrtd', p_c.astype(dt), vcc)

    nb = S // SEL_LEN
    sstart = np.arange(nb) * SEL_LEN
    overlap = ((cstart[:, None] < sstart[None, :] + SEL_LEN) & (cstart[:, None] + CMP_LEN > sstart[None, :])).astype(np.float32)
    imp = jnp.einsum('bgrtn,nj->bgtj', p_c, jnp.asarray(overlap))
    tb = pos // SEL_LEN
    jb = np.arange(nb)
    valid = jb[None, :] <= tb[:, None]
    forced = valid & ((jb[None, :] == 0) | (jb[None, :] == tb[:, None]) | (jb[None, :] == tb[:, None] - 1))
    score = jnp.where(jnp.asarray(forced), FORCE_SCORE, jnp.where(jnp.asarray(valid), imp, -1.0))
    n_sel = min(SEL_TOPK, nb)
    _, sel_idx = lax.top_k(score, n_sel)

    ks_blocks = ks.reshape(B, G, nb, SEL_LEN, dh)
    vs_blocks = vs.reshape(B, G, nb, SEL_LEN, dh)
    kw_pad = jnp.pad(kw, ((0, 0), (0, 0), (WINDOW, 0), (0, 0)))
    vw_pad = jnp.pad(vw, ((0, 0), (0, 0), (WINDOW, 0), (0, 0)))
    b_ix = jnp.arange(B)[:, None, None, None]
    g_ix = jnp.arange(G)[None, :, None, None]

    def block_fn(qb):
        s0 = qb * Q_BLOCK
        tq = s0 + jnp.arange(Q_BLOCK)
        qblk = lax.dynamic_slice_in_dim(qh, s0, Q_BLOCK, axis=3)
        idx = lax.dynamic_slice_in_dim(sel_idx, s0, Q_BLOCK, axis=2)
        kg = ks_blocks[b_ix, g_ix, idx]
        vg = vs_blocks[b_ix, g_ix, idx].reshape(B, G, Q_BLOCK, n_sel * SEL_LEN, dh)
        kpos = idx[..., None] * SEL_LEN + jnp.arange(SEL_LEN)
        m_s = (kpos <= tq[:, None, None]).reshape(B, G, 1, Q_BLOCK, n_sel * SEL_LEN)
        s_s = jnp.einsum('bgrqd,bgqnkd->bgrqnk', qblk, kg).reshape(B, G, R, Q_BLOCK, n_sel * SEL_LEN) * scale
        p_s = masked_softmax(s_s, m_s)
        o_s = jnp.einsum('bgrqm,bgqmd->bgrqd', p_s.astype(dt), vg)
        kwb = lax.dynamic_slice_in_dim(kw_pad, s0, WINDOW + Q_BLOCK, axis=2)
        vwb = lax.dynamic_slice_in_dim(vw_pad, s0, WINDOW + Q_BLOCK, axis=2)
        kp = s0 - WINDOW + jnp.arange(WINDOW + Q_BLOCK)
        dpos = tq[:, None] - kp[None, :]
        m_w = (kp[None, :] >= 0) & (dpos >= 0) & (dpos < WINDOW)
        p_w = masked_softmax(jnp.einsum('bgrqd,bgkd->bgrqk', qblk, kwb) * scale, m_w)
        o_w = jnp.einsum('bgrqk,bgkd->bgrqd', p_w.astype(dt), vwb)
        return o_s, o_w

    o_s, o_w = lax.map(block_fn, jnp.arange(S // Q_BLOCK))
    o_s = jnp.moveaxis(o_s, 0, 3).reshape(B, G, R, S, dh)
    o_w = jnp.moveaxis(o_w, 0, 3).reshape(B, G, R, S, dh)

    gl = jax.nn.sigmoid(gate_logits.reshape(B, S, G, R, 3)).transpose(0, 2, 3, 1, 4)
    o = gl[..., 0:1] * o_c + gl[..., 1:2] * o_s + gl[..., 2:3] * o_w
    return o.transpose(0, 3, 1, 2, 4).reshape(B, S, NSA_WIDTH).astype(dt)


def hgrn2_mixer(q, f, i, g, lb, norm_g):
    B, S, _ = q.shape
    dt = q.dtype
    H, dk, dv, C = HGRN_HEADS, HGRN_DK, HGRN_DV, HGRN_CHUNK
    f32 = jnp.float32
    qh = (jax.nn.silu(q.astype(f32)) * dk ** -0.5).reshape(B, S, H, dk)
    lb_h = lb.reshape(H, dk)
    log_f = jnp.logaddexp(jnp.log(lb_h), jnp.log1p(-lb_h) + jax.nn.log_sigmoid(f.astype(f32).reshape(B, S, H, dk)))
    kh = -jnp.expm1(log_f)
    vh = i.astype(f32).reshape(B, S, H, dv)
    nc = S // C

    def to_chunks(t):
        return t.reshape(B, nc, C, H, t.shape[-1]).transpose(1, 0, 3, 2, 4)

    causal = np.tril(np.ones((C, C), dtype=bool))

    def step(state, inp):
        qc, kc, vc, lfc = inp
        b = jnp.cumsum(lfc, axis=2)
        diff = jnp.where(causal[:, :, None], b[:, :, :, None, :] - b[:, :, None, :, :], -jnp.inf)
        a = jnp.sum(qc[:, :, :, None, :] * kc[:, :, None, :, :] * jnp.exp(diff), axis=-1)
        o = jnp.einsum('bhts,bhsv->bhtv', a, vc) + jnp.einsum('bhtd,bhdv->bhtv', qc * jnp.exp(b), state)
        b_last = b[:, :, -1:, :]
        new_state = jnp.exp(b_last[:, :, 0, :])[..., None] * state + jnp.einsum('bhsd,bhsv->bhdv', kc * jnp.exp(b_last - b), vc)
        return new_state, o

    state0 = jnp.zeros((B, H, dk, dv), f32)
    _, o = lax.scan(step, state0, (to_chunks(qh), to_chunks(kh), to_chunks(vh), to_chunks(log_f)))
    o = o.transpose(1, 0, 3, 2, 4).reshape(B, S, H, dv)
    o = o * lax.rsqrt(jnp.mean(jnp.square(o), axis=-1, keepdims=True) + RMS_EPS) * norm_g.astype(f32)
    o = o.reshape(B, S, HGRN_VWIDTH) * jax.nn.silu(g.astype(f32))
    return o.astype(dt)


def setup_inputs(seed: int = 0) -> dict:
    key = jax.random.key(seed)
    ks = jax.random.split(key, 23)
    L, D = DEPTH, D_MODEL

    def nrm(k, shape, scale):
        return scale * jax.random.normal(k, shape, jnp.float32)

    return {
        'x': nrm(ks[0], (BATCH, SEQ, D), 1.0),
        'c': nrm(ks[1], (BATCH, D), 1.0),
        'w_in': nrm(ks[2], (L, D, N_IN), D ** -0.5),
        'b_in': nrm(ks[3], (L, N_IN), 0.02),
        'cmp_pe_k': nrm(ks[4], (L, CMP_LEN, NSA_DH), 0.02),
        'cmp_pe_v': nrm(ks[5], (L, CMP_LEN, NSA_DH), 0.02),
        'cmp_wk1': nrm(ks[6], (L, CMP_LEN * NSA_DH, CMP_HIDDEN), (CMP_LEN * NSA_DH) ** -0.5),
        'cmp_wk2': nrm(ks[7], (L, CMP_HIDDEN, NSA_DH), CMP_HIDDEN ** -0.5),
        'cmp_wv1': nrm(ks[8], (L, CMP_LEN * NSA_DH, CMP_HIDDEN), (CMP_LEN * NSA_DH) ** -0.5),
        'cmp_wv2': nrm(ks[9], (L, CMP_HIDDEN, NSA_DH), CMP_HIDDEN ** -0.5),
        'hgrn_lb_logits': nrm(ks[10], (L, HGRN_WIDTH), 1.0),
        'hgrn_norm_g': 1.0 + nrm(ks[11], (L, HGRN_DV), 0.02),
        'w_branch_a': nrm(ks[12], (L, NSA_WIDTH, D), NSA_WIDTH ** -0.5),
        'w_branch_b': nrm(ks[13], (L, HGRN_VWIDTH, D), HGRN_VWIDTH ** -0.5),
        'w_out': nrm(ks[14], (L, D, D), DEEPNORM_BETA * D ** -0.5),
        'w_ada': nrm(ks[15], (L, D, 6 * D), 0.1 * D ** -0.5),
        'b_ada': nrm(ks[16], (L, 6 * D), 0.02),
        'ln1_g': 1.0 + nrm(ks[17], (L, D), 0.02),
        'ln1_b': nrm(ks[18], (L, D), 0.02),
        'w_mlp1': nrm(ks[19], (L, D, MLP_HIDDEN), D ** -0.5),
        'w_mlp2': nrm(ks[20], (L, MLP_HIDDEN, D), DEEPNORM_BETA * MLP_HIDDEN ** -0.5),
        'ln2_g': 1.0 + nrm(ks[21], (L, D), 0.02),
        'ln2_b': nrm(ks[22], (L, D), 0.02),
    }


def reference(x, c, w_in, b_in, cmp_pe_k, cmp_pe_v, cmp_wk1, cmp_wk2, cmp_wv1, cmp_wv2, hgrn_lb_logits, hgrn_norm_g, w_branch_a, w_branch_b, w_out, w_ada, b_ada, ln1_g, ln1_b, w_mlp1, w_mlp2, ln2_g, ln2_b):
    lb_all = jnp.cumsum(jax.nn.softmax(hgrn_lb_logits.astype(jnp.float32), axis=0), axis=0)
    lb_all = lb_all - lb_all[0:1]
    cond = jax.nn.silu(c)
    for l in range(DEPTH):
        mod = cond @ w_ada[l] + b_ada[l]
        sh1, sc1, gt1, sh2, sc2, gt2 = [m[:, None, :] for m in jnp.split(mod, 6, axis=-1)]
        u = x * (1.0 + sc1) + sh1
        h = u @ w_in[l] + b_in[l]
        q_a, k_c, v_c, k_s, v_s, k_w, v_w, g_a, q_b, f_b, i_b, g_b, gm_a, gm_b = jnp.split(h, IN_OFFSETS, axis=-1)
        y_a = nsa_mixer(q_a, k_c, v_c, k_s, v_s, k_w, v_w, g_a, cmp_pe_k[l], cmp_pe_v[l], cmp_wk1[l], cmp_wk2[l], cmp_wv1[l], cmp_wv2[l])
        y_b = hgrn2_mixer(q_b, f_b, i_b, g_b, lb_all[l], hgrn_norm_g[l])
        merged = jax.nn.sigmoid(gm_a) * (y_a @ w_branch_a[l]) + jax.nn.sigmoid(gm_b) * (y_b @ w_branch_b[l])
        y = merged @ w_out[l]
        x = layer_norm(DEEPNORM_ALPHA * x + (1.0 + gt1) * y, ln1_g[l], ln1_b[l])
        u = x * (1.0 + sc2) + sh2
        y = jnp.square(jax.nn.relu(u @ w_mlp1[l])) @ w_mlp2[l]
        x = layer_norm(DEEPNORM_ALPHA * x + (1.0 + gt2) * y, ln2_g[l], ln2_b[l])
    return x
```

```python
import functools

import numpy as np
import jax
import jax.numpy as jnp
from jax import lax
from jax.experimental import pallas as pl
from jax.experimental.pallas import tpu as pltpu

D_MODEL = 1024
DEPTH = 2
NSA_HEADS = 8
NSA_GROUPS = 2
NSA_REP = NSA_HEADS // NSA_GROUPS
NSA_DH = 64
NSA_WIDTH = NSA_HEADS * NSA_DH
NSA_KV_WIDTH = NSA_GROUPS * NSA_DH
CMP_LEN = 32
CMP_STRIDE = 16
CMP_HIDDEN = 2 * NSA_DH
SEL_LEN = 64
SEL_TOPK = 8
FORCE_SCORE = 1.0e4
WINDOW = 512
HGRN_HEADS = 4
HGRN_DK = 128
HGRN_DV = 128
HGRN_WIDTH = HGRN_HEADS * HGRN_DK
HGRN_VWIDTH = HGRN_HEADS * HGRN_DV
MLP_HIDDEN = 4 * D_MODEL
ROPE_THETA = 10000.0
LN_EPS = 1e-5
RMS_EPS = 1e-6
DEEPNORM_ALPHA = (2 * DEPTH) ** 0.25
IN_SIZES = (NSA_WIDTH,) + (NSA_KV_WIDTH,) * 6 + (3 * NSA_HEADS,) + (HGRN_WIDTH, HGRN_WIDTH, HGRN_VWIDTH, HGRN_VWIDTH) + (D_MODEL, D_MODEL)
IN_OFFSETS = [0] + [int(v) for v in np.cumsum(IN_SIZES)]

LANES = 128
VMEM_LIMIT = 48 * 1024 * 1024
TOKEN_TILE = 512
Q_TILE = 128
K_CHUNK = 128
HGRN_CHUNK = 64
HGRN_SUB = 8
NEG_BIG = -1e30

F32 = jnp.float32
BF16 = jnp.bfloat16


def _cparams(n_grid):
    return pltpu.CompilerParams(dimension_semantics=("arbitrary",) * n_grid, vmem_limit_bytes=VMEM_LIMIT)


def _resident(shape):
    nd = len(shape)
    return pl.BlockSpec(shape, lambda *_: (0,) * nd, pipeline_mode=pl.Buffered(1))


def _dot(a, b):
    return jnp.dot(a, b, preferred_element_type=F32)


def _dot_nt(a, b):
    return lax.dot_general(a, b, (((1,), (1,)), ((), ())), preferred_element_type=F32)


def _dot_tn(a, b):
    return lax.dot_general(a, b, (((0,), (0,)), ((), ())), preferred_element_type=F32)


def _sigmoid(x):
    return 1.0 / (1.0 + jnp.exp(-x))


def _silu(x):
    return x * _sigmoid(x)


def _layer_norm(z, g, b):
    mu = jnp.mean(z, axis=-1, keepdims=True)
    zc = z - mu
    var = jnp.mean(zc * zc, axis=-1, keepdims=True)
    return zc * lax.rsqrt(var + LN_EPS) * g + b


def _adaln_kernel(c_ref, w_ref, b_ref, o_ref):
    cond = _silu(c_ref[...]).astype(BF16)
    o_ref[0] = _dot(cond, w_ref[0]) + b_ref[0]


def adaln_mod(c, w_ada, b_ada):
    L, D, N = w_ada.shape
    B = c.shape[0]
    tn = D
    return pl.pallas_call(
        _adaln_kernel,
        grid=(L, N // tn),
        in_specs=[
            pl.BlockSpec((B, D), lambda l, j: (0, 0)),
            pl.BlockSpec((1, D, tn), lambda l, j: (l, 0, j)),
            pl.BlockSpec((1, 1, tn), lambda l, j: (l, 0, j)),
        ],
        out_specs=pl.BlockSpec((1, B, tn), lambda l, j: (l, 0, j)),
        out_shape=jax.ShapeDtypeStruct((L, B, N), F32),
        compiler_params=_cparams(2),
        name="adaln_mod",
    )(c, w_ada.astype(BF16), b_ada.reshape(L, 1, N))


N_ROPE = NSA_WIDTH + 3 * NSA_KV_WIDTH
N_NSA = N_ROPE + 3 * NSA_KV_WIDTH
N_GATE = NSA_GROUPS * LANES
N_HGRN = 2 * HGRN_WIDTH + 2 * HGRN_VWIDTH
N_MERGE = 2 * D_MODEL


def _in_proj_kernel(x_ref, sc_ref, sh_ref, cos_ref, sin_ref, lb_ref,
                    wn_ref, bn_ref, wg_ref, bg_ref, wh_ref, bh_ref, wm_ref, bm_ref,
                    q_ref, kv_ref, ga_ref, hq_ref, hlf_ref, hv_ref, hg_ref, gma_ref, gmb_ref):
    u = (x_ref[0] * (1.0 + sc_ref[0]) + sh_ref[0]).astype(BF16)
    cos = cos_ref[...]
    sin = sin_ref[...]
    first_half = (lax.broadcasted_iota(jnp.int32, cos.shape, 1) % NSA_DH) < (NSA_DH // 2)

    for i in range(N_ROPE // LANES):
        t = _dot(u, wn_ref[:, i * LANES:(i + 1) * LANES]) + bn_ref[:, i * LANES:(i + 1) * LANES]
        rot = jnp.where(first_half, pltpu.roll(t, LANES - NSA_DH // 2, 1), pltpu.roll(t, NSA_DH // 2, 1))
        r = t * cos + rot * sin
        for j in range(2):
            h = 2 * i + j
            piece = r[:, j * NSA_DH:(j + 1) * NSA_DH]
            if h < NSA_HEADS:
                q_ref[0, h] = (piece * (NSA_DH ** -0.5)).astype(BF16)
            else:
                kv_ref[0, h - NSA_HEADS] = piece.astype(BF16)
    for i in range(N_ROPE // LANES, N_NSA // LANES):
        t = _dot(u, wn_ref[:, i * LANES:(i + 1) * LANES]) + bn_ref[:, i * LANES:(i + 1) * LANES]
        for j in range(2):
            kv_ref[0, 2 * i + j - NSA_HEADS] = t[:, j * NSA_DH:(j + 1) * NSA_DH].astype(BF16)
    ga_ref[0] = _sigmoid(_dot(u, wg_ref[...]) + bg_ref[...])
    W = HGRN_WIDTH
    hq = _dot(u, wh_ref[:, 0:W]) + bh_ref[:, 0:W]
    hq_ref[0] = (_silu(hq) * (HGRN_DK ** -0.5)).astype(BF16)
    z = _dot(u, wh_ref[:, W:2 * W]) + bh_ref[:, W:2 * W]
    lb = lb_ref[...]
    log_sig = jnp.minimum(z, 0.0) - jnp.log1p(jnp.exp(-jnp.abs(z)))
    a = jnp.log(lb)
    bb = jnp.log1p(-lb) + log_sig
    hlf_ref[0] = jnp.maximum(a, bb) + jnp.log1p(jnp.exp(-jnp.abs(a - bb)))
    hv_ref[0] = (_dot(u, wh_ref[:, 2 * W:3 * W]) + bh_ref[:, 2 * W:3 * W]).astype(BF16)
    hg_ref[0] = _silu(_dot(u, wh_ref[:, 3 * W:4 * W]) + bh_ref[:, 3 * W:4 * W]).astype(BF16)
    for i, ref in enumerate((gma_ref, gmb_ref)):
        for j in range(2):
            c0 = i * D_MODEL + j * (D_MODEL // 2)
            c1 = c0 + D_MODEL // 2
            ref[0, :, j * (D_MODEL // 2):(j + 1) * (D_MODEL // 2)] = _sigmoid(
                _dot(u, wm_ref[:, c0:c1]) + bm_ref[:, c0:c1]).astype(BF16)


def in_proj(x, sc, sh, cos_t, sin_t, lb, wts):
    B, S, D = x.shape
    tm = min(TOKEN_TILE, S)
    wn, bn, wg, bg, wh, bh, wm, bm = wts
    tok = lambda n: pl.BlockSpec((1, tm, n), lambda b, i: (b, i, 0))
    per_b = pl.BlockSpec((1, 1, D), lambda b, i: (b, 0, 0))
    tab = pl.BlockSpec((tm, LANES), lambda b, i: (i, 0))
    out_shape = (
        jax.ShapeDtypeStruct((B, NSA_HEADS, S, NSA_DH), BF16),
        jax.ShapeDtypeStruct((B, 6 * NSA_GROUPS, S, NSA_DH), BF16),
        jax.ShapeDtypeStruct((B, S, N_GATE), F32),
        jax.ShapeDtypeStruct((B, S, HGRN_WIDTH), BF16),
        jax.ShapeDtypeStruct((B, S, HGRN_WIDTH), F32),
        jax.ShapeDtypeStruct((B, S, HGRN_VWIDTH), BF16),
        jax.ShapeDtypeStruct((B, S, HGRN_VWIDTH), BF16),
        jax.ShapeDtypeStruct((B, S, D), BF16),
        jax.ShapeDtypeStruct((B, S, D), BF16),
    )
    out_specs = (
        pl.BlockSpec((1, NSA_HEADS, tm, NSA_DH), lambda b, i: (b, 0, i, 0)),
        pl.BlockSpec((1, 6 * NSA_GROUPS, tm, NSA_DH), lambda b, i: (b, 0, i, 0)),
        tok(N_GATE), tok(HGRN_WIDTH), tok(HGRN_WIDTH), tok(HGRN_VWIDTH), tok(HGRN_VWIDTH), tok(D), tok(D),
    )
    return pl.pallas_call(
        _in_proj_kernel,
        grid=(B, S // tm),
        in_specs=[tok(D), per_b, per_b, tab, tab, _resident(lb.shape),
                  _resident(wn.shape), _resident(bn.shape), _resident(wg.shape), _resident(bg.shape),
                  _resident(wh.shape), _resident(bh.shape), _resident(wm.shape), _resident(bm.shape)],
        out_specs=out_specs,
        out_shape=out_shape,
        compiler_params=_cparams(2),
        name="in_proj",
    )(x, sc, sh, cos_t, sin_t, lb, wn, bn, wg, bg, wh, bh, wm, bm)


def _prep_in_proj_weights(w_in_l, b_in_l):
    o = IN_OFFSETS
    col = lambda i: (w_in_l[:, o[i]:o[i + 1]], b_in_l[o[i]:o[i + 1]])
    q_a, k_c, v_c, k_s, v_s, k_w, v_w, g_a, q_b, f_b, i_b, g_b, gm_a, gm_b = [col(i) for i in range(14)]

    def cat(parts):
        return (jnp.concatenate([p[0] for p in parts], axis=1).astype(BF16),
                jnp.concatenate([p[1] for p in parts], axis=0).reshape(1, -1).astype(F32))

    wn, bn = cat([q_a, k_c, k_s, k_w, v_c, v_s, v_w])
    per_group = 3 * NSA_REP
    gw = jnp.zeros((w_in_l.shape[0], N_GATE), w_in_l.dtype)
    gb = jnp.zeros((N_GATE,), b_in_l.dtype)
    for g in range(NSA_GROUPS):
        gw = gw.at[:, g * LANES:g * LANES + per_group].set(g_a[0][:, g * per_group:(g + 1) * per_group])
        gb = gb.at[g * LANES:g * LANES + per_group].set(g_a[1][g * per_group:(g + 1) * per_group])
    wg, bg = gw.astype(BF16), gb.reshape(1, -1).astype(F32)
    wh, bh = cat([q_b, f_b, i_b, g_b])
    wm, bm = cat([gm_a, gm_b])
    return wn, bn, wg, bg, wh, bh, wm, bm


def _compress_kernel(t_ref, pe_ref, w1_ref, w2_ref, o_ref):
    half = CMP_STRIDE * NSA_DH
    t = t_ref[0, 0]
    nrow = t.shape[0]
    a = _dot(t, w1_ref[0, 0:half, :])
    b = _dot(t, w1_ref[0, half:2 * half, :])
    pe = jnp.broadcast_to(pe_ref[0], (8, 2 * half)).astype(BF16)
    c = _dot(pe, w1_ref[0])[0:1]
    h = a + pltpu.roll(b, nrow - 1, 0) + c
    o_ref[0, 0] = _dot(_silu(h).astype(BF16), w2_ref[0]).astype(BF16)


def nsa_compress(kv, pe, w1, w2):
    B, _, S, dh = kv.shape
    nrow = S // CMP_STRIDE
    kv_rows = kv.reshape(B, 6 * NSA_GROUPS, nrow, CMP_STRIDE * dh)
    G = NSA_GROUPS
    return pl.pallas_call(
        _compress_kernel,
        grid=(B, 2, G),
        in_specs=[
            pl.BlockSpec((1, 1, nrow, CMP_STRIDE * dh), lambda b, s, g: (b, s * 3 * G + g, 0, 0)),
            pl.BlockSpec((1, 1, CMP_LEN * dh), lambda b, s, g: (s, 0, 0)),
            pl.BlockSpec((1, CMP_LEN * dh, CMP_HIDDEN), lambda b, s, g: (s, 0, 0)),
            pl.BlockSpec((1, CMP_HIDDEN, dh), lambda b, s, g: (s, 0, 0)),
        ],
        out_specs=pl.BlockSpec((1, 1, nrow, dh), lambda b, s, g: (b, s * G + g, 0, 0)),
        out_shape=jax.ShapeDtypeStruct((B, 2 * G, nrow, dh), BF16),
        compiler_params=_cparams(3),
        name="nsa_compress",
    )(kv_rows, pe, w1, w2)


def _nsa_kernel(q_ref, kc_ref, vc_ref, ks_ref, vs_ref, kw_ref, vw_ref, ga_ref, ov_ref, ex_ref,
                o_ref, sel_ref, m_ref, l_ref, acc_ref, *, n_sel):
    R, TQ, dh = NSA_REP, Q_TILE, NSA_DH
    rows = R * TQ
    qb = pl.program_id(2)
    s0 = qb * TQ
    q = q_ref[0].reshape(rows, dh)
    t_row = s0 + (lax.broadcasted_iota(jnp.int32, (rows, 1), 0) % TQ)

    ncb = kc_ref.shape[2]
    sc = _dot_nt(q, kc_ref[0, 0])
    n_lane = lax.broadcasted_iota(jnp.int32, (rows, ncb), 1)
    mask_c = n_lane * CMP_STRIDE + (CMP_LEN - 1) <= t_row
    sc = jnp.where(mask_c, sc, NEG_BIG)
    mc = jnp.max(sc, axis=-1, keepdims=True)
    ec = jnp.where(mask_c, jnp.exp(sc - mc), 0.0)
    pc = ec / jnp.maximum(jnp.sum(ec, axis=-1, keepdims=True), 1e-30)
    o_c = _dot(pc.astype(BF16), vc_ref[0, 0])

    psum = pc[0:TQ]
    for r in range(1, R):
        psum = psum + pc[r * TQ:(r + 1) * TQ]
    p_hi = psum.astype(BF16)
    p_lo = (psum - p_hi.astype(F32)).astype(BF16)
    imp = _dot(p_hi, ov_ref[...]) + _dot(p_lo, ov_ref[...])
    nb = imp.shape[1]
    jb = lax.broadcasted_iota(jnp.int32, (TQ, nb), 1)
    tb = (s0 + lax.broadcasted_iota(jnp.int32, (TQ, nb), 0)) // SEL_LEN
    valid = jb <= tb
    forced = jnp.where(valid, jnp.where(jb == 0, 1.0, jnp.where(jb >= tb - 1, 1.0, 0.0)), 0.0)
    score = jnp.where(forced > 0.5, FORCE_SCORE, jnp.where(valid, imp, -1.0))
    rank = jnp.zeros((TQ, nb), F32)
    for i in range(nb):
        si = score[:, i:i + 1]
        tie_first = jnp.where(jb > i, 1.0, 0.0)
        rank = rank + jnp.where(si > score, 1.0, jnp.where(si == score, tie_first, 0.0))
    sel = jnp.where(rank < n_sel, 1.0, 0.0).astype(BF16)
    selx = _dot(sel, ex_ref[...])
    for c in range(selx.shape[1] // K_CHUNK):
        sel_ref[c] = selx[:, c * K_CHUNK:(c + 1) * K_CHUNK]

    lane = lax.broadcasted_iota(jnp.int32, (rows, K_CHUNK), 1)

    def reset():
        m_ref[...] = jnp.full(m_ref.shape, NEG_BIG, F32)
        l_ref[...] = jnp.zeros(l_ref.shape, F32)
        acc_ref[...] = jnp.zeros(acc_ref.shape, F32)

    def online_step(k, v, mask):
        s = jnp.where(mask, _dot_nt(q, k), NEG_BIG)
        m_prev = m_ref[...]
        m_new = jnp.maximum(m_prev, jnp.max(s, axis=-1, keepdims=True))
        alpha = jnp.exp(m_prev - m_new)
        p = jnp.where(mask, jnp.exp(s - m_new), 0.0)
        l_ref[...] = alpha * l_ref[...] + jnp.sum(p, axis=-1, keepdims=True)
        acc_ref[...] = alpha[:, 0:dh] * acc_ref[...] + _dot(p.astype(BF16), v)
        m_ref[...] = m_new

    def result():
        return acc_ref[...] / l_ref[:, 0:dh]

    reset()

    def sel_body(c, carry):
        k0 = pl.multiple_of(c * K_CHUNK, K_CHUNK)
        chosen = jnp.concatenate([sel_ref[c]] * R, axis=0) > 0.5
        mask = chosen & ((k0 + lane) <= t_row)
        online_step(ks_ref[0, 0, pl.ds(k0, K_CHUNK), :], vs_ref[0, 0, pl.ds(k0, K_CHUNK), :], mask)
        return carry

    lax.fori_loop(0, (s0 + TQ) // K_CHUNK, sel_body, 0)
    o_s = result()

    reset()
    for i in range(WINDOW // K_CHUNK + TQ // K_CHUNK):
        k0 = s0 - WINDOW + i * K_CHUNK

        @pl.when(k0 >= 0)
        def _():
            k0a = pl.multiple_of(jnp.maximum(k0, 0), K_CHUNK)
            dpos = t_row - (k0a + lane)
            mask = (dpos >= 0) & (dpos < WINDOW)
            online_step(kw_ref[0, 0, pl.ds(k0a, K_CHUNK), :], vw_ref[0, 0, pl.ds(k0a, K_CHUNK), :], mask)

    o_w = result()

    gate = ga_ref[0]
    pieces = []
    for r in range(R):
        sl = slice(r * TQ, (r + 1) * TQ)
        pieces.append(gate[:, 3 * r:3 * r + 1] * o_c[sl] + gate[:, 3 * r + 1:3 * r + 2] * o_s[sl]
                      + gate[:, 3 * r + 2:3 * r + 3] * o_w[sl])
    o_ref[0] = jnp.concatenate(pieces, axis=1).astype(BF16)


def nsa_attend(q, kv, cmp, gates):
    B, H, S, dh = q.shape
    G, R = NSA_GROUPS, NSA_REP
    ncb = S // CMP_STRIDE
    nb = S // SEL_LEN
    assert S % Q_TILE == 0 and Q_TILE % K_CHUNK == 0 and WINDOW % K_CHUNK == 0 and K_CHUNK % SEL_LEN == 0
    cstart = np.arange(ncb) * CMP_STRIDE
    sstart = np.arange(nb) * SEL_LEN
    overlap = ((cstart[:, None] < sstart[None, :] + SEL_LEN) & (cstart[:, None] + CMP_LEN > sstart[None, :]))
    expand = (np.arange(S)[None, :] // SEL_LEN) == np.arange(nb)[:, None]
    ov = jnp.asarray(overlap, BF16)
    ex = jnp.asarray(expand, BF16)
    kv_spec = lambda idx: pl.BlockSpec((1, 1, S, dh), lambda b, g, i, idx=idx: (b, idx * G + g, 0, 0))
    cmp_spec = lambda idx: pl.BlockSpec((1, 1, ncb, dh), lambda b, g, i, idx=idx: (b, idx * G + g, 0, 0))
    rows = R * Q_TILE
    return pl.pallas_call(
        functools.partial(_nsa_kernel, n_sel=min(SEL_TOPK, nb)),
        grid=(B, G, S // Q_TILE),
        in_specs=[
            pl.BlockSpec((1, R, Q_TILE, dh), lambda b, g, i: (b, g, i, 0)),
            cmp_spec(0), cmp_spec(1),
            kv_spec(1), kv_spec(4), kv_spec(2), kv_spec(5),
            pl.BlockSpec((1, Q_TILE, LANES), lambda b, g, i: (b, i, g)),
            pl.BlockSpec(ov.shape, lambda b, g, i: (0, 0)),
            pl.BlockSpec(ex.shape, lambda b, g, i: (0, 0)),
        ],
        out_specs=pl.BlockSpec((1, Q_TILE, R * dh), lambda b, g, i: (b, i, g)),
        out_shape=jax.ShapeDtypeStruct((B, S, H * dh), BF16),
        scratch_shapes=[
            pltpu.VMEM((S // K_CHUNK, Q_TILE, K_CHUNK), F32),
            pltpu.VMEM((rows, K_CHUNK), F32),
            pltpu.VMEM((rows, K_CHUNK), F32),
            pltpu.VMEM((rows, dh), F32),
        ],
        compiler_params=_cparams(3),
        name="nsa_attend",
    )(q, cmp, cmp, kv, kv, kv, kv, gates, ov, ex)


def _hgrn_kernel(q_ref, lf_ref, v_ref, g_ref, ng_ref, tri_ref, o_ref, st_ref):
    C, SB = HGRN_CHUNK, HGRN_SUB
    NBK = C // SB
    S = q_ref.shape[1]
    st_ref[...] = jnp.zeros(st_ref.shape, F32)
    row = lax.broadcasted_iota(jnp.int32, (C, HGRN_DK), 0)
    sub_row = lax.broadcasted_iota(jnp.int32, (SB, 1), 0)

    def chunk(ci, carry):
        r0 = pl.multiple_of(ci * C, C)
        q = q_ref[0, pl.ds(r0, C), :].astype(F32)
        lf = lf_ref[0, pl.ds(r0, C), :]
        v_bf = v_ref[0, pl.ds(r0, C), :]
        v = v_bf.astype(F32)
        kh = 1.0 - jnp.exp(lf)
        tri = tri_ref[...]
        lf0 = lf.astype(BF16)
        lf1 = (lf - lf0.astype(F32)).astype(BF16)
        lf2 = (lf - lf0.astype(F32) - lf1.astype(F32)).astype(BF16)
        b = _dot(tri, lf0) + _dot(tri, lf1) + _dot(tri, lf2)
        b_last = b[C - 1:C, :]
        st = st_ref[...]
        o = _dot_nt((q * jnp.exp(b)).astype(BF16), st.astype(BF16))
        b_end = jnp.concatenate(
            [jnp.broadcast_to(b[(j + 1) * SB - 1:(j + 1) * SB, :], (SB, HGRN_DK)) for j in range(NBK)], axis=0)
        k_end = kh * jnp.exp(b_end - b)
        q_parts, k_parts = [], []
        for j in range(NBK - 1):
            lo = (j + 1) * SB
            qj = q[lo:] * jnp.exp(b[lo:] - b[lo - 1:lo, :])
            q_parts.append(jnp.concatenate([jnp.zeros((lo, HGRN_DK), F32), qj], axis=0))
            k_parts.append(jnp.where((row >= j * SB) & (row < lo), k_end, 0.0))
        q_cat = jnp.concatenate(q_parts, axis=1).astype(BF16)
        k_cat = jnp.concatenate(k_parts, axis=1).astype(BF16)
        a_off = _dot_nt(q_cat, k_cat)
        o = o + _dot(a_off.astype(BF16), v_bf)
        diag = []
        for j in range(NBK):
            sl = slice(j * SB, (j + 1) * SB)
            qj, bj, kj, vj = q[sl], b[sl], kh[sl], v[sl]
            oj = jnp.zeros((SB, HGRN_DV), F32)
            for s in range(SB):
                w = jnp.exp(jnp.minimum(bj - bj[s:s + 1, :], 0.0))
                a = jnp.sum(qj * kj[s:s + 1, :] * w, axis=-1, keepdims=True)
                a = jnp.where(sub_row >= s, a, 0.0)
                oj = oj + a * vj[s:s + 1, :]
            diag.append(oj)
        o = o + jnp.concatenate(diag, axis=0)
        k_last = (kh * jnp.exp(b_last - b)).astype(BF16)
        st_ref[...] = st * jnp.exp(b_last) + _dot_tn(v_bf, k_last)
        o = o * lax.rsqrt(jnp.mean(o * o, axis=-1, keepdims=True) + RMS_EPS) * ng_ref[...]
        o_ref[0, pl.ds(r0, C), :] = (o * g_ref[0, pl.ds(r0, C), :].astype(F32)).astype(BF16)
        return carry

    lax.fori_loop(0, S // C, chunk, 0)


def hgrn2(hq, hlf, hv, hg, norm_g):
    B, S, _ = hq.shape
    C = HGRN_CHUNK
    assert S % C == 0
    tri = jnp.asarray(np.tril(np.ones((C, C))), BF16)
    spec = lambda n: pl.BlockSpec((1, S, n), lambda b, h: (b, 0, h))
    return pl.pallas_call(
        _hgrn_kernel,
        grid=(B, HGRN_HEADS),
        in_specs=[spec(HGRN_DK), spec(HGRN_DK), spec(HGRN_DV), spec(HGRN_DV),
                  pl.BlockSpec((1, HGRN_DV), lambda b, h: (0, 0)),
                  pl.BlockSpec((C, C), lambda b, h: (0, 0))],
        out_specs=spec(HGRN_DV),
        out_shape=jax.ShapeDtypeStruct((B, S, HGRN_VWIDTH), BF16),
        scratch_shapes=[pltpu.VMEM((HGRN_DV, HGRN_DK), F32)],
        compiler_params=_cparams(2),
        name="hgrn2",
    )(hq, hlf, hv, hg, norm_g.reshape(1, HGRN_DV).astype(F32), tri)


def _merge_kernel(ya_ref, yb_ref, gma_ref, gmb_ref, x_ref, gt_ref, lg_ref, lbias_ref,
                  wa_ref, wb_ref, wo_ref, o_ref):
    pa = _dot(ya_ref[0], wa_ref[...])
    pb = _dot(yb_ref[0], wb_ref[...])
    merged = gma_ref[0].astype(F32) * pa + gmb_ref[0].astype(F32) * pb
    y = _dot(merged.astype(BF16), wo_ref[...])
    z = DEEPNORM_ALPHA * x_ref[0] + (1.0 + gt_ref[0]) * y
    o_ref[0] = _layer_norm(z, lg_ref[...], lbias_ref[...])


def merge_out(ya, yb, gma, gmb, x, gt, ln_g, ln_b, wa, wb, wo):
    B, S, D = x.shape
    tm = min(TOKEN_TILE, S)
    tok = lambda n: pl.BlockSpec((1, tm, n), lambda b, i: (b, i, 0))
    return pl.pallas_call(
        _merge_kernel,
        grid=(B, S // tm),
        in_specs=[tok(NSA_WIDTH), tok(HGRN_VWIDTH), tok(D), tok(D), tok(D),
                  pl.BlockSpec((1, 1, D), lambda b, i: (b, 0, 0)),
                  _resident((1, D)), _resident((1, D)),
                  _resident(wa.shape), _resident(wb.shape), _resident(wo.shape)],
        out_specs=tok(D),
        out_shape=jax.ShapeDtypeStruct((B, S, D), F32),
        compiler_params=_cparams(2),
        name="merge_out",
    )(ya, yb, gma, gmb, x, gt, ln_g.reshape(1, D), ln_b.reshape(1, D), wa, wb, wo)


MLP_COLS = 1024


def _mlp_kernel(x_ref, sc_ref, sh_ref, gt_ref, lg_ref, lbias_ref, w1_ref, w2_ref, o_ref):
    x = x_ref[0]
    u = (x * (1.0 + sc_ref[0]) + sh_ref[0]).astype(BF16)
    y = jnp.zeros(x.shape, F32)
    for c in range(MLP_HIDDEN // MLP_COLS):
        h = jnp.maximum(_dot(u, w1_ref[:, c * MLP_COLS:(c + 1) * MLP_COLS]), 0.0)
        y = y + _dot((h * h).astype(BF16), w2_ref[c * MLP_COLS:(c + 1) * MLP_COLS, :])
    z = DEEPNORM_ALPHA * x + (1.0 + gt_ref[0]) * y
    o_ref[0] = _layer_norm(z, lg_ref[...], lbias_ref[...])


def mlp(x, sc, sh, gt, ln_g, ln_b, w1, w2):
    B, S, D = x.shape
    tm = min(TOKEN_TILE, S)
    tok = pl.BlockSpec((1, tm, D), lambda b, i: (b, i, 0))
    per_b = pl.BlockSpec((1, 1, D), lambda b, i: (b, 0, 0))
    return pl.pallas_call(
        _mlp_kernel,
        grid=(B, S // tm),
        in_specs=[tok, per_b, per_b, per_b, _resident((1, D)), _resident((1, D)),
                  _resident(w1.shape), _resident(w2.shape)],
        out_specs=tok,
        out_shape=jax.ShapeDtypeStruct((B, S, D), F32),
        compiler_params=_cparams(2),
        name="mlp",
    )(x, sc, sh, gt, ln_g.reshape(1, D), ln_b.reshape(1, D), w1, w2)


def _rope_tables(S):
    half = NSA_DH // 2
    inv = 1.0 / (ROPE_THETA ** (jnp.arange(0, NSA_DH, 2, dtype=F32) / NSA_DH))
    ang = jnp.arange(S, dtype=F32)[:, None] * inv[None, :]
    cos, sin = jnp.cos(ang), jnp.sin(ang)
    reps = LANES // NSA_DH
    return (jnp.tile(jnp.concatenate([cos, cos], axis=1), (1, reps)),
            jnp.tile(jnp.concatenate([-sin, sin], axis=1), (1, reps)))


def kernel(x, c, w_in, b_in, cmp_pe_k, cmp_pe_v, cmp_wk1, cmp_wk2, cmp_wv1, cmp_wv2, hgrn_lb_logits, hgrn_norm_g, w_branch_a, w_branch_b, w_out, w_ada, b_ada, ln1_g, ln1_b, w_mlp1, w_mlp2, ln2_g, ln2_b):
    B, S, D = x.shape
    lb_all = jnp.cumsum(jax.nn.softmax(hgrn_lb_logits.astype(F32), axis=0), axis=0)
    lb_all = lb_all - lb_all[0:1]
    cos_t, sin_t = _rope_tables(S)
    mod = adaln_mod(c, w_ada, b_ada)
    for l in range(DEPTH):
        sh1, sc1, gt1, sh2, sc2, gt2 = [mod[l, :, None, i * D:(i + 1) * D] for i in range(6)]
        wts = _prep_in_proj_weights(w_in[l], b_in[l])
        q, kv, ga, hq, hlf, hv, hg, gma, gmb = in_proj(x, sc1, sh1, cos_t, sin_t, lb_all[l].reshape(1, -1), wts)
        pe = jnp.stack([cmp_pe_k[l].reshape(1, -1), cmp_pe_v[l].reshape(1, -1)])
        w1 = jnp.stack([cmp_wk1[l], cmp_wv1[l]]).astype(BF16)
        w2 = jnp.stack([cmp_wk2[l], cmp_wv2[l]]).astype(BF16)
        cmp = nsa_compress(kv, pe, w1, w2)
        ya = nsa_attend(q, kv, cmp, ga)
        yb = hgrn2(hq, hlf, hv, hg, hgrn_norm_g[l])
        x = merge_out(ya, yb, gma, gmb, x, gt1, ln1_g[l], ln1_b[l],
                      w_branch_a[l].astype(BF16), w_branch_b[l].astype(BF16), w_out[l].astype(BF16))
        x = mlp(x, sc2, sh2, gt2, ln2_g[l], ln2_b[l], w_mlp1[l].astype(BF16), w_mlp2[l].astype(BF16))
    return x
```

```python
import functools

import numpy as np
import jax
import jax.numpy as jnp
from jax import lax
from jax.experimental import pallas as pl
from jax.experimental.pallas import tpu as pltpu

D_MODEL = 1024
DEPTH = 2
NSA_HEADS = 8
NSA_GROUPS = 2
NSA_REP = NSA_HEADS // NSA_GROUPS
NSA_DH = 64
NSA_WIDTH = NSA_HEADS * NSA_DH
NSA_KV_WIDTH = NSA_GROUPS * NSA_DH
CMP_LEN = 32
CMP_STRIDE = 16
CMP_HIDDEN = 2 * NSA_DH
SEL_LEN = 64
SEL_TOPK = 8
FORCE_SCORE = 1.0e4
WINDOW = 512
HGRN_HEADS = 4
HGRN_DK = 128
HGRN_DV = 128
HGRN_WIDTH = HGRN_HEADS * HGRN_DK
HGRN_VWIDTH = HGRN_HEADS * HGRN_DV
MLP_HIDDEN = 4 * D_MODEL
ROPE_THETA = 10000.0
LN_EPS = 1e-5
RMS_EPS = 1e-6
DEEPNORM_ALPHA = (2 * DEPTH) ** 0.25
IN_SIZES = (NSA_WIDTH,) + (NSA_KV_WIDTH,) * 6 + (3 * NSA_HEADS,) + (HGRN_WIDTH, HGRN_WIDTH, HGRN_VWIDTH, HGRN_VWIDTH) + (D_MODEL, D_MODEL)
IN_OFFSETS = [0] + [int(v) for v in np.cumsum(IN_SIZES)]

LANES = 128
SUBLANES = 8
VMEM_LIMIT = 48 * 1024 * 1024
TOKEN_TILE = 512
Q_TILE = 128
K_CHUNK = 128
SEL_STEP = 4
HGRN_CHUNK = 64
HGRN_SUB = 8
NEG_BIG = -1e30
LOG2E = 1.4426950408889634
Q_SCALE = NSA_DH ** -0.5 * LOG2E

F32 = jnp.float32
BF16 = jnp.bfloat16


def _cparams(n_grid):
    return pltpu.CompilerParams(dimension_semantics=("arbitrary",) * n_grid, vmem_limit_bytes=VMEM_LIMIT)


def _resident(shape):
    nd = len(shape)
    return pl.BlockSpec(shape, lambda *_: (0,) * nd, pipeline_mode=pl.Buffered(1))


def _dot(a, b):
    return jnp.dot(a, b, preferred_element_type=F32)


def _dot_nt(a, b):
    return lax.dot_general(a, b, (((1,), (1,)), ((), ())), preferred_element_type=F32)


def _dot_tn(a, b):
    return lax.dot_general(a, b, (((0,), (0,)), ((), ())), preferred_element_type=F32)


def _sigmoid(x):
    return 1.0 / (1.0 + jnp.exp(-x))


def _silu(x):
    return x * _sigmoid(x)


def _layer_norm(z, g, b):
    mu = jnp.mean(z, axis=-1, keepdims=True)
    zc = z - mu
    var = jnp.mean(zc * zc, axis=-1, keepdims=True)
    return zc * lax.rsqrt(var + LN_EPS) * g + b


def _adaln_kernel(c_ref, w_ref, b_ref, o_ref):
    cond = _silu(c_ref[...]).astype(BF16)
    o_ref[0] = _dot(cond, w_ref[0]) + b_ref[0]


def adaln_mod(c, w_ada, b_ada):
    L, D, N = w_ada.shape
    B = c.shape[0]
    tn = D
    return pl.pallas_call(
        _adaln_kernel,
        grid=(L, N // tn),
        in_specs=[
            pl.BlockSpec((B, D), lambda l, j: (0, 0)),
            pl.BlockSpec((1, D, tn), lambda l, j: (l, 0, j)),
            pl.BlockSpec((1, 1, tn), lambda l, j: (l, 0, j)),
        ],
        out_specs=pl.BlockSpec((1, B, tn), lambda l, j: (l, 0, j)),
        out_shape=jax.ShapeDtypeStruct((L, B, N), F32),
        compiler_params=_cparams(2),
        name="adaln_mod",
    )(c, w_ada.astype(BF16), b_ada.reshape(L, 1, N))


N_ROPE = NSA_WIDTH + 3 * NSA_KV_WIDTH
N_NSA = N_ROPE + 3 * NSA_KV_WIDTH
N_GATE = NSA_GROUPS * LANES
N_HGRN = 2 * HGRN_WIDTH + 2 * HGRN_VWIDTH
N_MERGE = 2 * D_MODEL


def _in_proj_kernel(x_ref, sc_ref, sh_ref, cos_ref, sin_ref, lb_ref,
                    wn_ref, bn_ref, wg_ref, bg_ref, wh_ref, bh_ref, wm_ref, bm_ref,
                    q_ref, kx_ref, kv_ref, ga_ref, hq_ref, hlf_ref, hv_ref, hg_ref, gma_ref, gmb_ref):
    u = (x_ref[0] * (1.0 + sc_ref[0]) + sh_ref[0]).astype(BF16)
    tm = u.shape[0]
    cos = cos_ref[...]
    sin = sin_ref[...]
    lane = lax.broadcasted_iota(jnp.int32, cos.shape, 1)
    first_half = (lane % NSA_DH) < (NSA_DH // 2)
    low = lane < NSA_DH
    pos = pl.program_id(1) * tm + lax.broadcasted_iota(jnp.int32, cos.shape, 0)
    block_onehot = jnp.where(lane - NSA_DH == pos // SEL_LEN, 1.0, 0.0)

    def heads(t, upper):
        return jnp.where(low, t, upper), jnp.where(low, pltpu.roll(t, NSA_DH, 1), upper)

    for i in range(N_ROPE // LANES):
        t = _dot(u, wn_ref[:, i * LANES:(i + 1) * LANES]) + bn_ref[:, i * LANES:(i + 1) * LANES]
        rot = jnp.where(first_half, pltpu.roll(t, LANES - NSA_DH // 2, 1), pltpu.roll(t, NSA_DH // 2, 1))
        r = t * cos + rot * sin
        if i < NSA_HEADS // 2:
            for j, piece in enumerate(heads(r * Q_SCALE, 0.0)):
                q_ref[0, 2 * i + j] = piece.astype(BF16)
        elif i == NSA_HEADS // 2:
            for j in range(2):
                kv_ref[0, j] = r[:, j * NSA_DH:(j + 1) * NSA_DH].astype(BF16)
        else:
            kind = i - NSA_HEADS // 2 - 1
            for j, piece in enumerate(heads(r, block_onehot if kind == 0 else 0.0)):
                kx_ref[0, 2 * kind + j] = piece.astype(BF16)
    for i in range(N_ROPE // LANES, N_NSA // LANES):
        t = _dot(u, wn_ref[:, i * LANES:(i + 1) * LANES]) + bn_ref[:, i * LANES:(i + 1) * LANES]
        for j in range(2):
            kv_ref[0, 2 * (i - N_ROPE // LANES) + 2 + j] = t[:, j * NSA_DH:(j + 1) * NSA_DH].astype(BF16)
    ga_ref[0] = _sigmoid(_dot(u, wg_ref[...]) + bg_ref[...])
    W = HGRN_WIDTH
    hq = _dot(u, wh_ref[:, 0:W]) + bh_ref[:, 0:W]
    hq_ref[0] = (_silu(hq) * (HGRN_DK ** -0.5)).astype(BF16)
    z = _dot(u, wh_ref[:, W:2 * W]) + bh_ref[:, W:2 * W]
    lb = lb_ref[...]
    log_sig = jnp.minimum(z, 0.0) - jnp.log1p(jnp.exp(-jnp.abs(z)))
    a = jnp.log(lb)
    bb = jnp.log1p(-lb) + log_sig
    hlf_ref[0] = jnp.maximum(a, bb) + jnp.log1p(jnp.exp(-jnp.abs(a - bb)))
    hv_ref[0] = (_dot(u, wh_ref[:, 2 * W:3 * W]) + bh_ref[:, 2 * W:3 * W]).astype(BF16)
    hg_ref[0] = _silu(_dot(u, wh_ref[:, 3 * W:4 * W]) + bh_ref[:, 3 * W:4 * W]).astype(BF16)
    for i, ref in enumerate((gma_ref, gmb_ref)):
        for j in range(2):
            c0 = i * D_MODEL + j * (D_MODEL // 2)
            c1 = c0 + D_MODEL // 2
            ref[0, :, j * (D_MODEL // 2):(j + 1) * (D_MODEL // 2)] = _sigmoid(
                _dot(u, wm_ref[:, c0:c1]) + bm_ref[:, c0:c1]).astype(BF16)


def in_proj(x, sc, sh, cos_t, sin_t, lb, wts):
    B, S, D = x.shape
    tm = min(TOKEN_TILE, S)
    assert S // SEL_LEN <= LANES - NSA_DH
    wn, bn, wg, bg, wh, bh, wm, bm = wts
    tok = lambda n: pl.BlockSpec((1, tm, n), lambda b, i: (b, i, 0))
    per_b = pl.BlockSpec((1, 1, D), lambda b, i: (b, 0, 0))
    tab = pl.BlockSpec((tm, LANES), lambda b, i: (i, 0))
    out_shape = (
        jax.ShapeDtypeStruct((B, NSA_HEADS, S, LANES), BF16),
        jax.ShapeDtypeStruct((B, 2 * NSA_GROUPS, S, LANES), BF16),
        jax.ShapeDtypeStruct((B, 4 * NSA_GROUPS, S, NSA_DH), BF16),
        jax.ShapeDtypeStruct((B, S, N_GATE), F32),
        jax.ShapeDtypeStruct((B, S, HGRN_WIDTH), BF16),
        jax.ShapeDtypeStruct((B, S, HGRN_WIDTH), F32),
        jax.ShapeDtypeStruct((B, S, HGRN_VWIDTH), BF16),
        jax.ShapeDtypeStruct((B, S, HGRN_VWIDTH), BF16),
        jax.ShapeDtypeStruct((B, S, D), BF16),
        jax.ShapeDtypeStruct((B, S, D), BF16),
    )
    out_specs = (
        pl.BlockSpec((1, NSA_HEADS, tm, LANES), lambda b, i: (b, 0, i, 0)),
        pl.BlockSpec((1, 2 * NSA_GROUPS, tm, LANES), lambda b, i: (b, 0, i, 0)),
        pl.BlockSpec((1, 4 * NSA_GROUPS, tm, NSA_DH), lambda b, i: (b, 0, i, 0)),
        tok(N_GATE), tok(HGRN_WIDTH), tok(HGRN_WIDTH), tok(HGRN_VWIDTH), tok(HGRN_VWIDTH), tok(D), tok(D),
    )
    return pl.pallas_call(
        _in_proj_kernel,
        grid=(B, S // tm),
        in_specs=[tok(D), per_b, per_b, tab, tab, _resident(lb.shape),
                  _resident(wn.shape), _resident(bn.shape), _resident(wg.shape), _resident(bg.shape),
                  _resident(wh.shape), _resident(bh.shape), _resident(wm.shape), _resident(bm.shape)],
        out_specs=out_specs,
        out_shape=out_shape,
        compiler_params=_cparams(2),
        name="in_proj",
    )(x, sc, sh, cos_t, sin_t, lb, wn, bn, wg, bg, wh, bh, wm, bm)


def _prep_in_proj_weights(w_in_l, b_in_l):
    o = IN_OFFSETS
    col = lambda i: (w_in_l[:, o[i]:o[i + 1]], b_in_l[o[i]:o[i + 1]])
    q_a, k_c, v_c, k_s, v_s, k_w, v_w, g_a, q_b, f_b, i_b, g_b, gm_a, gm_b = [col(i) for i in range(14)]

    def cat(parts):
        return (jnp.concatenate([p[0] for p in parts], axis=1).astype(BF16),
                jnp.concatenate([p[1] for p in parts], axis=0).reshape(1, -1).astype(F32))

    wn, bn = cat([q_a, k_c, k_s, k_w, v_c, v_s, v_w])
    per_group = 3 * NSA_REP
    gw = jnp.zeros((w_in_l.shape[0], N_GATE), w_in_l.dtype)
    gb = jnp.zeros((N_GATE,), b_in_l.dtype)
    for g in range(NSA_GROUPS):
        gw = gw.at[:, g * LANES:g * LANES + per_group].set(g_a[0][:, g * per_group:(g + 1) * per_group])
        gb = gb.at[g * LANES:g * LANES + per_group].set(g_a[1][g * per_group:(g + 1) * per_group])
    wg, bg = gw.astype(BF16), gb.reshape(1, -1).astype(F32)
    wh, bh = cat([q_b, f_b, i_b, g_b])
    wm, bm = cat([gm_a, gm_b])
    return wn, bn, wg, bg, wh, bh, wm, bm


def _compress_kernel(t_ref, pe_ref, w1_ref, w2_ref, o_ref):
    half = CMP_STRIDE * NSA_DH
    t = t_ref[0, 0]
    nrow = t.shape[0]
    a = _dot(t, w1_ref[0, 0:half, :])
    b = _dot(t, w1_ref[0, half:2 * half, :])
    pe = jnp.broadcast_to(pe_ref[0], (8, 2 * half)).astype(BF16)
    c = _dot(pe, w1_ref[0])[0:1]
    h = a + pltpu.roll(b, nrow - 1, 0) + c
    o_ref[0, 0] = _dot(_silu(h).astype(BF16), w2_ref[0]).astype(BF16)


def nsa_compress(kv, pe, w1, w2):
    B, _, S, dh = kv.shape
    nrow = S // CMP_STRIDE
    G = NSA_GROUPS
    kv_rows = kv.reshape(B, 2 * G, nrow, CMP_STRIDE * dh)
    return pl.pallas_call(
        _compress_kernel,
        grid=(B, 2, G),
        in_specs=[
            pl.BlockSpec((1, 1, nrow, CMP_STRIDE * dh), lambda b, s, g: (b, s * G + g, 0, 0)),
            pl.BlockSpec((1, 1, CMP_LEN * dh), lambda b, s, g: (s, 0, 0)),
            pl.BlockSpec((1, CMP_LEN * dh, CMP_HIDDEN), lambda b, s, g: (s, 0, 0)),
            pl.BlockSpec((1, CMP_HIDDEN, LANES), lambda b, s, g: (s, 0, 0)),
        ],
        out_specs=pl.BlockSpec((1, 1, nrow, LANES), lambda b, s, g: (b, s * G + g, 0, 0)),
        out_shape=jax.ShapeDtypeStruct((B, 2 * G, nrow, LANES), BF16),
        compiler_params=_cparams(3),
        name="nsa_compress",
    )(kv_rows, pe, w1, w2)


def _nsa_kernel(q_ref, kc_ref, vc_ref, ks_ref, kw_ref, vst_ref, vwt_ref, gt_ref, ovt_ref, pm_ref,
                o_ref, m_ref, l_ref, acc_ref, os_ref, ow_ref, *, n_sel):
    R, TQ, dh = NSA_REP, Q_TILE, NSA_DH
    cols = R * TQ
    qb = pl.program_id(2)
    s0 = qb * TQ
    q = q_ref[0].reshape(cols, LANES)
    t_lane = s0 + (lax.broadcasted_iota(jnp.int32, (1, cols), 1) % TQ)

    ncb = kc_ref.shape[2]
    sc = _dot_nt(kc_ref[0, 0], q)
    n_sub = lax.broadcasted_iota(jnp.int32, (ncb, cols), 0)
    mask_c = n_sub * CMP_STRIDE + (CMP_LEN - 1) <= t_lane
    sc = jnp.where(mask_c, sc, NEG_BIG)
    mc = jnp.max(sc, axis=0, keepdims=True)
    ec = jnp.where(mask_c, jnp.exp2(sc - mc), 0.0)
    pc = ec * (1.0 / jnp.maximum(jnp.sum(ec, axis=0, keepdims=True), 1e-30))
    o_c = _dot_tn(vc_ref[0, 0], pc.astype(BF16))[0:dh]

    psum = pc[:, 0:TQ]
    for r in range(1, R):
        psum = psum + pc[:, r * TQ:(r + 1) * TQ]
    p_hi = psum.astype(BF16)
    p_lo = (psum - p_hi.astype(F32)).astype(BF16)
    imp = _dot(ovt_ref[...], p_hi) + _dot(ovt_ref[...], p_lo)
    nb = imp.shape[0]
    jb = lax.broadcasted_iota(jnp.int32, (nb, TQ), 0)
    tb = (s0 + lax.broadcasted_iota(jnp.int32, (nb, TQ), 1)) // SEL_LEN
    valid = jb <= tb
    forced = jnp.where(valid, jnp.where(jb == 0, 1.0, jnp.where(jb >= tb - 1, 1.0, 0.0)), 0.0)
    score = jnp.where(forced > 0.5, FORCE_SCORE, jnp.where(valid, imp, -1.0))
    rank = jnp.zeros((nb, TQ), F32)
    for i in range(nb):
        si = score[i:i + 1, :]
        tie_first = jnp.where(jb > i, 1.0, 0.0)
        rank = rank + jnp.where(si > score, 1.0, jnp.where(si == score, tie_first, 0.0))
    not_sel = jnp.where(rank < n_sel, 0.0, 1.0).astype(BF16)
    bias = _dot_tn(not_sel, pm_ref[...]).astype(BF16)
    q_sel = q + jnp.concatenate([bias] * R, axis=0)

    kidx = lax.broadcasted_iota(jnp.int32, (K_CHUNK, cols), 0)
    tq = lax.broadcasted_iota(jnp.int32, (K_CHUNK, cols), 1) % TQ

    causal = kidx <= tq

    def block_softmax(k_ref, vt_ref, qmat, c0, n, band_first=False, causal_last=True):
        k0 = pl.multiple_of(c0 * K_CHUNK, K_CHUNK)
        s = _dot_nt(k_ref[0, 0, pl.ds(k0, n * K_CHUNK), :], qmat)
        parts = [s[i * K_CHUNK:(i + 1) * K_CHUNK] for i in range(n)]
        if band_first:
            parts[0] = jnp.where(tq < kidx, parts[0], NEG_BIG)
        if causal_last:
            parts[-1] = jnp.where(causal, parts[-1], NEG_BIG)
        s = jnp.concatenate(parts, axis=0) if n > 1 else parts[0]
        m = jnp.max(s, axis=0, keepdims=True)
        p = jnp.exp2(s - m)
        l = jnp.sum(p, axis=0, keepdims=True)
        vt = jnp.concatenate([vt_ref[0, 0, c0 + i] for i in range(n)], axis=1) if n > 1 else vt_ref[0, 0, c0]
        return m, l, _dot(vt, p.astype(BF16))

    def merge(a, b):
        m = jnp.maximum(a[0], b[0])
        ca, cb = jnp.exp2(a[0] - m), jnp.exp2(b[0] - m)
        return m, ca * a[1] + cb * b[1], ca * a[2] + cb * b[2]

    n_win = WINDOW // K_CHUNK

    @pl.when(qb >= n_win)
    def _():
        _, l, acc = block_softmax(kw_ref, vwt_ref, q, qb - n_win, n_win + 1, band_first=True)
        ow_ref[...] = acc * (1.0 / l)

    for n in range(1, n_win + 1):
        @pl.when(qb == n - 1)
        def _():
            _, l, acc = block_softmax(kw_ref, vwt_ref, q, 0, n)
            ow_ref[...] = acc * (1.0 / l)

    m_ref[...] = jnp.full(m_ref.shape, NEG_BIG, F32)
    l_ref[...] = jnp.zeros(l_ref.shape, F32)
    acc_ref[...] = jnp.zeros(acc_ref.shape, F32)

    def sel_body(c, carry):
        m, l, acc = merge((m_ref[...], l_ref[...], acc_ref[...]),
                          block_softmax(ks_ref, vst_ref, q_sel, c * SEL_STEP, SEL_STEP, causal_last=False))
        m_ref[...], l_ref[...], acc_ref[...] = m, l, acc
        return carry

    n_full = qb // SEL_STEP
    lax.fori_loop(0, n_full, sel_body, 0)
    for n in range(1, SEL_STEP + 1):
        @pl.when(qb - n_full * SEL_STEP == n - 1)
        def _():
            _, l, acc = merge((m_ref[...], l_ref[...], acc_ref[...]),
                              block_softmax(ks_ref, vst_ref, q_sel, n_full * SEL_STEP, n))
            os_ref[...] = acc * (1.0 / l)

    o_s = os_ref[...]
    o_w = ow_ref[...]

    gate = gt_ref[0, 0]
    pieces = []
    for r in range(R):
        sl = slice(r * TQ, (r + 1) * TQ)
        o_r = (gate[3 * r:3 * r + 1, :] * o_c[:, sl] + gate[3 * r + 1:3 * r + 2, :] * o_s[:, sl]
               + gate[3 * r + 2:3 * r + 3, :] * o_w[:, sl])
        pieces.append(o_r.T)
    o_ref[0] = jnp.concatenate(pieces, axis=1).astype(BF16)


def nsa_attend(q, kx, vt, cmp, gates_t):
    B, H, S, _ = q.shape
    G, R, dh = NSA_GROUPS, NSA_REP, NSA_DH
    ncb = S // CMP_STRIDE
    nb = S // SEL_LEN
    assert (S % Q_TILE == 0 and Q_TILE == K_CHUNK and WINDOW % K_CHUNK == 0 and K_CHUNK % SEL_LEN == 0
            and 3 * R <= 2 * SUBLANES)
    cstart = np.arange(ncb) * CMP_STRIDE
    sstart = np.arange(nb) * SEL_LEN
    overlap = ((cstart[:, None] < sstart[None, :] + SEL_LEN) & (cstart[:, None] + CMP_LEN > sstart[None, :]))
    ovt = jnp.asarray(overlap.T, BF16)
    place = np.zeros((nb, LANES), np.float32)
    place[np.arange(nb), dh + np.arange(nb)] = NEG_BIG
    pm = jnp.asarray(place, BF16)
    nck = S // K_CHUNK
    kx_spec = lambda idx: pl.BlockSpec((1, 1, S, LANES), lambda b, g, i, idx=idx: (b, idx * G + g, 0, 0))
    vt_spec = lambda idx: pl.BlockSpec((1, 1, nck, dh, K_CHUNK), lambda b, g, i, idx=idx: (b, idx * G + g, 0, 0, 0))
    cmp_spec = lambda idx: pl.BlockSpec((1, 1, ncb, LANES), lambda b, g, i, idx=idx: (b, idx * G + g, 0, 0))
    cols = R * Q_TILE
    return pl.pallas_call(
        functools.partial(_nsa_kernel, n_sel=min(SEL_TOPK, nb)),
        grid=(B, G, S // Q_TILE),
        in_specs=[
            pl.BlockSpec((1, R, Q_TILE, LANES), lambda b, g, i: (b, g, i, 0)),
            cmp_spec(0), cmp_spec(1),
            kx_spec(0), kx_spec(1),
            vt_spec(0), vt_spec(1),
            pl.BlockSpec((1, 1, 2 * SUBLANES, Q_TILE), lambda b, g, i: (b, g, 0, i)),
            pl.BlockSpec(ovt.shape, lambda b, g, i: (0, 0)),
            pl.BlockSpec(pm.shape, lambda b, g, i: (0, 0)),
        ],
        out_specs=pl.BlockSpec((1, Q_TILE, R * dh), lambda b, g, i: (b, i, g)),
        out_shape=jax.ShapeDtypeStruct((B, S, H * dh), BF16),
        scratch_shapes=[
            pltpu.VMEM((1, cols), F32),
            pltpu.VMEM((1, cols), F32),
            pltpu.VMEM((dh, cols), F32),
            pltpu.VMEM((dh, cols), F32),
            pltpu.VMEM((dh, cols), F32),
        ],
        compiler_params=_cparams(3),
        name="nsa_attend",
    )(q, cmp, cmp, kx, kx, vt, vt, gates_t, ovt, pm)


def _hgrn_kernel(q_ref, lf_ref, v_ref, g_ref, ng_ref, tri_ref, o_ref, st_ref):
    C, SB = HGRN_CHUNK, HGRN_SUB
    NBK = C // SB
    S = q_ref.shape[1]
    st_ref[...] = jnp.zeros(st_ref.shape, F32)
    row = lax.broadcasted_iota(jnp.int32, (C, HGRN_DK), 0)
    sub_row = lax.broadcasted_iota(jnp.int32, (SB, 1), 0)

    def chunk(ci, carry):
        r0 = pl.multiple_of(ci * C, C)
        q = q_ref[0, pl.ds(r0, C), :].astype(F32)
        lf = lf_ref[0, pl.ds(r0, C), :]
        v_bf = v_ref[0, pl.ds(r0, C), :]
        v = v_bf.astype(F32)
        kh = 1.0 - jnp.exp(lf)
        tri = tri_ref[...]
        lf0 = lf.astype(BF16)
        lf1 = (lf - lf0.astype(F32)).astype(BF16)
        lf2 = (lf - lf0.astype(F32) - lf1.astype(F32)).astype(BF16)
        b = _dot(tri, lf0) + _dot(tri, lf1) + _dot(tri, lf2)
        b_last = b[C - 1:C, :]
        st = st_ref[...]
        o = _dot_nt((q * jnp.exp(b)).astype(BF16), st.astype(BF16))
        b_end = jnp.concatenate(
            [jnp.broadcast_to(b[(j + 1) * SB - 1:(j + 1) * SB, :], (SB, HGRN_DK)) for j in range(NBK)], axis=0)
        k_end = kh * jnp.exp(b_end - b)
        q_parts, k_parts = [], []
        for j in range(NBK - 1):
            lo = (j + 1) * SB
            qj = q[lo:] * jnp.exp(b[lo:] - b[lo - 1:lo, :])
            q_parts.append(jnp.concatenate([jnp.zeros((lo, HGRN_DK), F32), qj], axis=0))
            k_parts.append(jnp.where((row >= j * SB) & (row < lo), k_end, 0.0))
        q_cat = jnp.concatenate(q_parts, axis=1).astype(BF16)
        k_cat = jnp.concatenate(k_parts, axis=1).astype(BF16)
        a_off = _dot_nt(q_cat, k_cat)
        o = o + _dot(a_off.astype(BF16), v_bf)
        diag = []
        for j in range(NBK):
            sl = slice(j * SB, (j + 1) * SB)
            qj, bj, kj, vj = q[sl], b[sl], kh[sl], v[sl]
            oj = jnp.zeros((SB, HGRN_DV), F32)
            for s in range(SB):
                w = jnp.exp(jnp.minimum(bj - bj[s:s + 1, :], 0.0))
                a = jnp.sum(qj * kj[s:s + 1, :] * w, axis=-1, keepdims=True)
                a = jnp.where(sub_row >= s, a, 0.0)
                oj = oj + a * vj[s:s + 1, :]
            diag.append(oj)
        o = o + jnp.concatenate(diag, axis=0)
        k_last = (kh * jnp.exp(b_last - b)).astype(BF16)
        st_ref[...] = st * jnp.exp(b_last) + _dot_tn(v_bf, k_last)
        o = o * lax.rsqrt(jnp.mean(o * o, axis=-1, keepdims=True) + RMS_EPS) * ng_ref[...]
        o_ref[0, pl.ds(r0, C), :] = (o * g_ref[0, pl.ds(r0, C), :].astype(F32)).astype(BF16)
        return carry

    lax.fori_loop(0, S // C, chunk, 0)


def hgrn2(hq, hlf, hv, hg, norm_g):
    B, S, _ = hq.shape
    C = HGRN_CHUNK
    assert S % C == 0
    tri = jnp.asarray(np.tril(np.ones((C, C))), BF16)
    spec = lambda n: pl.BlockSpec((1, S, n), lambda b, h: (b, 0, h))
    return pl.pallas_call(
        _hgrn_kernel,
        grid=(B, HGRN_HEADS),
        in_specs=[spec(HGRN_DK), spec(HGRN_DK), spec(HGRN_DV), spec(HGRN_DV),
                  pl.BlockSpec((1, HGRN_DV), lambda b, h: (0, 0)),
                  pl.BlockSpec((C, C), lambda b, h: (0, 0))],
        out_specs=spec(HGRN_DV),
        out_shape=jax.ShapeDtypeStruct((B, S, HGRN_VWIDTH), BF16),
        scratch_shapes=[pltpu.VMEM((HGRN_DV, HGRN_DK), F32)],
        compiler_params=_cparams(2),
        name="hgrn2",
    )(hq, hlf, hv, hg, norm_g.reshape(1, HGRN_DV).astype(F32), tri)


def _merge_kernel(ya_ref, yb_ref, gma_ref, gmb_ref, x_ref, gt_ref, lg_ref, lbias_ref,
                  wa_ref, wb_ref, wo_ref, o_ref):
    pa = _dot(ya_ref[0], wa_ref[...])
    pb = _dot(yb_ref[0], wb_ref[...])
    merged = gma_ref[0].astype(F32) * pa + gmb_ref[0].astype(F32) * pb
    y = _dot(merged.astype(BF16), wo_ref[...])
    z = DEEPNORM_ALPHA * x_ref[0] + (1.0 + gt_ref[0]) * y
    o_ref[0] = _layer_norm(z, lg_ref[...], lbias_ref[...])


def merge_out(ya, yb, gma, gmb, x, gt, ln_g, ln_b, wa, wb, wo):
    B, S, D = x.shape
    tm = min(TOKEN_TILE, S)
    tok = lambda n: pl.BlockSpec((1, tm, n), lambda b, i: (b, i, 0))
    return pl.pallas_call(
        _merge_kernel,
        grid=(B, S // tm),
        in_specs=[tok(NSA_WIDTH), tok(HGRN_VWIDTH), tok(D), tok(D), tok(D),
                  pl.BlockSpec((1, 1, D), lambda b, i: (b, 0, 0)),
                  _resident((1, D)), _resident((1, D)),
                  _resident(wa.shape), _resident(wb.shape), _resident(wo.shape)],
        out_specs=tok(D),
        out_shape=jax.ShapeDtypeStruct((B, S, D), F32),
        compiler_params=_cparams(2),
        name="merge_out",
    )(ya, yb, gma, gmb, x, gt, ln_g.reshape(1, D), ln_b.reshape(1, D), wa, wb, wo)


MLP_COLS = 1024


def _mlp_kernel(x_ref, sc_ref, sh_ref, gt_ref, lg_ref, lbias_ref, w1_ref, w2_ref, o_ref):
    x = x_ref[0]
    u = (x * (1.0 + sc_ref[0]) + sh_ref[0]).astype(BF16)
    y = jnp.zeros(x.shape, F32)
    for c in range(MLP_HIDDEN // MLP_COLS):
        h = jnp.maximum(_dot(u, w1_ref[:, c * MLP_COLS:(c + 1) * MLP_COLS]), 0.0)
        y = y + _dot((h * h).astype(BF16), w2_ref[c * MLP_COLS:(c + 1) * MLP_COLS, :])
    z = DEEPNORM_ALPHA * x + (1.0 + gt_ref[0]) * y
    o_ref[0] = _layer_norm(z, lg_ref[...], lbias_ref[...])


def mlp(x, sc, sh, gt, ln_g, ln_b, w1, w2):
    B, S, D = x.shape
    tm = min(TOKEN_TILE, S)
    tok = pl.BlockSpec((1, tm, D), lambda b, i: (b, i, 0))
    per_b = pl.BlockSpec((1, 1, D), lambda b, i: (b, 0, 0))
    return pl.pallas_call(
        _mlp_kernel,
        grid=(B, S // tm),
        in_specs=[tok, per_b, per_b, per_b, _resident((1, D)), _resident((1, D)),
                  _resident(w1.shape), _resident(w2.shape)],
        out_specs=tok,
        out_shape=jax.ShapeDtypeStruct((B, S, D), F32),
        compiler_params=_cparams(2),
        name="mlp",
    )(x, sc, sh, gt, ln_g.reshape(1, D), ln_b.reshape(1, D), w1, w2)


def _rope_tables(S):
    inv = 1.0 / (ROPE_THETA ** (jnp.arange(0, NSA_DH, 2, dtype=F32) / NSA_DH))
    ang = jnp.arange(S, dtype=F32)[:, None] * inv[None, :]
    cos, sin = jnp.cos(ang), jnp.sin(ang)
    reps = LANES // NSA_DH
    return (jnp.tile(jnp.concatenate([cos, cos], axis=1), (1, reps)),
            jnp.tile(jnp.concatenate([-sin, sin], axis=1), (1, reps)))


def nsa_layout(kv, ga):
    B, _, S, dh = kv.shape
    G = NSA_GROUPS
    vt = kv[:, 2 * G:].reshape(B, 2 * G, S // K_CHUNK, K_CHUNK, dh).swapaxes(-1, -2)
    gates_t = ga.reshape(B, S, G, LANES)[..., :2 * SUBLANES].transpose(0, 2, 3, 1)
    return vt, gates_t


def kernel(x, c, w_in, b_in, cmp_pe_k, cmp_pe_v, cmp_wk1, cmp_wk2, cmp_wv1, cmp_wv2, hgrn_lb_logits, hgrn_norm_g, w_branch_a, w_branch_b, w_out, w_ada, b_ada, ln1_g, ln1_b, w_mlp1, w_mlp2, ln2_g, ln2_b):
    B, S, D = x.shape
    G = NSA_GROUPS
    lb_all = jnp.cumsum(jax.nn.softmax(hgrn_lb_logits.astype(F32), axis=0), axis=0)
    lb_all = lb_all - lb_all[0:1]
    cos_t, sin_t = _rope_tables(S)
    mod = adaln_mod(c, w_ada, b_ada)
    for l in range(DEPTH):
        sh1, sc1, gt1, sh2, sc2, gt2 = [mod[l, :, None, i * D:(i + 1) * D] for i in range(6)]
        wts = _prep_in_proj_weights(w_in[l], b_in[l])
        q, kx, kv, ga, hq, hlf, hv, hg, gma, gmb = in_proj(x, sc1, sh1, cos_t, sin_t, lb_all[l].reshape(1, -1), wts)
        pe = jnp.stack([cmp_pe_k[l].reshape(1, -1), cmp_pe_v[l].reshape(1, -1)])
        w1 = jnp.stack([cmp_wk1[l], cmp_wv1[l]]).astype(BF16)
        w2 = jnp.pad(jnp.stack([cmp_wk2[l], cmp_wv2[l]]), ((0, 0), (0, 0), (0, LANES - NSA_DH))).astype(BF16)
        cmp = nsa_compress(kv[:, :2 * G], pe, w1, w2)
        vt, gates_t = nsa_layout(kv, ga)
        ya = nsa_attend(q, kx, vt, cmp, gates_t)
        yb = hgrn2(hq, hlf, hv, hg, hgrn_norm_g[l])
        x = merge_out(ya, yb, gma, gmb, x, gt1, ln1_g[l], ln1_b[l],
                      w_branch_a[l].astype(BF16), w_branch_b[l].astype(BF16), w_out[l].astype(BF16))
        x = mlp(x, sc2, sh2, gt2, ln2_g[l], ln2_b[l], w_mlp1[l].astype(BF16), w_mlp2[l].astype(BF16))
    return x
```

```python
import functools

import numpy as np
import jax
import jax.numpy as jnp
from jax import lax
from jax.experimental import pallas as pl
from jax.experimental.pallas import tpu as pltpu

D_MODEL = 1024
DEPTH = 2
NSA_HEADS = 8
NSA_GROUPS = 2
NSA_REP = NSA_HEADS // NSA_GROUPS
NSA_DH = 64
NSA_WIDTH = NSA_HEADS * NSA_DH
NSA_KV_WIDTH = NSA_GROUPS * NSA_DH
CMP_LEN = 32
CMP_STRIDE = 16
CMP_HIDDEN = 2 * NSA_DH
SEL_LEN = 64
SEL_TOPK = 8
FORCE_SCORE = 1.0e4
WINDOW = 512
HGRN_HEADS = 4
HGRN_DK = 128
HGRN_DV = 128
HGRN_WIDTH = HGRN_HEADS * HGRN_DK
HGRN_VWIDTH = HGRN_HEADS * HGRN_DV
MLP_HIDDEN = 4 * D_MODEL
ROPE_THETA = 10000.0
LN_EPS = 1e-5
RMS_EPS = 1e-6
DEEPNORM_ALPHA = (2 * DEPTH) ** 0.25
IN_SIZES = (NSA_WIDTH,) + (NSA_KV_WIDTH,) * 6 + (3 * NSA_HEADS,) + (HGRN_WIDTH, HGRN_WIDTH, HGRN_VWIDTH, HGRN_VWIDTH) + (D_MODEL, D_MODEL)
IN_OFFSETS = [0] + [int(v) for v in np.cumsum(IN_SIZES)]

LANES = 128
SUBLANES = 8
VMEM_LIMIT = 48 * 1024 * 1024
TOKEN_TILE = 512
Q_TILE = 128
K_CHUNK = 128
HGRN_CHUNK = 64
HGRN_SUB = 8
NEG_BIG = -1e30
LOG2E = 1.4426950408889634
Q_SCALE = NSA_DH ** -0.5 * LOG2E

F32 = jnp.float32
BF16 = jnp.bfloat16


def _cparams(n_grid):
    return pltpu.CompilerParams(dimension_semantics=("arbitrary",) * n_grid, vmem_limit_bytes=VMEM_LIMIT)


def _resident(shape):
    nd = len(shape)
    return pl.BlockSpec(shape, lambda *_: (0,) * nd, pipeline_mode=pl.Buffered(1))


def _dot(a, b):
    return jnp.dot(a, b, preferred_element_type=F32)


def _dot_nt(a, b):
    return lax.dot_general(a, b, (((1,), (1,)), ((), ())), preferred_element_type=F32)


def _dot_tn(a, b):
    return lax.dot_general(a, b, (((0,), (0,)), ((), ())), preferred_element_type=F32)


def _sigmoid(x):
    return 1.0 / (1.0 + jnp.exp(-x))


def _silu(x):
    return x * _sigmoid(x)


def _layer_norm(z, g, b):
    mu = jnp.mean(z, axis=-1, keepdims=True)
    zc = z - mu
    var = jnp.mean(zc * zc, axis=-1, keepdims=True)
    return zc * lax.rsqrt(var + LN_EPS) * g + b


def _adaln_kernel(c_ref, w_ref, b_ref, o_ref):
    cond = _silu(c_ref[...]).astype(BF16)
    o_ref[0] = _dot(cond, w_ref[0]) + b_ref[0]


def adaln_mod(c, w_ada, b_ada):
    L, D, N = w_ada.shape
    B = c.shape[0]
    tn = D
    return pl.pallas_call(
        _adaln_kernel,
        grid=(L, N // tn),
        in_specs=[
            pl.BlockSpec((B, D), lambda l, j: (0, 0)),
            pl.BlockSpec((1, D, tn), lambda l, j: (l, 0, j)),
            pl.BlockSpec((1, 1, tn), lambda l, j: (l, 0, j)),
        ],
        out_specs=pl.BlockSpec((1, B, tn), lambda l, j: (l, 0, j)),
        out_shape=jax.ShapeDtypeStruct((L, B, N), F32),
        compiler_params=_cparams(2),
        name="adaln_mod",
    )(c, w_ada.astype(BF16), b_ada.reshape(L, 1, N))


N_ROPE = NSA_WIDTH + 3 * NSA_KV_WIDTH
N_NSA = N_ROPE + 3 * NSA_KV_WIDTH
N_GATE = NSA_GROUPS * LANES
N_HGRN = 2 * HGRN_WIDTH + 2 * HGRN_VWIDTH
N_MERGE = 2 * D_MODEL


def _in_proj_kernel(x_ref, sc_ref, sh_ref, cos_ref, sin_ref, lb_ref,
                    wn_ref, bn_ref, wg_ref, bg_ref, wh_ref, bh_ref, wm_ref, bm_ref,
                    q_ref, kx_ref, kv_ref, ga_ref, hq_ref, hlf_ref, hv_ref, hg_ref, gma_ref, gmb_ref):
    u = (x_ref[0] * (1.0 + sc_ref[0]) + sh_ref[0]).astype(BF16)
    tm = u.shape[0]
    cos = cos_ref[...]
    sin = sin_ref[...]
    lane = lax.broadcasted_iota(jnp.int32, cos.shape, 1)
    first_half = (lane % NSA_DH) < (NSA_DH // 2)
    low = lane < NSA_DH
    pos = pl.program_id(1) * tm + lax.broadcasted_iota(jnp.int32, cos.shape, 0)
    block_onehot = jnp.where(lane - NSA_DH == pos // SEL_LEN, 1.0, 0.0)

    def heads(t, upper):
        return jnp.where(low, t, upper), jnp.where(low, pltpu.roll(t, NSA_DH, 1), upper)

    for i in range(N_ROPE // LANES):
        t = _dot(u, wn_ref[:, i * LANES:(i + 1) * LANES]) + bn_ref[:, i * LANES:(i + 1) * LANES]
        rot = jnp.where(first_half, pltpu.roll(t, LANES - NSA_DH // 2, 1), pltpu.roll(t, NSA_DH // 2, 1))
        r = t * cos + rot * sin
        if i < NSA_HEADS // 2:
            for j, piece in enumerate(heads(r * Q_SCALE, 0.0)):
                q_ref[0, 2 * i + j] = piece.astype(BF16)
        elif i == NSA_HEADS // 2:
            for j in range(2):
                kv_ref[0, j] = r[:, j * NSA_DH:(j + 1) * NSA_DH].astype(BF16)
        else:
            kind = i - NSA_HEADS // 2 - 1
            for j, piece in enumerate(heads(r, block_onehot if kind == 0 else 0.0)):
                kx_ref[0, 2 * kind + j] = piece.astype(BF16)
    for i in range(N_ROPE // LANES, N_NSA // LANES):
        t = _dot(u, wn_ref[:, i * LANES:(i + 1) * LANES]) + bn_ref[:, i * LANES:(i + 1) * LANES]
        for j in range(2):
            kv_ref[0, 2 * (i - N_ROPE // LANES) + 2 + j] = t[:, j * NSA_DH:(j + 1) * NSA_DH].astype(BF16)
    ga_ref[0] = _sigmoid(_dot(u, wg_ref[...]) + bg_ref[...])
    W = HGRN_WIDTH
    hq = _dot(u, wh_ref[:, 0:W]) + bh_ref[:, 0:W]
    hq_ref[0] = (_silu(hq) * (HGRN_DK ** -0.5)).astype(BF16)
    z = _dot(u, wh_ref[:, W:2 * W]) + bh_ref[:, W:2 * W]
    lb = lb_ref[...]
    log_sig = jnp.minimum(z, 0.0) - jnp.log1p(jnp.exp(-jnp.abs(z)))
    a = jnp.log(lb)
    bb = jnp.log1p(-lb) + log_sig
    hlf_ref[0] = jnp.maximum(a, bb) + jnp.log1p(jnp.exp(-jnp.abs(a - bb)))
    hv_ref[0] = (_dot(u, wh_ref[:, 2 * W:3 * W]) + bh_ref[:, 2 * W:3 * W]).astype(BF16)
    hg_ref[0] = _silu(_dot(u, wh_ref[:, 3 * W:4 * W]) + bh_ref[:, 3 * W:4 * W]).astype(BF16)
    for i, ref in enumerate((gma_ref, gmb_ref)):
        for j in range(2):
            c0 = i * D_MODEL + j * (D_MODEL // 2)
            c1 = c0 + D_MODEL // 2
            ref[0, :, j * (D_MODEL // 2):(j + 1) * (D_MODEL // 2)] = _sigmoid(
                _dot(u, wm_ref[:, c0:c1]) + bm_ref[:, c0:c1]).astype(BF16)


def in_proj(x, sc, sh, cos_t, sin_t, lb, wts):
    B, S, D = x.shape
    tm = min(TOKEN_TILE, S)
    assert S // SEL_LEN <= LANES - NSA_DH
    wn, bn, wg, bg, wh, bh, wm, bm = wts
    tok = lambda n: pl.BlockSpec((1, tm, n), lambda b, i: (b, i, 0))
    per_b = pl.BlockSpec((1, 1, D), lambda b, i: (b, 0, 0))
    tab = pl.BlockSpec((tm, LANES), lambda b, i: (i, 0))
    out_shape = (
        jax.ShapeDtypeStruct((B, NSA_HEADS, S, LANES), BF16),
        jax.ShapeDtypeStruct((B, 2 * NSA_GROUPS, S, LANES), BF16),
        jax.ShapeDtypeStruct((B, 4 * NSA_GROUPS, S, NSA_DH), BF16),
        jax.ShapeDtypeStruct((B, S, N_GATE), F32),
        jax.ShapeDtypeStruct((B, S, HGRN_WIDTH), BF16),
        jax.ShapeDtypeStruct((B, S, HGRN_WIDTH), F32),
        jax.ShapeDtypeStruct((B, S, HGRN_VWIDTH), BF16),
        jax.ShapeDtypeStruct((B, S, HGRN_VWIDTH), BF16),
        jax.ShapeDtypeStruct((B, S, D), BF16),
        jax.ShapeDtypeStruct((B, S, D), BF16),
    )
    out_specs = (
        pl.BlockSpec((1, NSA_HEADS, tm, LANES), lambda b, i: (b, 0, i, 0)),
        pl.BlockSpec((1, 2 * NSA_GROUPS, tm, LANES), lambda b, i: (b, 0, i, 0)),
        pl.BlockSpec((1, 4 * NSA_GROUPS, tm, NSA_DH), lambda b, i: (b, 0, i, 0)),
        tok(N_GATE), tok(HGRN_WIDTH), tok(HGRN_WIDTH), tok(HGRN_VWIDTH), tok(HGRN_VWIDTH), tok(D), tok(D),
    )
    return pl.pallas_call(
        _in_proj_kernel,
        grid=(B, S // tm),
        in_specs=[tok(D), per_b, per_b, tab, tab, _resident(lb.shape),
                  _resident(wn.shape), _resident(bn.shape), _resident(wg.shape), _resident(bg.shape),
                  _resident(wh.shape), _resident(bh.shape), _resident(wm.shape), _resident(bm.shape)],
        out_specs=out_specs,
        out_shape=out_shape,
        compiler_params=_cparams(2),
        name="in_proj",
    )(x, sc, sh, cos_t, sin_t, lb, wn, bn, wg, bg, wh, bh, wm, bm)


def _prep_in_proj_weights(w_in_l, b_in_l):
    o = IN_OFFSETS
    col = lambda i: (w_in_l[:, o[i]:o[i + 1]], b_in_l[o[i]:o[i + 1]])
    q_a, k_c, v_c, k_s, v_s, k_w, v_w, g_a, q_b, f_b, i_b, g_b, gm_a, gm_b = [col(i) for i in range(14)]

    def cat(parts):
        return (jnp.concatenate([p[0] for p in parts], axis=1).astype(BF16),
                jnp.concatenate([p[1] for p in parts], axis=0).reshape(1, -1).astype(F32))

    wn, bn = cat([q_a, k_c, k_s, k_w, v_c, v_s, v_w])
    per_group = 3 * NSA_REP
    gw = jnp.zeros((w_in_l.shape[0], N_GATE), w_in_l.dtype)
    gb = jnp.zeros((N_GATE,), b_in_l.dtype)
    for g in range(NSA_GROUPS):
        gw = gw.at[:, g * LANES:g * LANES + per_group].set(g_a[0][:, g * per_group:(g + 1) * per_group])
        gb = gb.at[g * LANES:g * LANES + per_group].set(g_a[1][g * per_group:(g + 1) * per_group])
    wg, bg = gw.astype(BF16), gb.reshape(1, -1).astype(F32)
    wh, bh = cat([q_b, f_b, i_b, g_b])
    wm, bm = cat([gm_a, gm_b])
    return wn, bn, wg, bg, wh, bh, wm, bm


def _compress_kernel(t_ref, pe_ref, w1_ref, w2_ref, o_ref):
    half = CMP_STRIDE * NSA_DH
    t = t_ref[0, 0]
    nrow = t.shape[0]
    a = _dot(t, w1_ref[0, 0:half, :])
    b = _dot(t, w1_ref[0, half:2 * half, :])
    pe = jnp.broadcast_to(pe_ref[0], (8, 2 * half)).astype(BF16)
    c = _dot(pe, w1_ref[0])[0:1]
    h = a + pltpu.roll(b, nrow - 1, 0) + c
    o_ref[0, 0] = _dot(_silu(h).astype(BF16), w2_ref[0]).astype(BF16)


def nsa_compress(kv, pe, w1, w2):
    B, _, S, dh = kv.shape
    nrow = S // CMP_STRIDE
    G = NSA_GROUPS
    kv_rows = kv.reshape(B, 2 * G, nrow, CMP_STRIDE * dh)
    return pl.pallas_call(
        _compress_kernel,
        grid=(B, 2, G),
        in_specs=[
            pl.BlockSpec((1, 1, nrow, CMP_STRIDE * dh), lambda b, s, g: (b, s * G + g, 0, 0)),
            pl.BlockSpec((1, 1, CMP_LEN * dh), lambda b, s, g: (s, 0, 0)),
            pl.BlockSpec((1, CMP_LEN * dh, CMP_HIDDEN), lambda b, s, g: (s, 0, 0)),
            pl.BlockSpec((1, CMP_HIDDEN, LANES), lambda b, s, g: (s, 0, 0)),
        ],
        out_specs=pl.BlockSpec((1, 1, nrow, LANES), lambda b, s, g: (b, s * G + g, 0, 0)),
        out_shape=jax.ShapeDtypeStruct((B, 2 * G, nrow, LANES), BF16),
        compiler_params=_cparams(3),
        name="nsa_compress",
    )(kv_rows, pe, w1, w2)


def _nsa_kernel(q_ref, kc_ref, vc_ref, ks_ref, kw_ref, vst_ref, vwt_ref, gt_ref, ovt_ref, pm_ref,
                o_ref, *, n_sel, n_tiles):
    for qs in range(n_tiles):
        pl.when(pl.program_id(2) == qs)(functools.partial(
            _nsa_tile, qs, q_ref, kc_ref, vc_ref, ks_ref, kw_ref, vst_ref, vwt_ref, gt_ref, ovt_ref, pm_ref,
            o_ref, n_sel))


def _round_up(x, m):
    return -(-x // m) * m


def _nsa_tile(qb, q_ref, kc_ref, vc_ref, ks_ref, kw_ref, vst_ref, vwt_ref, gt_ref, ovt_ref, pm_ref, o_ref, n_sel):
    R, TQ, dh = NSA_REP, Q_TILE, NSA_DH
    cols = R * TQ
    s0 = qb * TQ
    q = q_ref[0].reshape(cols, LANES)
    t_lane = s0 + (lax.broadcasted_iota(jnp.int32, (1, cols), 1) % TQ)

    ncb = min(kc_ref.shape[2], _round_up((s0 + TQ - CMP_LEN) // CMP_STRIDE + 1, 2 * SUBLANES))
    sc = _dot_nt(kc_ref[0, 0, 0:ncb, :], q)
    n_sub = lax.broadcasted_iota(jnp.int32, (ncb, cols), 0)
    mask_c = n_sub * CMP_STRIDE + (CMP_LEN - 1) <= t_lane
    sc = jnp.where(mask_c, sc, NEG_BIG)
    mc = jnp.max(sc, axis=0, keepdims=True)
    ec = jnp.where(mask_c, jnp.exp2(sc - mc), 0.0)
    pc = ec * (1.0 / jnp.maximum(jnp.sum(ec, axis=0, keepdims=True), 1e-30))
    o_c = _dot_tn(vc_ref[0, 0, 0:ncb, :], pc.astype(BF16))[0:dh]

    nb_live = (s0 + TQ) // SEL_LEN
    nb = min(ovt_ref.shape[0], _round_up(nb_live, 2 * SUBLANES))
    psum = pc[:, 0:TQ]
    for r in range(1, R):
        psum = psum + pc[:, r * TQ:(r + 1) * TQ]
    p_hi = psum.astype(BF16)
    p_lo = (psum - p_hi.astype(F32)).astype(BF16)
    ovt = ovt_ref[0:nb, 0:ncb]
    imp = _dot(ovt, p_hi) + _dot(ovt, p_lo)
    jb = lax.broadcasted_iota(jnp.int32, (nb, TQ), 0)
    tb = (s0 + lax.broadcasted_iota(jnp.int32, (nb, TQ), 1)) // SEL_LEN
    valid = jb <= tb
    forced = jnp.where(valid, jnp.where(jb == 0, 1.0, jnp.where(jb >= tb - 1, 1.0, 0.0)), 0.0)
    score = jnp.where(forced > 0.5, FORCE_SCORE, jnp.where(valid, imp, -1.0))
    rank = jnp.zeros((nb, TQ), F32)
    for i in range(nb_live):
        si = score[i:i + 1, :]
        tie_first = jnp.where(jb > i, 1.0, 0.0)
        rank = rank + jnp.where(si > score, 1.0, jnp.where(si == score, tie_first, 0.0))
    not_sel = jnp.where(rank < n_sel, 0.0, 1.0).astype(BF16)
    bias = _dot_tn(not_sel, pm_ref[0:nb, :]).astype(BF16)
    q_sel = q + jnp.concatenate([bias] * R, axis=0)

    kidx = lax.broadcasted_iota(jnp.int32, (K_CHUNK, cols), 0)
    tq = lax.broadcasted_iota(jnp.int32, (K_CHUNK, cols), 1) % TQ

    causal = kidx <= tq

    def block_softmax(k_ref, vt_ref, qmat, c0, n, band_first=False, causal_last=True):
        s = _dot_nt(k_ref[0, 0, c0 * K_CHUNK:(c0 + n) * K_CHUNK, :], qmat)
        parts = [s[i * K_CHUNK:(i + 1) * K_CHUNK] for i in range(n)]
        if band_first:
            parts[0] = jnp.where(tq < kidx, parts[0], NEG_BIG)
        if causal_last:
            parts[-1] = jnp.where(causal, parts[-1], NEG_BIG)
        s = jnp.concatenate(parts, axis=0) if n > 1 else parts[0]
        m = jnp.max(s, axis=0, keepdims=True)
        p = jnp.exp2(s - m)
        l = jnp.sum(p, axis=0, keepdims=True)
        vt = jnp.concatenate([vt_ref[0, 0, c0 + i] for i in range(n)], axis=1) if n > 1 else vt_ref[0, 0, c0]
        return m, l, _dot(vt, p.astype(BF16))

    n_win = WINDOW // K_CHUNK
    _, l_w, acc_w = block_softmax(kw_ref, vwt_ref, q, max(qb - n_win, 0), min(qb, n_win) + 1, band_first=qb >= n_win)
    o_w = acc_w * (1.0 / l_w)

    _, l_s, acc_s = block_softmax(ks_ref, vst_ref, q_sel, 0, qb + 1)
    o_s = acc_s * (1.0 / l_s)

    gate = gt_ref[0, 0]
    pieces = []
    for r in range(R):
        sl = slice(r * TQ, (r + 1) * TQ)
        o_r = (gate[3 * r:3 * r + 1, :] * o_c[:, sl] + gate[3 * r + 1:3 * r + 2, :] * o_s[:, sl]
               + gate[3 * r + 2:3 * r + 3, :] * o_w[:, sl])
        pieces.append(o_r.T)
    o_ref[0] = jnp.concatenate(pieces, axis=1).astype(BF16)


def nsa_attend(q, kx, vt, cmp, gates_t):
    B, H, S, _ = q.shape
    G, R, dh = NSA_GROUPS, NSA_REP, NSA_DH
    ncb = S // CMP_STRIDE
    nb = S // SEL_LEN
    assert (S % Q_TILE == 0 and Q_TILE == K_CHUNK and WINDOW % K_CHUNK == 0 and K_CHUNK % SEL_LEN == 0
            and 3 * R <= 2 * SUBLANES)
    cstart = np.arange(ncb) * CMP_STRIDE
    sstart = np.arange(nb) * SEL_LEN
    overlap = ((cstart[:, None] < sstart[None, :] + SEL_LEN) & (cstart[:, None] + CMP_LEN > sstart[None, :]))
    ovt = jnp.asarray(overlap.T, BF16)
    place = np.zeros((nb, LANES), np.float32)
    place[np.arange(nb), dh + np.arange(nb)] = NEG_BIG
    pm = jnp.asarray(place, BF16)
    nck = S // K_CHUNK
    kx_spec = lambda idx: pl.BlockSpec((1, 1, S, LANES), lambda b, g, i, idx=idx: (b, idx * G + g, 0, 0))
    vt_spec = lambda idx: pl.BlockSpec((1, 1, nck, dh, K_CHUNK), lambda b, g, i, idx=idx: (b, idx * G + g, 0, 0, 0))
    cmp_spec = lambda idx: pl.BlockSpec((1, 1, ncb, LANES), lambda b, g, i, idx=idx: (b, idx * G + g, 0, 0))
    return pl.pallas_call(
        functools.partial(_nsa_kernel, n_sel=min(SEL_TOPK, nb), n_tiles=S // Q_TILE),
        grid=(B, G, S // Q_TILE),
        in_specs=[
            pl.BlockSpec((1, R, Q_TILE, LANES), lambda b, g, i: (b, g, i, 0)),
            cmp_spec(0), cmp_spec(1),
            kx_spec(0), kx_spec(1),
            vt_spec(0), vt_spec(1),
            pl.BlockSpec((1, 1, 2 * SUBLANES, Q_TILE), lambda b, g, i: (b, g, 0, i)),
            pl.BlockSpec(ovt.shape, lambda b, g, i: (0, 0)),
            pl.BlockSpec(pm.shape, lambda b, g, i: (0, 0)),
        ],
        out_specs=pl.BlockSpec((1, Q_TILE, R * dh), lambda b, g, i: (b, i, g)),
        out_shape=jax.ShapeDtypeStruct((B, S, H * dh), BF16),
        compiler_params=_cparams(3),
        name="nsa_attend",
    )(q, cmp, cmp, kx, kx, vt, vt, gates_t, ovt, pm)


def _hgrn_kernel(q_ref, lf_ref, v_ref, g_ref, ng_ref, tri_ref, o_ref, st_ref):
    C, SB = HGRN_CHUNK, HGRN_SUB
    NBK = C // SB
    S = q_ref.shape[1]
    st_ref[...] = jnp.zeros(st_ref.shape, F32)
    row = lax.broadcasted_iota(jnp.int32, (C, HGRN_DK), 0)
    sub_row = lax.broadcasted_iota(jnp.int32, (SB, 1), 0)

    def head_chunk(h, r0):
        hs = slice(h * HGRN_DK, (h + 1) * HGRN_DK)
        q = q_ref[0, pl.ds(r0, C), hs].astype(F32)
        lf = lf_ref[0, pl.ds(r0, C), hs]
        v_bf = v_ref[0, pl.ds(r0, C), hs]
        v = v_bf.astype(F32)
        kh = 1.0 - jnp.exp(lf)
        tri = tri_ref[...]
        lf0 = lf.astype(BF16)
        lf1 = (lf - lf0.astype(F32)).astype(BF16)
        lf2 = (lf - lf0.astype(F32) - lf1.astype(F32)).astype(BF16)
        b = _dot(tri, lf0) + _dot(tri, lf1) + _dot(tri, lf2)
        b_last = b[C - 1:C, :]
        st = st_ref[h]
        o = _dot_nt((q * jnp.exp(b)).astype(BF16), st.astype(BF16))
        b_end = jnp.concatenate(
            [jnp.broadcast_to(b[(j + 1) * SB - 1:(j + 1) * SB, :], (SB, HGRN_DK)) for j in range(NBK)], axis=0)
        k_end = kh * jnp.exp(b_end - b)
        q_parts, k_parts = [], []
        for j in range(NBK - 1):
            lo = (j + 1) * SB
            qj = q[lo:] * jnp.exp(b[lo:] - b[lo - 1:lo, :])
            q_parts.append(jnp.concatenate([jnp.zeros((lo, HGRN_DK), F32), qj], axis=0))
            k_parts.append(jnp.where((row >= j * SB) & (row < lo), k_end, 0.0))
        q_cat = jnp.concatenate(q_parts, axis=1).astype(BF16)
        k_cat = jnp.concatenate(k_parts, axis=1).astype(BF16)
        a_off = _dot_nt(q_cat, k_cat)
        o = o + _dot(a_off.astype(BF16), v_bf)
        diag = []
        for j in range(NBK):
            sl = slice(j * SB, (j + 1) * SB)
            qj, bj, kj, vj = q[sl], b[sl], kh[sl], v[sl]
            oj = jnp.zeros((SB, HGRN_DV), F32)
            for s in range(SB):
                w = jnp.exp(jnp.minimum(bj - bj[s:s + 1, :], 0.0))
                a = jnp.sum(qj * kj[s:s + 1, :] * w, axis=-1, keepdims=True)
                a = jnp.where(sub_row >= s, a, 0.0)
                oj = oj + a * vj[s:s + 1, :]
            diag.append(oj)
        o = o + jnp.concatenate(diag, axis=0)
        k_last = (kh * jnp.exp(b_last - b)).astype(BF16)
        st_ref[h] = st * jnp.exp(b_last) + _dot_tn(v_bf, k_last)
        o = o * lax.rsqrt(jnp.mean(o * o, axis=-1, keepdims=True) + RMS_EPS) * ng_ref[...]
        o_ref[0, pl.ds(r0, C), hs] = (o * g_ref[0, pl.ds(r0, C), hs].astype(F32)).astype(BF16)

    def chunk(ci, carry):
        r0 = pl.multiple_of(ci * C, C)
        for h in range(HGRN_HEADS):
            head_chunk(h, r0)
        return carry

    lax.fori_loop(0, S // C, chunk, 0)


def hgrn2(hq, hlf, hv, hg, norm_g):
    B, S, _ = hq.shape
    C = HGRN_CHUNK
    assert S % C == 0
    tri = jnp.asarray(np.tril(np.ones((C, C))), BF16)
    spec = lambda n: pl.BlockSpec((1, S, n), lambda b: (b, 0, 0))
    return pl.pallas_call(
        _hgrn_kernel,
        grid=(B,),
        in_specs=[spec(HGRN_WIDTH), spec(HGRN_WIDTH), spec(HGRN_VWIDTH), spec(HGRN_VWIDTH),
                  pl.BlockSpec((1, HGRN_DV), lambda b: (0, 0)),
                  pl.BlockSpec((C, C), lambda b: (0, 0))],
        out_specs=spec(HGRN_VWIDTH),
        out_shape=jax.ShapeDtypeStruct((B, S, HGRN_VWIDTH), BF16),
        scratch_shapes=[pltpu.VMEM((HGRN_HEADS, HGRN_DV, HGRN_DK), F32)],
        compiler_params=_cparams(1),
        name="hgrn2",
    )(hq, hlf, hv, hg, norm_g.reshape(1, HGRN_DV).astype(F32), tri)


def _merge_kernel(ya_ref, yb_ref, gma_ref, gmb_ref, x_ref, gt_ref, lg_ref, lbias_ref,
                  wa_ref, wb_ref, wo_ref, o_ref):
    pa = _dot(ya_ref[0], wa_ref[...])
    pb = _dot(yb_ref[0], wb_ref[...])
    merged = gma_ref[0].astype(F32) * pa + gmb_ref[0].astype(F32) * pb
    y = _dot(merged.astype(BF16), wo_ref[...])
    z = DEEPNORM_ALPHA * x_ref[0] + (1.0 + gt_ref[0]) * y
    o_ref[0] = _layer_norm(z, lg_ref[...], lbias_ref[...])


def merge_out(ya, yb, gma, gmb, x, gt, ln_g, ln_b, wa, wb, wo):
    B, S, D = x.shape
    tm = min(TOKEN_TILE, S)
    tok = lambda n: pl.BlockSpec((1, tm, n), lambda b, i: (b, i, 0))
    return pl.pallas_call(
        _merge_kernel,
        grid=(B, S // tm),
        in_specs=[tok(NSA_WIDTH), tok(HGRN_VWIDTH), tok(D), tok(D), tok(D),
                  pl.BlockSpec((1, 1, D), lambda b, i: (b, 0, 0)),
                  _resident((1, D)), _resident((1, D)),
                  _resident(wa.shape), _resident(wb.shape), _resident(wo.shape)],
        out_specs=tok(D),
        out_shape=jax.ShapeDtypeStruct((B, S, D), F32),
        compiler_params=_cparams(2),
        name="merge_out",
    )(ya, yb, gma, gmb, x, gt, ln_g.reshape(1, D), ln_b.reshape(1, D), wa, wb, wo)


MLP_COLS = 1024


def _mlp_kernel(x_ref, sc_ref, sh_ref, gt_ref, lg_ref, lbias_ref, w1_ref, w2_ref, o_ref):
    x = x_ref[0]
    u = (x * (1.0 + sc_ref[0]) + sh_ref[0]).astype(BF16)
    y = jnp.zeros(x.shape, F32)
    for c in range(MLP_HIDDEN // MLP_COLS):
        h = jnp.maximum(_dot(u, w1_ref[:, c * MLP_COLS:(c + 1) * MLP_COLS]), 0.0)
        y = y + _dot((h * h).astype(BF16), w2_ref[c * MLP_COLS:(c + 1) * MLP_COLS, :])
    z = DEEPNORM_ALPHA * x + (1.0 + gt_ref[0]) * y
    o_ref[0] = _layer_norm(z, lg_ref[...], lbias_ref[...])


def mlp(x, sc, sh, gt, ln_g, ln_b, w1, w2):
    B, S, D = x.shape
    tm = min(TOKEN_TILE, S)
    tok = pl.BlockSpec((1, tm, D), lambda b, i: (b, i, 0))
    per_b = pl.BlockSpec((1, 1, D), lambda b, i: (b, 0, 0))
    return pl.pallas_call(
        _mlp_kernel,
        grid=(B, S // tm),
        in_specs=[tok, per_b, per_b, per_b, _resident((1, D)), _resident((1, D)),
                  _resident(w1.shape), _resident(w2.shape)],
        out_specs=tok,
        out_shape=jax.ShapeDtypeStruct((B, S, D), F32),
        compiler_params=_cparams(2),
        name="mlp",
    )(x, sc, sh, gt, ln_g.reshape(1, D), ln_b.reshape(1, D), w1, w2)


def _rope_tables(S):
    inv = 1.0 / (ROPE_THETA ** (jnp.arange(0, NSA_DH, 2, dtype=F32) / NSA_DH))
    ang = jnp.arange(S, dtype=F32)[:, None] * inv[None, :]
    cos, sin = jnp.cos(ang), jnp.sin(ang)
    reps = LANES // NSA_DH
    return (jnp.tile(jnp.concatenate([cos, cos], axis=1), (1, reps)),
            jnp.tile(jnp.concatenate([-sin, sin], axis=1), (1, reps)))


def nsa_layout(kv, ga):
    B, _, S, dh = kv.shape
    G = NSA_GROUPS
    vt = kv[:, 2 * G:].reshape(B, 2 * G, S // K_CHUNK, K_CHUNK, dh).swapaxes(-1, -2)
    gates_t = ga.reshape(B, S, G, LANES)[..., :2 * SUBLANES].transpose(0, 2, 3, 1)
    return vt, gates_t


def kernel(x, c, w_in, b_in, cmp_pe_k, cmp_pe_v, cmp_wk1, cmp_wk2, cmp_wv1, cmp_wv2, hgrn_lb_logits, hgrn_norm_g, w_branch_a, w_branch_b, w_out, w_ada, b_ada, ln1_g, ln1_b, w_mlp1, w_mlp2, ln2_g, ln2_b):
    B, S, D = x.shape
    G = NSA_GROUPS
    lb_all = jnp.cumsum(jax.nn.softmax(hgrn_lb_logits.astype(F32), axis=0), axis=0)
    lb_all = lb_all - lb_all[0:1]
    cos_t, sin_t = _rope_tables(S)
    mod = adaln_mod(c, w_ada, b_ada)
    for l in range(DEPTH):
        sh1, sc1, gt1, sh2, sc2, gt2 = [mod[l, :, None, i * D:(i + 1) * D] for i in range(6)]
        wts = _prep_in_proj_weights(w_in[l], b_in[l])
        q, kx, kv, ga, hq, hlf, hv, hg, gma, gmb = in_proj(x, sc1, sh1, cos_t, sin_t, lb_all[l].reshape(1, -1), wts)
        pe = jnp.stack([cmp_pe_k[l].reshape(1, -1), cmp_pe_v[l].reshape(1, -1)])
        w1 = jnp.stack([cmp_wk1[l], cmp_wv1[l]]).astype(BF16)
        w2 = jnp.pad(jnp.stack([cmp_wk2[l], cmp_wv2[l]]), ((0, 0), (0, 0), (0, LANES - NSA_DH))).astype(BF16)
        cmp = nsa_compress(kv[:, :2 * G], pe, w1, w2)
        vt, gates_t = nsa_layout(kv, ga)
        ya = nsa_attend(q, kx, vt, cmp, gates_t)
        yb = hgrn2(hq, hlf, hv, hg, hgrn_norm_g[l])
        x = merge_out(ya, yb, gma, gmb, x, gt1, ln1_g[l], ln1_b[l],
                      w_branch_a[l].astype(BF16), w_branch_b[l].astype(BF16), w_out[l].astype(BF16))
        x = mlp(x, sc2, sh2, gt2, ln2_g[l], ln2_b[l], w_mlp1[l].astype(BF16), w_mlp2[l].astype(BF16))
    return x
```

```python
import functools

import numpy as np
import jax
import jax.numpy as jnp
from jax import lax
from jax.experimental import pallas as pl
from jax.experimental.pallas import tpu as pltpu

D_MODEL = 1024
DEPTH = 2
NSA_HEADS = 8
NSA_GROUPS = 2
NSA_REP = NSA_HEADS // NSA_GROUPS
NSA_DH = 64
NSA_WIDTH = NSA_HEADS * NSA_DH
NSA_KV_WIDTH = NSA_GROUPS * NSA_DH
CMP_LEN = 32
CMP_STRIDE = 16
CMP_HIDDEN = 2 * NSA_DH
SEL_LEN = 64
SEL_TOPK = 8
FORCE_SCORE = 1.0e4
WINDOW = 512
HGRN_HEADS = 4
HGRN_DK = 128
HGRN_DV = 128
HGRN_WIDTH = HGRN_HEADS * HGRN_DK
HGRN_VWIDTH = HGRN_HEADS * HGRN_DV
MLP_HIDDEN = 4 * D_MODEL
ROPE_THETA = 10000.0
LN_EPS = 1e-5
RMS_EPS = 1e-6
DEEPNORM_ALPHA = (2 * DEPTH) ** 0.25
IN_SIZES = (NSA_WIDTH,) + (NSA_KV_WIDTH,) * 6 + (3 * NSA_HEADS,) + (HGRN_WIDTH, HGRN_WIDTH, HGRN_VWIDTH, HGRN_VWIDTH) + (D_MODEL, D_MODEL)
IN_OFFSETS = [0] + [int(v) for v in np.cumsum(IN_SIZES)]

LANES = 128
SUBLANES = 8
VMEM_LIMIT = 48 * 1024 * 1024
TOKEN_TILE = 512
Q_TILE = 128
K_CHUNK = 128
HGRN_CHUNK = 64
HGRN_SUB = 8
HGRN_MAX_STEP_DECAY = 7.5
NEG_BIG = -1e30
LOG2E = 1.4426950408889634
Q_SCALE = NSA_DH ** -0.5 * LOG2E

F32 = jnp.float32
BF16 = jnp.bfloat16


def _cparams(n_grid):
    return pltpu.CompilerParams(dimension_semantics=("arbitrary",) * n_grid, vmem_limit_bytes=VMEM_LIMIT)


def _resident(shape):
    nd = len(shape)
    return pl.BlockSpec(shape, lambda *_: (0,) * nd, pipeline_mode=pl.Buffered(1))


def _dot(a, b):
    return jnp.dot(a, b, preferred_element_type=F32)


def _dot_nt(a, b):
    return lax.dot_general(a, b, (((1,), (1,)), ((), ())), preferred_element_type=F32)


def _dot_tn(a, b):
    return lax.dot_general(a, b, (((0,), (0,)), ((), ())), preferred_element_type=F32)


def _sigmoid(x):
    return 1.0 / (1.0 + jnp.exp(-x))


def _silu(x):
    return x * _sigmoid(x)


def _layer_norm(z, g, b):
    mu = jnp.mean(z, axis=-1, keepdims=True)
    zc = z - mu
    var = jnp.mean(zc * zc, axis=-1, keepdims=True)
    return zc * lax.rsqrt(var + LN_EPS) * g + b


def _adaln_kernel(c_ref, w_ref, b_ref, o_ref):
    cond = _silu(c_ref[...]).astype(BF16)
    o_ref[0] = _dot(cond, w_ref[0]) + b_ref[0]


def adaln_mod(c, w_ada, b_ada):
    L, D, N = w_ada.shape
    B = c.shape[0]
    tn = D
    return pl.pallas_call(
        _adaln_kernel,
        grid=(L, N // tn),
        in_specs=[
            pl.BlockSpec((B, D), lambda l, j: (0, 0)),
            pl.BlockSpec((1, D, tn), lambda l, j: (l, 0, j)),
            pl.BlockSpec((1, 1, tn), lambda l, j: (l, 0, j)),
        ],
        out_specs=pl.BlockSpec((1, B, tn), lambda l, j: (l, 0, j)),
        out_shape=jax.ShapeDtypeStruct((L, B, N), F32),
        compiler_params=_cparams(2),
        name="adaln_mod",
    )(c, w_ada.astype(BF16), b_ada.reshape(L, 1, N))


N_ROPE = NSA_WIDTH + 3 * NSA_KV_WIDTH
N_NSA = N_ROPE + 3 * NSA_KV_WIDTH
N_GATE = NSA_GROUPS * LANES
N_HGRN = 2 * HGRN_WIDTH + 2 * HGRN_VWIDTH
N_MERGE = 2 * D_MODEL


def _in_proj_kernel(x_ref, sc_ref, sh_ref, cos_ref, sin_ref, lb_ref,
                    wn_ref, bn_ref, wg_ref, bg_ref, wh_ref, bh_ref, wm_ref, bm_ref,
                    q_ref, kx_ref, kv_ref, vt_ref, ga_ref, hq_ref, hlf_ref, hv_ref, hg_ref, gma_ref, gmb_ref):
    u = (x_ref[0] * (1.0 + sc_ref[0]) + sh_ref[0]).astype(BF16)
    tm = u.shape[0]
    cos = cos_ref[...]
    sin = sin_ref[...]
    lane = lax.broadcasted_iota(jnp.int32, cos.shape, 1)
    first_half = (lane % NSA_DH) < (NSA_DH // 2)
    low = lane < NSA_DH
    pos = pl.program_id(1) * tm + lax.broadcasted_iota(jnp.int32, cos.shape, 0)
    block_onehot = jnp.where(lane - NSA_DH == pos // SEL_LEN, 1.0, 0.0)

    def heads(t, upper):
        return jnp.where(low, t, upper), jnp.where(low, pltpu.roll(t, NSA_DH, 1), upper)

    n_q, n_rope = NSA_HEADS // 2, N_ROPE // LANES
    for i2 in range(0, N_NSA // LANES, 2):
        t2 = _dot(u, wn_ref[:, i2 * LANES:(i2 + 2) * LANES]) + bn_ref[:, i2 * LANES:(i2 + 2) * LANES]
        for i in (i2, i2 + 1):
            t = t2[:, (i - i2) * LANES:(i - i2 + 1) * LANES]
            if i < n_rope:
                rot = jnp.where(first_half, pltpu.roll(t, LANES - NSA_DH // 2, 1), pltpu.roll(t, NSA_DH // 2, 1))
                t = t * cos + rot * sin
            if i < n_q:
                for j, piece in enumerate(heads(t * Q_SCALE, 0.0)):
                    q_ref[0, 2 * i + j] = piece.astype(BF16)
            elif i == n_q or i == n_rope:
                for j in range(2):
                    kv_ref[0, (2 if i == n_rope else 0) + j] = t[:, j * NSA_DH:(j + 1) * NSA_DH].astype(BF16)
            elif i < n_rope:
                kind = i - n_q - 1
                for j, piece in enumerate(heads(t, block_onehot if kind == 0 else 0.0)):
                    kx_ref[0, 2 * kind + j] = piece.astype(BF16)
            else:
                kind = i - n_rope - 1
                tt = t.T.astype(BF16)
                for j in range(2):
                    vt_ref[0, 2 * kind + j] = tt[j * NSA_DH:(j + 1) * NSA_DH]
    gates = _sigmoid(_dot(u, wg_ref[...]) + bg_ref[...])
    for g in range(NSA_GROUPS):
        ga_ref[0, g] = gates[:, g * LANES:(g + 1) * LANES].T[0:2 * SUBLANES]
    W = HGRN_WIDTH
    hq = _dot(u, wh_ref[:, 0:W]) + bh_ref[:, 0:W]
    hq_ref[0] = (_silu(hq) * (HGRN_DK ** -0.5)).astype(BF16)
    z = _dot(u, wh_ref[:, W:2 * W]) + bh_ref[:, W:2 * W]
    lb = lb_ref[...]
    log_sig = jnp.minimum(z, 0.0) - jnp.log1p(jnp.exp(-jnp.abs(z)))
    a = jnp.log(lb)
    bb = jnp.log1p(-lb) + log_sig
    hlf_ref[0] = jnp.maximum(a, bb) + jnp.log1p(jnp.exp(-jnp.abs(a - bb)))
    hv_ref[0] = (_dot(u, wh_ref[:, 2 * W:3 * W]) + bh_ref[:, 2 * W:3 * W]).astype(BF16)
    hg_ref[0] = _silu(_dot(u, wh_ref[:, 3 * W:4 * W]) + bh_ref[:, 3 * W:4 * W]).astype(BF16)
    for i, ref in enumerate((gma_ref, gmb_ref)):
        for j in range(2):
            c0 = i * D_MODEL + j * (D_MODEL // 2)
            c1 = c0 + D_MODEL // 2
            ref[0, :, j * (D_MODEL // 2):(j + 1) * (D_MODEL // 2)] = _sigmoid(
                _dot(u, wm_ref[:, c0:c1]) + bm_ref[:, c0:c1]).astype(BF16)


def in_proj(x, sc, sh, cos_t, sin_t, lb, wts):
    B, S, D = x.shape
    tm = min(TOKEN_TILE, S)
    assert S // SEL_LEN <= LANES - NSA_DH
    wn, bn, wg, bg, wh, bh, wm, bm = wts
    tok = lambda n: pl.BlockSpec((1, tm, n), lambda b, i: (b, i, 0))
    per_b = pl.BlockSpec((1, 1, D), lambda b, i: (b, 0, 0))
    tab = pl.BlockSpec((tm, LANES), lambda b, i: (i, 0))
    out_shape = (
        jax.ShapeDtypeStruct((B, NSA_HEADS, S, LANES), BF16),
        jax.ShapeDtypeStruct((B, 2 * NSA_GROUPS, S, LANES), BF16),
        jax.ShapeDtypeStruct((B, 2 * NSA_GROUPS, S, NSA_DH), BF16),
        jax.ShapeDtypeStruct((B, 2 * NSA_GROUPS, NSA_DH, S), BF16),
        jax.ShapeDtypeStruct((B, NSA_GROUPS, 2 * SUBLANES, S), F32),
        jax.ShapeDtypeStruct((B, S, HGRN_WIDTH), BF16),
        jax.ShapeDtypeStruct((B, S, HGRN_WIDTH), F32),
        jax.ShapeDtypeStruct((B, S, HGRN_VWIDTH), BF16),
        jax.ShapeDtypeStruct((B, S, HGRN_VWIDTH), BF16),
        jax.ShapeDtypeStruct((B, S, D), BF16),
        jax.ShapeDtypeStruct((B, S, D), BF16),
    )
    out_specs = (
        pl.BlockSpec((1, NSA_HEADS, tm, LANES), lambda b, i: (b, 0, i, 0)),
        pl.BlockSpec((1, 2 * NSA_GROUPS, tm, LANES), lambda b, i: (b, 0, i, 0)),
        pl.BlockSpec((1, 2 * NSA_GROUPS, tm, NSA_DH), lambda b, i: (b, 0, i, 0)),
        pl.BlockSpec((1, 2 * NSA_GROUPS, NSA_DH, tm), lambda b, i: (b, 0, 0, i)),
        pl.BlockSpec((1, NSA_GROUPS, 2 * SUBLANES, tm), lambda b, i: (b, 0, 0, i)),
        tok(HGRN_WIDTH), tok(HGRN_WIDTH), tok(HGRN_VWIDTH), tok(HGRN_VWIDTH), tok(D), tok(D),
    )
    return pl.pallas_call(
        _in_proj_kernel,
        grid=(B, S // tm),
        in_specs=[tok(D), per_b, per_b, tab, tab, _resident(lb.shape),
                  _resident(wn.shape), _resident(bn.shape), _resident(wg.shape), _resident(bg.shape),
                  _resident(wh.shape), _resident(bh.shape), _resident(wm.shape), _resident(bm.shape)],
        out_specs=out_specs,
        out_shape=out_shape,
        compiler_params=_cparams(2),
        name="in_proj",
    )(x, sc, sh, cos_t, sin_t, lb, wn, bn, wg, bg, wh, bh, wm, bm)


def _prep_in_proj_weights(w_in_l, b_in_l):
    o = IN_OFFSETS
    col = lambda i: (w_in_l[:, o[i]:o[i + 1]], b_in_l[o[i]:o[i + 1]])
    q_a, k_c, v_c, k_s, v_s, k_w, v_w, g_a, q_b, f_b, i_b, g_b, gm_a, gm_b = [col(i) for i in range(14)]

    def cat(parts):
        return (jnp.concatenate([p[0] for p in parts], axis=1).astype(BF16),
                jnp.concatenate([p[1] for p in parts], axis=0).reshape(1, -1).astype(F32))

    wn, bn = cat([q_a, k_c, k_s, k_w, v_c, v_s, v_w])
    per_group = 3 * NSA_REP
    gw = jnp.zeros((w_in_l.shape[0], N_GATE), w_in_l.dtype)
    gb = jnp.zeros((N_GATE,), b_in_l.dtype)
    for g in range(NSA_GROUPS):
        gw = gw.at[:, g * LANES:g * LANES + per_group].set(g_a[0][:, g * per_group:(g + 1) * per_group])
        gb = gb.at[g * LANES:g * LANES + per_group].set(g_a[1][g * per_group:(g + 1) * per_group])
    wg, bg = gw.astype(BF16), gb.reshape(1, -1).astype(F32)
    wh, bh = cat([q_b, f_b, i_b, g_b])
    wm, bm = cat([gm_a, gm_b])
    return wn, bn, wg, bg, wh, bh, wm, bm


def _compress_kernel(t_ref, pe_ref, w1_ref, w2_ref, o_ref):
    half = CMP_STRIDE * NSA_DH
    t = t_ref[0, 0]
    nrow = t.shape[0]
    a = _dot(t, w1_ref[0, 0:half, :])
    b = _dot(t, w1_ref[0, half:2 * half, :])
    pe = jnp.broadcast_to(pe_ref[0], (8, 2 * half)).astype(BF16)
    c = _dot(pe, w1_ref[0])[0:1]
    h = a + pltpu.roll(b, nrow - 1, 0) + c
    o_ref[0, 0] = _dot(_silu(h).astype(BF16), w2_ref[0]).astype(BF16)


def nsa_compress(kv, pe, w1, w2):
    B, _, S, dh = kv.shape
    nrow = S // CMP_STRIDE
    G = NSA_GROUPS
    kv_rows = kv.reshape(B, 2 * G, nrow, CMP_STRIDE * dh)
    return pl.pallas_call(
        _compress_kernel,
        grid=(B, 2, G),
        in_specs=[
            pl.BlockSpec((1, 1, nrow, CMP_STRIDE * dh), lambda b, s, g: (b, s * G + g, 0, 0)),
            pl.BlockSpec((1, 1, CMP_LEN * dh), lambda b, s, g: (s, 0, 0)),
            pl.BlockSpec((1, CMP_LEN * dh, CMP_HIDDEN), lambda b, s, g: (s, 0, 0)),
            pl.BlockSpec((1, CMP_HIDDEN, LANES), lambda b, s, g: (s, 0, 0)),
        ],
        out_specs=pl.BlockSpec((1, 1, nrow, LANES), lambda b, s, g: (b, s * G + g, 0, 0)),
        out_shape=jax.ShapeDtypeStruct((B, 2 * G, nrow, LANES), BF16),
        compiler_params=_cparams(3),
        name="nsa_compress",
    )(kv_rows, pe, w1, w2)


def _nsa_kernel(q_ref, kc_ref, vc_ref, ks_ref, kw_ref, vst_ref, vwt_ref, gt_ref, ovt_ref, pm_ref,
                o_ref, *, n_sel, n_tiles):
    for qs in range(n_tiles):
        pl.when(pl.program_id(2) == qs)(functools.partial(
            _nsa_tile, qs, q_ref, kc_ref, vc_ref, ks_ref, kw_ref, vst_ref, vwt_ref, gt_ref, ovt_ref, pm_ref,
            o_ref, n_sel))


def _round_up(x, m):
    return -(-x // m) * m


def _nsa_tile(qb, q_ref, kc_ref, vc_ref, ks_ref, kw_ref, vst_ref, vwt_ref, gt_ref, ovt_ref, pm_ref, o_ref, n_sel):
    R, TQ, dh = NSA_REP, Q_TILE, NSA_DH
    cols = R * TQ
    s0 = qb * TQ
    q = q_ref[0].reshape(cols, LANES)
    t_lane = s0 + (lax.broadcasted_iota(jnp.int32, (1, cols), 1) % TQ)

    ncb = min(kc_ref.shape[2], _round_up((s0 + TQ - CMP_LEN) // CMP_STRIDE + 1, 2 * SUBLANES))
    sc = _dot_nt(kc_ref[0, 0, 0:ncb, :], q)
    n_sub = lax.broadcasted_iota(jnp.int32, (ncb, cols), 0)
    mask_c = n_sub * CMP_STRIDE + (CMP_LEN - 1) <= t_lane
    sc = jnp.where(mask_c, sc, NEG_BIG)
    mc = jnp.max(sc, axis=0, keepdims=True)
    ec = jnp.where(mask_c, jnp.exp2(sc - mc), 0.0)
    pc = ec * (1.0 / jnp.maximum(jnp.sum(ec, axis=0, keepdims=True), 1e-30))
    o_c = _dot_tn(vc_ref[0, 0, 0:ncb, :], pc.astype(BF16))[0:dh]

    nb_live = (s0 + TQ) // SEL_LEN
    nb = min(ovt_ref.shape[0], _round_up(nb_live, 2 * SUBLANES))
    psum = pc[:, 0:TQ]
    for r in range(1, R):
        psum = psum + pc[:, r * TQ:(r + 1) * TQ]
    p_hi = psum.astype(BF16)
    p_lo = (psum - p_hi.astype(F32)).astype(BF16)
    ovt = ovt_ref[0:nb, 0:ncb]
    imp = _dot(ovt, p_hi) + _dot(ovt, p_lo)
    jb = lax.broadcasted_iota(jnp.int32, (nb, TQ), 0)
    tb = (s0 + lax.broadcasted_iota(jnp.int32, (nb, TQ), 1)) // SEL_LEN
    valid = jb <= tb
    forced = jnp.where(valid, jnp.where(jb == 0, 1.0, jnp.where(jb >= tb - 1, 1.0, 0.0)), 0.0)
    score = jnp.where(forced > 0.5, FORCE_SCORE, jnp.where(valid, imp, -1.0))
    rank = jnp.zeros((nb, TQ), F32)
    for i in range(nb_live):
        si = score[i:i + 1, :]
        tie_first = jnp.where(jb > i, 1.0, 0.0)
        rank = rank + jnp.where(si > score, 1.0, jnp.where(si == score, tie_first, 0.0))
    not_sel = jnp.where(rank < n_sel, 0.0, 1.0).astype(BF16)
    bias = _dot_tn(not_sel, pm_ref[0:nb, :]).astype(BF16)
    q_sel = q + jnp.concatenate([bias] * R, axis=0)

    kidx = lax.broadcasted_iota(jnp.int32, (K_CHUNK, cols), 0)
    tq = lax.broadcasted_iota(jnp.int32, (K_CHUNK, cols), 1) % TQ

    causal = kidx <= tq

    def block_softmax(k_ref, vt_ref, qmat, c0, n, band_first=False, causal_last=True):
        s = _dot_nt(k_ref[0, 0, c0 * K_CHUNK:(c0 + n) * K_CHUNK, :], qmat)
        parts = [s[i * K_CHUNK:(i + 1) * K_CHUNK] for i in range(n)]
        if band_first:
            parts[0] = jnp.where(tq < kidx, parts[0], NEG_BIG)
        if causal_last:
            parts[-1] = jnp.where(causal, parts[-1], NEG_BIG)
        s = jnp.concatenate(parts, axis=0) if n > 1 else parts[0]
        m = jnp.max(s, axis=0, keepdims=True)
        p = jnp.exp2(s - m)
        l = jnp.sum(p, axis=0, keepdims=True)
        vt = vt_ref[0, 0, :, c0 * K_CHUNK:(c0 + n) * K_CHUNK]
        return m, l, _dot(vt, p.astype(BF16))

    n_win = WINDOW // K_CHUNK
    _, l_w, acc_w = block_softmax(kw_ref, vwt_ref, q, max(qb - n_win, 0), min(qb, n_win) + 1, band_first=qb >= n_win)
    o_w = acc_w * (1.0 / l_w)

    _, l_s, acc_s = block_softmax(ks_ref, vst_ref, q_sel, 0, qb + 1)
    o_s = acc_s * (1.0 / l_s)

    gate = gt_ref[0, 0]
    pieces = []
    for r in range(R):
        sl = slice(r * TQ, (r + 1) * TQ)
        o_r = (gate[3 * r:3 * r + 1, :] * o_c[:, sl] + gate[3 * r + 1:3 * r + 2, :] * o_s[:, sl]
               + gate[3 * r + 2:3 * r + 3, :] * o_w[:, sl])
        pieces.append(o_r.T)
    o_ref[0] = jnp.concatenate(pieces, axis=1).astype(BF16)


def nsa_attend(q, kx, vt, cmp, gates_t):
    B, H, S, _ = q.shape
    G, R, dh = NSA_GROUPS, NSA_REP, NSA_DH
    ncb = S // CMP_STRIDE
    nb = S // SEL_LEN
    assert (S % Q_TILE == 0 and Q_TILE == K_CHUNK and WINDOW % K_CHUNK == 0 and K_CHUNK % SEL_LEN == 0
            and 3 * R <= 2 * SUBLANES)
    cstart = np.arange(ncb) * CMP_STRIDE
    sstart = np.arange(nb) * SEL_LEN
    overlap = ((cstart[:, None] < sstart[None, :] + SEL_LEN) & (cstart[:, None] + CMP_LEN > sstart[None, :]))
    ovt = jnp.asarray(overlap.T, BF16)
    place = np.zeros((nb, LANES), np.float32)
    place[np.arange(nb), dh + np.arange(nb)] = NEG_BIG
    pm = jnp.asarray(place, BF16)
    kx_spec = lambda idx: pl.BlockSpec((1, 1, S, LANES), lambda b, g, i, idx=idx: (b, idx * G + g, 0, 0))
    vt_spec = lambda idx: pl.BlockSpec((1, 1, dh, S), lambda b, g, i, idx=idx: (b, idx * G + g, 0, 0))
    cmp_spec = lambda idx: pl.BlockSpec((1, 1, ncb, LANES), lambda b, g, i, idx=idx: (b, idx * G + g, 0, 0))
    return pl.pallas_call(
        functools.partial(_nsa_kernel, n_sel=min(SEL_TOPK, nb), n_tiles=S // Q_TILE),
        grid=(B, G, S // Q_TILE),
        in_specs=[
            pl.BlockSpec((1, R, Q_TILE, LANES), lambda b, g, i: (b, g, i, 0)),
            cmp_spec(0), cmp_spec(1),
            kx_spec(0), kx_spec(1),
            vt_spec(0), vt_spec(1),
            pl.BlockSpec((1, 1, 2 * SUBLANES, Q_TILE), lambda b, g, i: (b, g, 0, i)),
            pl.BlockSpec(ovt.shape, lambda b, g, i: (0, 0)),
            pl.BlockSpec(pm.shape, lambda b, g, i: (0, 0)),
        ],
        out_specs=pl.BlockSpec((1, Q_TILE, R * dh), lambda b, g, i: (b, i, g)),
        out_shape=jax.ShapeDtypeStruct((B, S, H * dh), BF16),
        compiler_params=_cparams(3),
        name="nsa_attend",
    )(q, cmp, cmp, kx, kx, vt, vt, gates_t, ovt, pm)


def _hgrn_kernel(q_ref, lf_ref, v_ref, g_ref, ng_ref, tri_ref, o_ref, st_ref):
    C, SB = HGRN_CHUNK, HGRN_SUB
    NBK = C // SB
    S = q_ref.shape[1]
    st_ref[...] = jnp.zeros(st_ref.shape, F32)
    row = lax.broadcasted_iota(jnp.int32, (C, HGRN_DK), 0)
    sub_row = lax.broadcasted_iota(jnp.int32, (SB, 1), 0)
    cr = lax.broadcasted_iota(jnp.int32, (C, C), 0)
    cc = lax.broadcasted_iota(jnp.int32, (C, C), 1)
    diag_mask = (cr // SB == cc // SB) & (cc <= cr)

    def head_chunk(h, r0, bounded_decay):
        hs = slice(h * HGRN_DK, (h + 1) * HGRN_DK)
        q = q_ref[0, pl.ds(r0, C), hs].astype(F32)
        lf = lf_ref[0, pl.ds(r0, C), hs]
        v_bf = v_ref[0, pl.ds(r0, C), hs]
        v = v_bf.astype(F32)
        kh = 1.0 - jnp.exp(lf)
        tri = tri_ref[...]
        lf0 = lf.astype(BF16)
        lf1 = (lf - lf0.astype(F32)).astype(BF16)
        lf2 = (lf - lf0.astype(F32) - lf1.astype(F32)).astype(BF16)
        b = _dot(tri, lf0) + _dot(tri, lf1) + _dot(tri, lf2)
        b_last = b[C - 1:C, :]
        st = st_ref[h]
        o = _dot_nt((q * jnp.exp(b)).astype(BF16), st.astype(BF16))
        b_end = jnp.concatenate(
            [jnp.broadcast_to(b[(j + 1) * SB - 1:(j + 1) * SB, :], (SB, HGRN_DK)) for j in range(NBK)], axis=0)
        k_end = kh * jnp.exp(b_end - b)
        q_parts, k_parts = [], []
        for j in range(NBK - 1):
            lo = (j + 1) * SB
            qj = q[lo:] * jnp.exp(b[lo:] - b[lo - 1:lo, :])
            q_parts.append(jnp.concatenate([jnp.zeros((lo, HGRN_DK), F32), qj], axis=0))
            k_parts.append(jnp.where((row >= j * SB) & (row < lo), k_end, 0.0))
        q_cat = jnp.concatenate(q_parts, axis=1).astype(BF16)
        k_cat = jnp.concatenate(k_parts, axis=1).astype(BF16)
        a_off = _dot_nt(q_cat, k_cat)
        if bounded_decay:
            b_start = jnp.concatenate([jnp.zeros((SB, HGRN_DK), F32), b_end[:C - SB]], axis=0)
            a_dg = _dot_nt((q * jnp.exp(b - b_start)).astype(BF16), (kh * jnp.exp(b_start - b)).astype(BF16))
            o = o + _dot(jnp.where(diag_mask, a_dg, a_off).astype(BF16), v_bf)
        else:
            o = o + _dot(a_off.astype(BF16), v_bf)
            diag = []
            for j in range(NBK):
                sl = slice(j * SB, (j + 1) * SB)
                qj, bj, kj, vj = q[sl], b[sl], kh[sl], v[sl]
                oj = jnp.zeros((SB, HGRN_DV), F32)
                for s in range(SB):
                    w = jnp.exp(jnp.minimum(bj - bj[s:s + 1, :], 0.0))
                    a = jnp.sum(qj * kj[s:s + 1, :] * w, axis=-1, keepdims=True)
                    a = jnp.where(sub_row >= s, a, 0.0)
                    oj = oj + a * vj[s:s + 1, :]
                diag.append(oj)
            o = o + jnp.concatenate(diag, axis=0)
        k_last = (kh * jnp.exp(b_last - b)).astype(BF16)
        st_ref[h] = st * jnp.exp(b_last) + _dot_tn(v_bf, k_last)
        o = o * lax.rsqrt(jnp.mean(o * o, axis=-1, keepdims=True) + RMS_EPS) * ng_ref[...]
        o_ref[0, pl.ds(r0, C), hs] = (o * g_ref[0, pl.ds(r0, C), hs].astype(F32)).astype(BF16)

    def chunk(bounded_decay, ci, carry):
        r0 = pl.multiple_of(ci * C, C)
        for h in range(HGRN_HEADS):
            head_chunk(h, r0, bounded_decay)
        return carry

    bounded = jnp.min(lf_ref[0]) >= -HGRN_MAX_STEP_DECAY

    @pl.when(bounded)
    def _():
        lax.fori_loop(0, S // C, functools.partial(chunk, True), 0)

    @pl.when(jnp.logical_not(bounded))
    def _():
        lax.fori_loop(0, S // C, functools.partial(chunk, False), 0)


def hgrn2(hq, hlf, hv, hg, norm_g):
    B, S, _ = hq.shape
    C = HGRN_CHUNK
    assert S % C == 0
    tri = jnp.asarray(np.tril(np.ones((C, C))), BF16)
    spec = lambda n: pl.BlockSpec((1, S, n), lambda b: (b, 0, 0))
    return pl.pallas_call(
        _hgrn_kernel,
        grid=(B,),
        in_specs=[spec(HGRN_WIDTH), spec(HGRN_WIDTH), spec(HGRN_VWIDTH), spec(HGRN_VWIDTH),
                  pl.BlockSpec((1, HGRN_DV), lambda b: (0, 0)),
                  pl.BlockSpec((C, C), lambda b: (0, 0))],
        out_specs=spec(HGRN_VWIDTH),
        out_shape=jax.ShapeDtypeStruct((B, S, HGRN_VWIDTH), BF16),
        scratch_shapes=[pltpu.VMEM((HGRN_HEADS, HGRN_DV, HGRN_DK), F32)],
        compiler_params=_cparams(1),
        name="hgrn2",
    )(hq, hlf, hv, hg, norm_g.reshape(1, HGRN_DV).astype(F32), tri)


def _merge_kernel(ya_ref, yb_ref, gma_ref, gmb_ref, x_ref, gt_ref, lg_ref, lbias_ref,
                  wa_ref, wb_ref, wo_ref, o_ref):
    pa = _dot(ya_ref[0], wa_ref[...])
    pb = _dot(yb_ref[0], wb_ref[...])
    merged = gma_ref[0].astype(F32) * pa + gmb_ref[0].astype(F32) * pb
    y = _dot(merged.astype(BF16), wo_ref[...])
    z = DEEPNORM_ALPHA * x_ref[0] + (1.0 + gt_ref[0]) * y
    o_ref[0] = _layer_norm(z, lg_ref[...], lbias_ref[...])


def merge_out(ya, yb, gma, gmb, x, gt, ln_g, ln_b, wa, wb, wo):
    B, S, D = x.shape
    tm = min(TOKEN_TILE, S)
    tok = lambda n: pl.BlockSpec((1, tm, n), lambda b, i: (b, i, 0))
    return pl.pallas_call(
        _merge_kernel,
        grid=(B, S // tm),
        in_specs=[tok(NSA_WIDTH), tok(HGRN_VWIDTH), tok(D), tok(D), tok(D),
                  pl.BlockSpec((1, 1, D), lambda b, i: (b, 0, 0)),
                  _resident((1, D)), _resident((1, D)),
                  _resident(wa.shape), _resident(wb.shape), _resident(wo.shape)],
        out_specs=tok(D),
        out_shape=jax.ShapeDtypeStruct((B, S, D), F32),
        compiler_params=_cparams(2),
        name="merge_out",
    )(ya, yb, gma, gmb, x, gt, ln_g.reshape(1, D), ln_b.reshape(1, D), wa, wb, wo)


MLP_COLS = 1024


def _mlp_kernel(x_ref, sc_ref, sh_ref, gt_ref, lg_ref, lbias_ref, w1_ref, w2_ref, o_ref):
    x = x_ref[0]
    u = (x * (1.0 + sc_ref[0]) + sh_ref[0]).astype(BF16)
    y = jnp.zeros(x.shape, F32)
    for c in range(MLP_HIDDEN // MLP_COLS):
        h = jnp.maximum(_dot(u, w1_ref[:, c * MLP_COLS:(c + 1) * MLP_COLS]), 0.0)
        y = y + _dot((h * h).astype(BF16), w2_ref[c * MLP_COLS:(c + 1) * MLP_COLS, :])
    z = DEEPNORM_ALPHA * x + (1.0 + gt_ref[0]) * y
    o_ref[0] = _layer_norm(z, lg_ref[...], lbias_ref[...])


def mlp(x, sc, sh, gt, ln_g, ln_b, w1, w2):
    B, S, D = x.shape
    tm = min(TOKEN_TILE, S)
    tok = pl.BlockSpec((1, tm, D), lambda b, i: (b, i, 0))
    per_b = pl.BlockSpec((1, 1, D), lambda b, i: (b, 0, 0))
    return pl.pallas_call(
        _mlp_kernel,
        grid=(B, S // tm),
        in_specs=[tok, per_b, per_b, per_b, _resident((1, D)), _resident((1, D)),
                  _resident(w1.shape), _resident(w2.shape)],
        out_specs=tok,
        out_shape=jax.ShapeDtypeStruct((B, S, D), F32),
        compiler_params=_cparams(2),
        name="mlp",
    )(x, sc, sh, gt, ln_g.reshape(1, D), ln_b.reshape(1, D), w1, w2)


def _rope_tables(S):
    inv = 1.0 / (ROPE_THETA ** (jnp.arange(0, NSA_DH, 2, dtype=F32) / NSA_DH))
    ang = jnp.arange(S, dtype=F32)[:, None] * inv[None, :]
    cos, sin = jnp.cos(ang), jnp.sin(ang)
    reps = LANES // NSA_DH
    return (jnp.tile(jnp.concatenate([cos, cos], axis=1), (1, reps)),
            jnp.tile(jnp.concatenate([-sin, sin], axis=1), (1, reps)))


def kernel(x, c, w_in, b_in, cmp_pe_k, cmp_pe_v, cmp_wk1, cmp_wk2, cmp_wv1, cmp_wv2, hgrn_lb_logits, hgrn_norm_g, w_branch_a, w_branch_b, w_out, w_ada, b_ada, ln1_g, ln1_b, w_mlp1, w_mlp2, ln2_g, ln2_b):
    B, S, D = x.shape
    G = NSA_GROUPS
    lb_all = jnp.cumsum(jax.nn.softmax(hgrn_lb_logits.astype(F32), axis=0), axis=0)
    lb_all = lb_all - lb_all[0:1]
    cos_t, sin_t = _rope_tables(S)
    mod = adaln_mod(c, w_ada, b_ada)
    for l in range(DEPTH):
        sh1, sc1, gt1, sh2, sc2, gt2 = [mod[l, :, None, i * D:(i + 1) * D] for i in range(6)]
        wts = _prep_in_proj_weights(w_in[l], b_in[l])
        q, kx, kv, vt, gates_t, hq, hlf, hv, hg, gma, gmb = in_proj(x, sc1, sh1, cos_t, sin_t, lb_all[l].reshape(1, -1), wts)
        pe = jnp.stack([cmp_pe_k[l].reshape(1, -1), cmp_pe_v[l].reshape(1, -1)])
        w1 = jnp.stack([cmp_wk1[l], cmp_wv1[l]]).astype(BF16)
        w2 = jnp.pad(jnp.stack([cmp_wk2[l], cmp_wv2[l]]), ((0, 0), (0, 0), (0, LANES - NSA_DH))).astype(BF16)
        cmp = nsa_compress(kv, pe, w1, w2)
        ya = nsa_attend(q, kx, vt, cmp, gates_t)
        yb = hgrn2(hq, hlf, hv, hg, hgrn_norm_g[l])
        x = merge_out(ya, yb, gma, gmb, x, gt1, ln1_g[l], ln1_b[l],
                      w_branch_a[l].astype(BF16), w_branch_b[l].astype(BF16), w_out[l].astype(BF16))
        x = mlp(x, sc2, sh2, gt2, ln2_g[l], ln2_b[l], w_mlp1[l].astype(BF16), w_mlp2[l].astype(BF16))
    return x
```

```python
import functools

import numpy as np
import jax
import jax.numpy as jnp
from jax import lax
from jax.experimental import pallas as pl
from jax.experimental.pallas import tpu as pltpu

D_MODEL = 1024
DEPTH = 2
NSA_HEADS = 8
NSA_GROUPS = 2
NSA_REP = NSA_HEADS // NSA_GROUPS
NSA_DH = 64
NSA_WIDTH = NSA_HEADS * NSA_DH
NSA_KV_WIDTH = NSA_GROUPS * NSA_DH
CMP_LEN = 32
CMP_STRIDE = 16
CMP_HIDDEN = 2 * NSA_DH
SEL_LEN = 64
SEL_TOPK = 8
FORCE_SCORE = 1.0e4
WINDOW = 512
HGRN_HEADS = 4
HGRN_DK = 128
HGRN_DV = 128
HGRN_WIDTH = HGRN_HEADS * HGRN_DK
HGRN_VWIDTH = HGRN_HEADS * HGRN_DV
MLP_HIDDEN = 4 * D_MODEL
ROPE_THETA = 10000.0
LN_EPS = 1e-5
RMS_EPS = 1e-6
DEEPNORM_ALPHA = (2 * DEPTH) ** 0.25
IN_SIZES = (NSA_WIDTH,) + (NSA_KV_WIDTH,) * 6 + (3 * NSA_HEADS,) + (HGRN_WIDTH, HGRN_WIDTH, HGRN_VWIDTH, HGRN_VWIDTH) + (D_MODEL, D_MODEL)
IN_OFFSETS = [0] + [int(v) for v in np.cumsum(IN_SIZES)]

LANES = 128
SUBLANES = 8
VMEM_LIMIT = 48 * 1024 * 1024
TOKEN_TILE = 512
Q_TILE = 128
K_CHUNK = 128
ATT_BLOCK = 2
HGRN_CHUNK = 64
HGRN_UNROLL = 4
HGRN_SUB = 8
HGRN_MAX_STEP_DECAY = 7.5
NEG_BIG = -1e30
LOG2E = 1.4426950408889634
Q_SCALE = NSA_DH ** -0.5 * LOG2E

F32 = jnp.float32
BF16 = jnp.bfloat16


def _cparams(n_grid):
    return pltpu.CompilerParams(dimension_semantics=("arbitrary",) * n_grid, vmem_limit_bytes=VMEM_LIMIT)


def _resident(shape):
    nd = len(shape)
    return pl.BlockSpec(shape, lambda *_: (0,) * nd, pipeline_mode=pl.Buffered(1))


def _dot(a, b):
    return jnp.dot(a, b, preferred_element_type=F32)


def _dot_nt(a, b):
    return lax.dot_general(a, b, (((1,), (1,)), ((), ())), preferred_element_type=F32)


def _dot_tn(a, b):
    return lax.dot_general(a, b, (((0,), (0,)), ((), ())), preferred_element_type=F32)


def _sigmoid(x):
    return 1.0 / (1.0 + jnp.exp(-x))


def _silu(x):
    return x * _sigmoid(x)


def _layer_norm(z, g, b):
    mu = jnp.mean(z, axis=-1, keepdims=True)
    zc = z - mu
    var = jnp.mean(zc * zc, axis=-1, keepdims=True)
    return zc * lax.rsqrt(var + LN_EPS) * g + b


def _adaln_kernel(c_ref, w_ref, b_ref, o_ref):
    cond = _silu(c_ref[...]).astype(BF16)
    o_ref[0] = _dot(cond, w_ref[0]) + b_ref[0]


def adaln_mod(c, w_ada, b_ada):
    L, D, N = w_ada.shape
    B = c.shape[0]
    tn = D
    return pl.pallas_call(
        _adaln_kernel,
        grid=(L, N // tn),
        in_specs=[
            pl.BlockSpec((B, D), lambda l, j: (0, 0)),
            pl.BlockSpec((1, D, tn), lambda l, j: (l, 0, j)),
            pl.BlockSpec((1, 1, tn), lambda l, j: (l, 0, j)),
        ],
        out_specs=pl.BlockSpec((1, B, tn), lambda l, j: (l, 0, j)),
        out_shape=jax.ShapeDtypeStruct((L, B, N), F32),
        compiler_params=_cparams(2),
        name="adaln_mod",
    )(c, w_ada.astype(BF16), b_ada.reshape(L, 1, N))


N_ROPE = NSA_WIDTH + 3 * NSA_KV_WIDTH
N_NSA = N_ROPE + 3 * NSA_KV_WIDTH
N_GATE = NSA_GROUPS * LANES
N_HGRN = 2 * HGRN_WIDTH + 2 * HGRN_VWIDTH
N_MERGE = 2 * D_MODEL


def _in_proj_kernel(x_ref, sc_ref, sh_ref, cos_ref, sin_ref, lb_ref,
                    wn_ref, bn_ref, wg_ref, bg_ref, wh_ref, bh_ref, wm_ref, bm_ref,
                    q_ref, kx_ref, kv_ref, vt_ref, ga_ref, hq_ref, hlf_ref, hv_ref, hg_ref, gma_ref, gmb_ref):
    u = (x_ref[0] * (1.0 + sc_ref[0]) + sh_ref[0]).astype(BF16)
    cos = cos_ref[...]
    sin = sin_ref[...]
    lane = lax.broadcasted_iota(jnp.int32, cos.shape, 1)
    first_half = (lane % NSA_DH) < (NSA_DH // 2)
    low = lane < NSA_DH

    def heads(t, upper):
        return jnp.where(low, t, upper), jnp.where(low, pltpu.roll(t, NSA_DH, 1), upper)

    n_q, n_rope = NSA_HEADS // 2, N_ROPE // LANES
    for i2 in range(0, N_NSA // LANES, 2):
        t2 = _dot(u, wn_ref[:, i2 * LANES:(i2 + 2) * LANES]) + bn_ref[:, i2 * LANES:(i2 + 2) * LANES]
        for i in (i2, i2 + 1):
            t = t2[:, (i - i2) * LANES:(i - i2 + 1) * LANES]
            if i < n_rope:
                rot = jnp.where(first_half, pltpu.roll(t, LANES - NSA_DH // 2, 1), pltpu.roll(t, NSA_DH // 2, 1))
                t = t * cos + rot * sin
            if i < n_q:
                for j, piece in enumerate(heads(t * Q_SCALE, 0.0)):
                    q_ref[0, 2 * i + j] = piece.astype(BF16)
            elif i == n_q or i == n_rope:
                for j in range(2):
                    kv_ref[0, (2 if i == n_rope else 0) + j] = t[:, j * NSA_DH:(j + 1) * NSA_DH].astype(BF16)
            elif i < n_rope:
                kind = i - n_q - 1
                for j, piece in enumerate(heads(t, 0.0)):
                    kx_ref[0, 2 * kind + j] = piece.astype(BF16)
            else:
                kind = i - n_rope - 1
                tt = t.T.astype(BF16)
                for j in range(2):
                    vt_ref[0, 2 * kind + j] = tt[j * NSA_DH:(j + 1) * NSA_DH]
    gates = _sigmoid(_dot(u, wg_ref[...]) + bg_ref[...])
    for g in range(NSA_GROUPS):
        ga_ref[0, g] = gates[:, g * LANES:(g + 1) * LANES].T[0:2 * SUBLANES]
    W = HGRN_WIDTH
    hq = _dot(u, wh_ref[:, 0:W]) + bh_ref[:, 0:W]
    hq_ref[0] = (_silu(hq) * (HGRN_DK ** -0.5)).astype(BF16)
    z = _dot(u, wh_ref[:, W:2 * W]) + bh_ref[:, W:2 * W]
    lb = lb_ref[...]
    log_sig = jnp.minimum(z, 0.0) - jnp.log1p(jnp.exp(-jnp.abs(z)))
    a = jnp.log(lb)
    bb = jnp.log1p(-lb) + log_sig
    hlf_ref[0] = jnp.maximum(a, bb) + jnp.log1p(jnp.exp(-jnp.abs(a - bb)))
    hv_ref[0] = (_dot(u, wh_ref[:, 2 * W:3 * W]) + bh_ref[:, 2 * W:3 * W]).astype(BF16)
    hg_ref[0] = _silu(_dot(u, wh_ref[:, 3 * W:4 * W]) + bh_ref[:, 3 * W:4 * W]).astype(BF16)
    for i, ref in enumerate((gma_ref, gmb_ref)):
        for j in range(2):
            c0 = i * D_MODEL + j * (D_MODEL // 2)
            c1 = c0 + D_MODEL // 2
            ref[0, :, j * (D_MODEL // 2):(j + 1) * (D_MODEL // 2)] = _sigmoid(
                _dot(u, wm_ref[:, c0:c1]) + bm_ref[:, c0:c1]).astype(BF16)


def in_proj(x, sc, sh, cos_t, sin_t, lb, wts):
    B, S, D = x.shape
    tm = min(TOKEN_TILE, S)
    assert S // SEL_LEN <= LANES - NSA_DH
    wn, bn, wg, bg, wh, bh, wm, bm = wts
    tok = lambda n: pl.BlockSpec((1, tm, n), lambda b, i: (b, i, 0))
    per_b = pl.BlockSpec((1, 1, D), lambda b, i: (b, 0, 0))
    tab = pl.BlockSpec((tm, LANES), lambda b, i: (i, 0))
    out_shape = (
        jax.ShapeDtypeStruct((B, NSA_HEADS, S, LANES), BF16),
        jax.ShapeDtypeStruct((B, 2 * NSA_GROUPS, S, LANES), BF16),
        jax.ShapeDtypeStruct((B, 2 * NSA_GROUPS, S, NSA_DH), BF16),
        jax.ShapeDtypeStruct((B, 2 * NSA_GROUPS, NSA_DH, S), BF16),
        jax.ShapeDtypeStruct((B, NSA_GROUPS, 2 * SUBLANES, S), F32),
        jax.ShapeDtypeStruct((B, S, HGRN_WIDTH), BF16),
        jax.ShapeDtypeStruct((B, S, HGRN_WIDTH), F32),
        jax.ShapeDtypeStruct((B, S, HGRN_VWIDTH), BF16),
        jax.ShapeDtypeStruct((B, S, HGRN_VWIDTH), BF16),
        jax.ShapeDtypeStruct((B, S, D), BF16),
        jax.ShapeDtypeStruct((B, S, D), BF16),
    )
    out_specs = (
        pl.BlockSpec((1, NSA_HEADS, tm, LANES), lambda b, i: (b, 0, i, 0)),
        pl.BlockSpec((1, 2 * NSA_GROUPS, tm, LANES), lambda b, i: (b, 0, i, 0)),
        pl.BlockSpec((1, 2 * NSA_GROUPS, tm, NSA_DH), lambda b, i: (b, 0, i, 0)),
        pl.BlockSpec((1, 2 * NSA_GROUPS, NSA_DH, tm), lambda b, i: (b, 0, 0, i)),
        pl.BlockSpec((1, NSA_GROUPS, 2 * SUBLANES, tm), lambda b, i: (b, 0, 0, i)),
        tok(HGRN_WIDTH), tok(HGRN_WIDTH), tok(HGRN_VWIDTH), tok(HGRN_VWIDTH), tok(D), tok(D),
    )
    return pl.pallas_call(
        _in_proj_kernel,
        grid=(B, S // tm),
        in_specs=[tok(D), per_b, per_b, tab, tab, _resident(lb.shape),
                  _resident(wn.shape), _resident(bn.shape), _resident(wg.shape), _resident(bg.shape),
                  _resident(wh.shape), _resident(bh.shape), _resident(wm.shape), _resident(bm.shape)],
        out_specs=out_specs,
        out_shape=out_shape,
        compiler_params=_cparams(2),
        name="in_proj",
    )(x, sc, sh, cos_t, sin_t, lb, wn, bn, wg, bg, wh, bh, wm, bm)


def _prep_in_proj_weights(w_in_l, b_in_l):
    o = IN_OFFSETS
    col = lambda i: (w_in_l[:, o[i]:o[i + 1]], b_in_l[o[i]:o[i + 1]])
    q_a, k_c, v_c, k_s, v_s, k_w, v_w, g_a, q_b, f_b, i_b, g_b, gm_a, gm_b = [col(i) for i in range(14)]

    def cat(parts):
        return (jnp.concatenate([p[0] for p in parts], axis=1).astype(BF16),
                jnp.concatenate([p[1] for p in parts], axis=0).reshape(1, -1).astype(F32))

    wn, bn = cat([q_a, k_c, k_s, k_w, v_c, v_s, v_w])
    per_group = 3 * NSA_REP
    gw = jnp.zeros((w_in_l.shape[0], N_GATE), w_in_l.dtype)
    gb = jnp.zeros((N_GATE,), b_in_l.dtype)
    for g in range(NSA_GROUPS):
        gw = gw.at[:, g * LANES:g * LANES + per_group].set(g_a[0][:, g * per_group:(g + 1) * per_group])
        gb = gb.at[g * LANES:g * LANES + per_group].set(g_a[1][g * per_group:(g + 1) * per_group])
    wg, bg = gw.astype(BF16), gb.reshape(1, -1).astype(F32)
    wh, bh = cat([q_b, f_b, i_b, g_b])
    wm, bm = cat([gm_a, gm_b])
    return wn, bn, wg, bg, wh, bh, wm, bm


def _compress_kernel(t_ref, pe_ref, w1_ref, w2_ref, o_ref):
    half = CMP_STRIDE * NSA_DH
    t = t_ref[0, 0]
    nrow = t.shape[0]
    a = _dot(t, w1_ref[0, 0:half, :])
    b = _dot(t, w1_ref[0, half:2 * half, :])
    pe = jnp.broadcast_to(pe_ref[0], (8, 2 * half)).astype(BF16)
    c = _dot(pe, w1_ref[0])[0:1]
    h = a + pltpu.roll(b, nrow - 1, 0) + c
    o_ref[0, 0] = _dot(_silu(h).astype(BF16), w2_ref[0]).astype(BF16)


def nsa_compress(kv, pe, w1, w2):
    B, _, S, dh = kv.shape
    nrow = S // CMP_STRIDE
    G = NSA_GROUPS
    kv_rows = kv.reshape(B, 2 * G, nrow, CMP_STRIDE * dh)
    return pl.pallas_call(
        _compress_kernel,
        grid=(B, 2, G),
        in_specs=[
            pl.BlockSpec((1, 1, nrow, CMP_STRIDE * dh), lambda b, s, g: (b, s * G + g, 0, 0)),
            pl.BlockSpec((1, 1, CMP_LEN * dh), lambda b, s, g: (s, 0, 0)),
            pl.BlockSpec((1, CMP_LEN * dh, CMP_HIDDEN), lambda b, s, g: (s, 0, 0)),
            pl.BlockSpec((1, CMP_HIDDEN, LANES), lambda b, s, g: (s, 0, 0)),
        ],
        out_specs=pl.BlockSpec((1, 1, nrow, LANES), lambda b, s, g: (b, s * G + g, 0, 0)),
        out_shape=jax.ShapeDtypeStruct((B, 2 * G, nrow, LANES), BF16),
        compiler_params=_cparams(3),
        name="nsa_compress",
    )(kv_rows, pe, w1, w2)


def _nsa_kernel(q_ref, kc_ref, vc_ref, ks_ref, kw_ref, vst_ref, vwt_ref, gt_ref, ovt_ref,
                o_ref, *, n_sel, n_tiles):
    for qs in range(n_tiles):
        pl.when(pl.program_id(2) == qs)(functools.partial(
            _nsa_tile, qs, q_ref, kc_ref, vc_ref, ks_ref, kw_ref, vst_ref, vwt_ref, gt_ref, ovt_ref,
            o_ref, n_sel))


def _round_up(x, m):
    return -(-x // m) * m


def _nsa_tile(qb, q_ref, kc_ref, vc_ref, ks_ref, kw_ref, vst_ref, vwt_ref, gt_ref, ovt_ref, o_ref, n_sel):
    R, TQ, dh = NSA_REP, Q_TILE, NSA_DH
    cols = R * TQ
    s0 = qb * TQ
    q = q_ref[0].reshape(cols, LANES)
    t_lane = s0 + (lax.broadcasted_iota(jnp.int32, (1, cols), 1) % TQ)
    kidx = lax.broadcasted_iota(jnp.int32, (K_CHUNK, cols), 0)
    tq = lax.broadcasted_iota(jnp.int32, (K_CHUNK, cols), 1) % TQ
    causal = kidx <= tq

    def scores(k_ref, qmat, c0, n, band_first=False):
        blocks = []
        for b0 in range(0, n, ATT_BLOCK):
            nb_ = min(ATT_BLOCK, n - b0)
            s = _dot_nt(k_ref[0, 0, (c0 + b0) * K_CHUNK:(c0 + b0 + nb_) * K_CHUNK, :], qmat)
            parts = [s[i * K_CHUNK:(i + 1) * K_CHUNK] for i in range(nb_)]
            if band_first and b0 == 0:
                parts[0] = jnp.where(tq < kidx, parts[0], NEG_BIG)
            if b0 + nb_ == n:
                parts[-1] = jnp.where(causal, parts[-1], NEG_BIG)
            blocks.append(jnp.concatenate(parts, axis=0) if nb_ > 1 else parts[0])
        return blocks

    def attend(blocks, vt_ref, c0, block_bias=None):
        stats, b0 = [], 0
        for s in blocks:
            nk = s.shape[0]
            if block_bias is None:
                m = jnp.max(s, axis=0, keepdims=True)
                p = jnp.exp2(s - m)
            else:
                j0 = (c0 * K_CHUNK + b0) // SEL_LEN
                subs = [s[i * SEL_LEN:(i + 1) * SEL_LEN] for i in range(nk // SEL_LEN)]
                bias = [block_bias[j0 + i:j0 + i + 1, :] for i in range(nk // SEL_LEN)]
                m = None
                for s_i, b_i in zip(subs, bias):
                    m_i = jnp.max(s_i, axis=0, keepdims=True) + b_i
                    m = m_i if m is None else jnp.maximum(m, m_i)
                shift = jnp.where(m < 0.5 * NEG_BIG, 0.0, m)
                p = jnp.concatenate([jnp.exp2(s_i + (b_i - shift)) for s_i, b_i in zip(subs, bias)], axis=0)
            vt = vt_ref[0, 0, :, c0 * K_CHUNK + b0:c0 * K_CHUNK + b0 + nk]
            stats.append((m, jnp.sum(p, axis=0, keepdims=True), _dot(vt, p.astype(BF16))))
            b0 += nk
        if len(stats) == 1:
            return stats[0][2] * (1.0 / stats[0][1])
        m = stats[0][0]
        for st in stats[1:]:
            m = jnp.maximum(m, st[0])
        l = acc = None
        for m_i, l_i, acc_i in stats:
            w = jnp.exp2(m_i - m)
            l = w * l_i if l is None else l + w * l_i
            acc = w * acc_i if acc is None else acc + w * acc_i
        return acc * (1.0 / l)

    n_win = WINDOW // K_CHUNK
    c0_w = max(qb - n_win, 0)
    s_w = scores(kw_ref, q, c0_w, min(qb, n_win) + 1, band_first=qb >= n_win)
    s_s = scores(ks_ref, q, 0, qb + 1)

    ncb = min(kc_ref.shape[2], _round_up((s0 + TQ - CMP_LEN) // CMP_STRIDE + 1, 2 * SUBLANES))
    sc = _dot_nt(kc_ref[0, 0, 0:ncb, :], q)
    n_sub = lax.broadcasted_iota(jnp.int32, (ncb, cols), 0)
    mask_c = n_sub * CMP_STRIDE + (CMP_LEN - 1) <= t_lane
    sc = jnp.where(mask_c, sc, NEG_BIG)
    mc = jnp.max(sc, axis=0, keepdims=True)
    ec = jnp.where(mask_c, jnp.exp2(sc - mc), 0.0)
    pc = ec * (1.0 / jnp.maximum(jnp.sum(ec, axis=0, keepdims=True), 1e-30))
    o_c = _dot_tn(vc_ref[0, 0, 0:ncb, :], pc.astype(BF16))[0:dh]

    nb_live = (s0 + TQ) // SEL_LEN
    nb = min(ovt_ref.shape[0], _round_up(nb_live, 2 * SUBLANES))
    psum = pc[:, 0:TQ]
    for r in range(1, R):
        psum = psum + pc[:, r * TQ:(r + 1) * TQ]
    p_hi = psum.astype(BF16)
    p_lo = (psum - p_hi.astype(F32)).astype(BF16)
    ovt = ovt_ref[0:nb, 0:ncb]
    imp = _dot(ovt, p_hi) + _dot(ovt, p_lo)
    jb = lax.broadcasted_iota(jnp.int32, (nb, TQ), 0)
    tb = (s0 + lax.broadcasted_iota(jnp.int32, (nb, TQ), 1)) // SEL_LEN
    valid = jb <= tb
    forced = jnp.where(valid, jnp.where(jb == 0, 1.0, jnp.where(jb >= tb - 1, 1.0, 0.0)), 0.0)
    score = jnp.where(forced > 0.5, FORCE_SCORE, jnp.where(valid, imp, -1.0))
    rank = jnp.zeros((nb, TQ), F32)
    for i in range(nb_live):
        si = score[i:i + 1, :]
        tie_first = jnp.where(jb > i, 1.0, 0.0)
        rank = rank + jnp.where(si > score, 1.0, jnp.where(si == score, tie_first, 0.0))
    sel_bias = jnp.where(rank < n_sel, 0.0, NEG_BIG)
    sel_bias = jnp.concatenate([sel_bias] * R, axis=1)

    o_w = attend(s_w, vwt_ref, c0_w)
    o_s = attend(s_s, vst_ref, 0, block_bias=sel_bias)

    gate = gt_ref[0, 0]
    pieces = []
    for r in range(R):
        sl = slice(r * TQ, (r + 1) * TQ)
        o_r = (gate[3 * r:3 * r + 1, :] * o_c[:, sl] + gate[3 * r + 1:3 * r + 2, :] * o_s[:, sl]
               + gate[3 * r + 2:3 * r + 3, :] * o_w[:, sl])
        pieces.append(o_r.T)
    o_ref[0] = jnp.concatenate(pieces, axis=1).astype(BF16)


def nsa_attend(q, kx, vt, cmp, gates_t):
    B, H, S, _ = q.shape
    G, R, dh = NSA_GROUPS, NSA_REP, NSA_DH
    ncb = S // CMP_STRIDE
    nb = S // SEL_LEN
    assert (S % Q_TILE == 0 and Q_TILE == K_CHUNK and WINDOW % K_CHUNK == 0 and K_CHUNK % SEL_LEN == 0
            and 3 * R <= 2 * SUBLANES)
    cstart = np.arange(ncb) * CMP_STRIDE
    sstart = np.arange(nb) * SEL_LEN
    overlap = ((cstart[:, None] < sstart[None, :] + SEL_LEN) & (cstart[:, None] + CMP_LEN > sstart[None, :]))
    ovt = jnp.asarray(overlap.T, BF16)
    kx_spec = lambda idx: pl.BlockSpec((1, 1, S, LANES), lambda b, g, i, idx=idx: (b, idx * G + g, 0, 0))
    vt_spec = lambda idx: pl.BlockSpec((1, 1, dh, S), lambda b, g, i, idx=idx: (b, idx * G + g, 0, 0))
    cmp_spec = lambda idx: pl.BlockSpec((1, 1, ncb, LANES), lambda b, g, i, idx=idx: (b, idx * G + g, 0, 0))
    return pl.pallas_call(
        functools.partial(_nsa_kernel, n_sel=min(SEL_TOPK, nb), n_tiles=S // Q_TILE),
        grid=(B, G, S // Q_TILE),
        in_specs=[
            pl.BlockSpec((1, R, Q_TILE, LANES), lambda b, g, i: (b, g, i, 0)),
            cmp_spec(0), cmp_spec(1),
            kx_spec(0), kx_spec(1),
            vt_spec(0), vt_spec(1),
            pl.BlockSpec((1, 1, 2 * SUBLANES, Q_TILE), lambda b, g, i: (b, g, 0, i)),
            pl.BlockSpec(ovt.shape, lambda b, g, i: (0, 0)),
        ],
        out_specs=pl.BlockSpec((1, Q_TILE, R * dh), lambda b, g, i: (b, i, g)),
        out_shape=jax.ShapeDtypeStruct((B, S, H * dh), BF16),
        compiler_params=_cparams(3),
        name="nsa_attend",
    )(q, cmp, cmp, kx, kx, vt, vt, gates_t, ovt)


def _hgrn_kernel(q_ref, lf_ref, v_ref, g_ref, ng_ref, tri_ref, o_ref, st_ref):
    C, SB = HGRN_CHUNK, HGRN_SUB
    NBK = C // SB
    S = q_ref.shape[1]
    st_ref[...] = jnp.zeros(st_ref.shape, F32)
    row = lax.broadcasted_iota(jnp.int32, (C, HGRN_DK), 0)
    sub_row = lax.broadcasted_iota(jnp.int32, (SB, 1), 0)
    cr = lax.broadcasted_iota(jnp.int32, (C, C), 0)
    cc = lax.broadcasted_iota(jnp.int32, (C, C), 1)
    diag_mask = (cr // SB == cc // SB) & (cc <= cr)

    def head_chunk(h, r0, bounded_decay):
        hs = slice(h * HGRN_DK, (h + 1) * HGRN_DK)
        q = q_ref[0, pl.ds(r0, C), hs].astype(F32)
        lf = lf_ref[0, pl.ds(r0, C), hs]
        v_bf = v_ref[0, pl.ds(r0, C), hs]
        v = v_bf.astype(F32)
        kh = 1.0 - jnp.exp(lf)
        tri = tri_ref[...]
        lf0 = lf.astype(BF16)
        lf1 = (lf - lf0.astype(F32)).astype(BF16)
        lf2 = (lf - lf0.astype(F32) - lf1.astype(F32)).astype(BF16)
        b = _dot(tri, lf0) + _dot(tri, lf1) + _dot(tri, lf2)
        yield
        b_last = b[C - 1:C, :]
        st = st_ref[h]
        o = _dot_nt((q * jnp.exp(b)).astype(BF16), st.astype(BF16))
        b_end = jnp.concatenate(
            [jnp.broadcast_to(b[(j + 1) * SB - 1:(j + 1) * SB, :], (SB, HGRN_DK)) for j in range(NBK)], axis=0)
        k_end = kh * jnp.exp(b_end - b)
        q_parts, k_parts = [], []
        for j in range(NBK - 1):
            lo = (j + 1) * SB
            qj = q[lo:] * jnp.exp(b[lo:] - b[lo - 1:lo, :])
            q_parts.append(jnp.concatenate([jnp.zeros((lo, HGRN_DK), F32), qj], axis=0))
            k_parts.append(jnp.where((row >= j * SB) & (row < lo), k_end, 0.0))
        q_cat = jnp.concatenate(q_parts, axis=1).astype(BF16)
        k_cat = jnp.concatenate(k_parts, axis=1).astype(BF16)
        a_off = _dot_nt(q_cat, k_cat)
        k_last = (kh * jnp.exp(b_last - b)).astype(BF16)
        st_ref[h] = st * jnp.exp(b_last) + _dot_tn(v_bf, k_last)
        if bounded_decay:
            b_start = jnp.concatenate([jnp.zeros((SB, HGRN_DK), F32), b_end[:C - SB]], axis=0)
            a_dg = _dot_nt((q * jnp.exp(b - b_start)).astype(BF16), (kh * jnp.exp(b_start - b)).astype(BF16))
            yield
            o = o + _dot(jnp.where(diag_mask, a_dg, a_off).astype(BF16), v_bf)
            yield
        else:
            yield
            o = o + _dot(a_off.astype(BF16), v_bf)
            diag = []
            for j in range(NBK):
                sl = slice(j * SB, (j + 1) * SB)
                qj, bj, kj, vj = q[sl], b[sl], kh[sl], v[sl]
                oj = jnp.zeros((SB, HGRN_DV), F32)
                for s in range(SB):
                    w = jnp.exp(jnp.minimum(bj - bj[s:s + 1, :], 0.0))
                    a = jnp.sum(qj * kj[s:s + 1, :] * w, axis=-1, keepdims=True)
                    a = jnp.where(sub_row >= s, a, 0.0)
                    oj = oj + a * vj[s:s + 1, :]
                diag.append(oj)
            o = o + jnp.concatenate(diag, axis=0)
            yield
        o = o * lax.rsqrt(jnp.mean(o * o, axis=-1, keepdims=True) + RMS_EPS) * ng_ref[...]
        o_ref[0, pl.ds(r0, C), hs] = (o * g_ref[0, pl.ds(r0, C), hs].astype(F32)).astype(BF16)

    def chunk(bounded_decay, ci, carry):
        r0 = pl.multiple_of(ci * (C * HGRN_UNROLL), C * HGRN_UNROLL)
        heads = [head_chunk(h, r0 + u * C, bounded_decay) for u in range(HGRN_UNROLL) for h in range(HGRN_HEADS)]
        for _ in range(3):
            for gen in heads:
                next(gen)
        for gen in heads:
            next(gen, None)
        return carry

    bounded = jnp.min(lf_ref[0]) >= -HGRN_MAX_STEP_DECAY

    @pl.when(bounded)
    def _():
        lax.fori_loop(0, S // (C * HGRN_UNROLL), functools.partial(chunk, True), 0)

    @pl.when(jnp.logical_not(bounded))
    def _():
        lax.fori_loop(0, S // (C * HGRN_UNROLL), functools.partial(chunk, False), 0)


def hgrn2(hq, hlf, hv, hg, norm_g):
    B, S, _ = hq.shape
    C = HGRN_CHUNK
    assert S % (C * HGRN_UNROLL) == 0
    tri = jnp.asarray(np.tril(np.ones((C, C))), BF16)
    spec = lambda n: pl.BlockSpec((1, S, n), lambda b: (b, 0, 0))
    return pl.pallas_call(
        _hgrn_kernel,
        grid=(B,),
        in_specs=[spec(HGRN_WIDTH), spec(HGRN_WIDTH), spec(HGRN_VWIDTH), spec(HGRN_VWIDTH),
                  pl.BlockSpec((1, HGRN_DV), lambda b: (0, 0)),
                  pl.BlockSpec((C, C), lambda b: (0, 0))],
        out_specs=spec(HGRN_VWIDTH),
        out_shape=jax.ShapeDtypeStruct((B, S, HGRN_VWIDTH), BF16),
        scratch_shapes=[pltpu.VMEM((HGRN_HEADS, HGRN_DV, HGRN_DK), F32)],
        compiler_params=_cparams(1),
        name="hgrn2",
    )(hq, hlf, hv, hg, norm_g.reshape(1, HGRN_DV).astype(F32), tri)


def _merge_kernel(ya_ref, yb_ref, gma_ref, gmb_ref, x_ref, gt_ref, lg_ref, lbias_ref,
                  wa_ref, wb_ref, wo_ref, o_ref):
    pa = _dot(ya_ref[0], wa_ref[...])
    pb = _dot(yb_ref[0], wb_ref[...])
    merged = gma_ref[0].astype(F32) * pa + gmb_ref[0].astype(F32) * pb
    y = _dot(merged.astype(BF16), wo_ref[...])
    z = DEEPNORM_ALPHA * x_ref[0] + (1.0 + gt_ref[0]) * y
    o_ref[0] = _layer_norm(z, lg_ref[...], lbias_ref[...])


def merge_out(ya, yb, gma, gmb, x, gt, ln_g, ln_b, wa, wb, wo):
    B, S, D = x.shape
    tm = min(TOKEN_TILE, S)
    tok = lambda n: pl.BlockSpec((1, tm, n), lambda b, i: (b, i, 0))
    return pl.pallas_call(
        _merge_kernel,
        grid=(B, S // tm),
        in_specs=[tok(NSA_WIDTH), tok(HGRN_VWIDTH), tok(D), tok(D), tok(D),
                  pl.BlockSpec((1, 1, D), lambda b, i: (b, 0, 0)),
                  _resident((1, D)), _resident((1, D)),
                  _resident(wa.shape), _resident(wb.shape), _resident(wo.shape)],
        out_specs=tok(D),
        out_shape=jax.ShapeDtypeStruct((B, S, D), F32),
        compiler_params=_cparams(2),
        name="merge_out",
    )(ya, yb, gma, gmb, x, gt, ln_g.reshape(1, D), ln_b.reshape(1, D), wa, wb, wo)


MLP_COLS = 1024


def _mlp_kernel(x_ref, sc_ref, sh_ref, gt_ref, lg_ref, lbias_ref, w1_ref, w2_ref, o_ref):
    x = x_ref[0]
    u = (x * (1.0 + sc_ref[0]) + sh_ref[0]).astype(BF16)
    y = jnp.zeros(x.shape, F32)
    for c in range(MLP_HIDDEN // MLP_COLS):
        h = jnp.maximum(_dot(u, w1_ref[:, c * MLP_COLS:(c + 1) * MLP_COLS]), 0.0)
        y = y + _dot((h * h).astype(BF16), w2_ref[c * MLP_COLS:(c + 1) * MLP_COLS, :])
    z = DEEPNORM_ALPHA * x + (1.0 + gt_ref[0]) * y
    o_ref[0] = _layer_norm(z, lg_ref[...], lbias_ref[...])


def mlp(x, sc, sh, gt, ln_g, ln_b, w1, w2):
    B, S, D = x.shape
    tm = min(TOKEN_TILE, S)
    tok = pl.BlockSpec((1, tm, D), lambda b, i: (b, i, 0))
    per_b = pl.BlockSpec((1, 1, D), lambda b, i: (b, 0, 0))
    return pl.pallas_call(
        _mlp_kernel,
        grid=(B, S // tm),
        in_specs=[tok, per_b, per_b, per_b, _resident((1, D)), _resident((1, D)),
                  _resident(w1.shape), _resident(w2.shape)],
        out_specs=tok,
        out_shape=jax.ShapeDtypeStruct((B, S, D), F32),
        compiler_params=_cparams(2),
        name="mlp",
    )(x, sc, sh, gt, ln_g.reshape(1, D), ln_b.reshape(1, D), w1, w2)


def _rope_tables(S):
    inv = 1.0 / (ROPE_THETA ** (jnp.arange(0, NSA_DH, 2, dtype=F32) / NSA_DH))
    ang = jnp.arange(S, dtype=F32)[:, None] * inv[None, :]
    cos, sin = jnp.cos(ang), jnp.sin(ang)
    reps = LANES // NSA_DH
    return (jnp.tile(jnp.concatenate([cos, cos], axis=1), (1, reps)),
            jnp.tile(jnp.concatenate([-sin, sin], axis=1), (1, reps)))


def kernel(x, c, w_in, b_in, cmp_pe_k, cmp_pe_v, cmp_wk1, cmp_wk2, cmp_wv1, cmp_wv2, hgrn_lb_logits, hgrn_norm_g, w_branch_a, w_branch_b, w_out, w_ada, b_ada, ln1_g, ln1_b, w_mlp1, w_mlp2, ln2_g, ln2_b):
    B, S, D = x.shape
    G = NSA_GROUPS
    lb_all = jnp.cumsum(jax.nn.softmax(hgrn_lb_logits.astype(F32), axis=0), axis=0)
    lb_all = lb_all - lb_all[0:1]
    cos_t, sin_t = _rope_tables(S)
    mod = adaln_mod(c, w_ada, b_ada)
    for l in range(DEPTH):
        sh1, sc1, gt1, sh2, sc2, gt2 = [mod[l, :, None, i * D:(i + 1) * D] for i in range(6)]
        wts = _prep_in_proj_weights(w_in[l], b_in[l])
        q, kx, kv, vt, gates_t, hq, hlf, hv, hg, gma, gmb = in_proj(x, sc1, sh1, cos_t, sin_t, lb_all[l].reshape(1, -1), wts)
        pe = jnp.stack([cmp_pe_k[l].reshape(1, -1), cmp_pe_v[l].reshape(1, -1)])
        w1 = jnp.stack([cmp_wk1[l], cmp_wv1[l]]).astype(BF16)
        w2 = jnp.pad(jnp.stack([cmp_wk2[l], cmp_wv2[l]]), ((0, 0), (0, 0), (0, LANES - NSA_DH))).astype(BF16)
        cmp = nsa_compress(kv, pe, w1, w2)
        ya = nsa_attend(q, kx, vt, cmp, gates_t)
        yb = hgrn2(hq, hlf, hv, hg, hgrn_norm_g[l])
        x = merge_out(ya, yb, gma, gmb, x, gt1, ln1_g[l], ln1_b[l],
                      w_branch_a[l].astype(BF16), w_branch_b[l].astype(BF16), w_out[l].astype(BF16))
        x = mlp(x, sc2, sh2, gt2, ln2_g[l], ln2_b[l], w_mlp1[l].astype(BF16), w_mlp2[l].astype(BF16))
    return x
```

```python
import functools

import numpy as np
import jax
import jax.numpy as jnp
from jax import lax
from jax.experimental import pallas as pl
from jax.experimental.pallas import tpu as pltpu

D_MODEL = 1024
DEPTH = 2
NSA_HEADS = 8
NSA_GROUPS = 2
NSA_REP = NSA_HEADS // NSA_GROUPS
NSA_DH = 64
NSA_WIDTH = NSA_HEADS * NSA_DH
NSA_KV_WIDTH = NSA_GROUPS * NSA_DH
CMP_LEN = 32
CMP_STRIDE = 16
CMP_HIDDEN = 2 * NSA_DH
SEL_LEN = 64
SEL_TOPK = 8
FORCE_SCORE = 1.0e4
WINDOW = 512
HGRN_HEADS = 4
HGRN_DK = 128
HGRN_DV = 128
HGRN_WIDTH = HGRN_HEADS * HGRN_DK
HGRN_VWIDTH = HGRN_HEADS * HGRN_DV
MLP_HIDDEN = 4 * D_MODEL
ROPE_THETA = 10000.0
LN_EPS = 1e-5
RMS_EPS = 1e-6
DEEPNORM_ALPHA = (2 * DEPTH) ** 0.25
IN_SIZES = (NSA_WIDTH,) + (NSA_KV_WIDTH,) * 6 + (3 * NSA_HEADS,) + (HGRN_WIDTH, HGRN_WIDTH, HGRN_VWIDTH, HGRN_VWIDTH) + (D_MODEL, D_MODEL)
IN_OFFSETS = [0] + [int(v) for v in np.cumsum(IN_SIZES)]

LANES = 128
SUBLANES = 8
VMEM_LIMIT = 48 * 1024 * 1024
TOKEN_TILE = 512
Q_TILE = 128
K_CHUNK = 128
ATT_BLOCK = 4
ATT_LOOKAHEAD = 2
HGRN_CHUNK = 64
HGRN_UNROLL = 4
HGRN_SUB = 8
HGRN_MAX_STEP_DECAY = 7.5
NEG_BIG = -1e30
LOG2E = 1.4426950408889634
Q_SCALE = NSA_DH ** -0.5 * LOG2E
SCORE_BOUND = 96.0
VT_ROWS = NSA_DH + 16

F32 = jnp.float32
BF16 = jnp.bfloat16


def _cparams(n_grid):
    return pltpu.CompilerParams(dimension_semantics=("arbitrary",) * n_grid, vmem_limit_bytes=VMEM_LIMIT)


def _resident(shape):
    nd = len(shape)
    return pl.BlockSpec(shape, lambda *_: (0,) * nd, pipeline_mode=pl.Buffered(1))


def _dot(a, b):
    return jnp.dot(a, b, preferred_element_type=F32)


def _dot_nt(a, b):
    return lax.dot_general(a, b, (((1,), (1,)), ((), ())), preferred_element_type=F32)


def _dot_tn(a, b):
    return lax.dot_general(a, b, (((0,), (0,)), ((), ())), preferred_element_type=F32)


def _sigmoid(x):
    return 1.0 / (1.0 + jnp.exp(-x))


def _silu(x):
    return x * _sigmoid(x)


def _layer_norm(z, g, b):
    mu = jnp.mean(z, axis=-1, keepdims=True)
    zc = z - mu
    var = jnp.mean(zc * zc, axis=-1, keepdims=True)
    return zc * lax.rsqrt(var + LN_EPS) * g + b


def _adaln_kernel(c_ref, w_ref, b_ref, o_ref):
    cond = _silu(c_ref[...]).astype(BF16)
    o_ref[0] = _dot(cond, w_ref[0]) + b_ref[0]


def adaln_mod(c, w_ada, b_ada):
    L, D, N = w_ada.shape
    B = c.shape[0]
    tn = D
    return pl.pallas_call(
        _adaln_kernel,
        grid=(L, N // tn),
        in_specs=[
            pl.BlockSpec((B, D), lambda l, j: (0, 0)),
            pl.BlockSpec((1, D, tn), lambda l, j: (l, 0, j)),
            pl.BlockSpec((1, 1, tn), lambda l, j: (l, 0, j)),
        ],
        out_specs=pl.BlockSpec((1, B, tn), lambda l, j: (l, 0, j)),
        out_shape=jax.ShapeDtypeStruct((L, B, N), F32),
        compiler_params=_cparams(2),
        name="adaln_mod",
    )(c, w_ada.astype(BF16), b_ada.reshape(L, 1, N))


N_ROPE = NSA_WIDTH + 3 * NSA_KV_WIDTH
N_NSA = N_ROPE + 3 * NSA_KV_WIDTH
N_GATE = NSA_GROUPS * LANES
N_HGRN = 2 * HGRN_WIDTH + 2 * HGRN_VWIDTH
N_MERGE = 2 * D_MODEL


def _in_proj_kernel(x_ref, sc_ref, sh_ref, cos_ref, sin_ref, lb_ref,
                    wn_ref, bn_ref, wg_ref, bg_ref, wh_ref, bh_ref, wm_ref, bm_ref,
                    q_ref, kx_ref, kv_ref, vt_ref, ga_ref, nrm_ref, hq_ref, hlf_ref, hv_ref, hg_ref, gma_ref, gmb_ref):
    u = (x_ref[0] * (1.0 + sc_ref[0]) + sh_ref[0]).astype(BF16)
    cos = cos_ref[...]
    sin = sin_ref[...]
    lane = lax.broadcasted_iota(jnp.int32, cos.shape, 1)
    first_half = (lane % NSA_DH) < (NSA_DH // 2)
    low = lane < NSA_DH

    def heads(t, upper):
        return jnp.where(low, t, upper), jnp.where(low, pltpu.roll(t, NSA_DH, 1), upper)

    n_q, n_rope = NSA_HEADS // 2, N_ROPE // LANES
    tm = u.shape[0]
    ones_rows = jnp.where(lax.broadcasted_iota(jnp.int32, (VT_ROWS - NSA_DH, tm), 0) == 0, 1.0, 0.0).astype(BF16)

    def max_sq_norm(t, acc):
        n = jnp.max(jnp.sum(t * t, axis=1, keepdims=True), axis=0, keepdims=True)
        return n if acc is None else jnp.maximum(acc, n)

    q_sq = k_sq = None
    for i2 in range(0, N_NSA // LANES, 2):
        t2 = _dot(u, wn_ref[:, i2 * LANES:(i2 + 2) * LANES]) + bn_ref[:, i2 * LANES:(i2 + 2) * LANES]
        for i in (i2, i2 + 1):
            t = t2[:, (i - i2) * LANES:(i - i2 + 1) * LANES]
            if i < n_rope:
                rot = jnp.where(first_half, pltpu.roll(t, LANES - NSA_DH // 2, 1), pltpu.roll(t, NSA_DH // 2, 1))
                t = t * cos + rot * sin
            if i < n_q:
                t = t * Q_SCALE
                q_sq = max_sq_norm(t, q_sq)
                for j, piece in enumerate(heads(t, 0.0)):
                    q_ref[0, 2 * i + j] = piece.astype(BF16)
            elif i == n_q or i == n_rope:
                for j in range(2):
                    kv_ref[0, (2 if i == n_rope else 0) + j] = t[:, j * NSA_DH:(j + 1) * NSA_DH].astype(BF16)
            elif i < n_rope:
                kind = i - n_q - 1
                k_sq = max_sq_norm(t, k_sq)
                for j, piece in enumerate(heads(t, 0.0)):
                    kx_ref[0, 2 * kind + j] = piece.astype(BF16)
            else:
                kind = i - n_rope - 1
                tt = t.T.astype(BF16)
                for j in range(2):
                    vt_ref[0, 2 * kind + j] = jnp.concatenate([tt[j * NSA_DH:(j + 1) * NSA_DH], ones_rows], axis=0)
    srow = lax.broadcasted_iota(jnp.int32, (SUBLANES, LANES), 0)
    nrm_ref[0, 0] = jnp.where(srow == 0, q_sq, jnp.where(srow == 1, k_sq, 0.0))
    gates = _sigmoid(_dot(u, wg_ref[...]) + bg_ref[...])
    for g in range(NSA_GROUPS):
        ga_ref[0, g] = gates[:, g * LANES:(g + 1) * LANES].T[0:2 * SUBLANES]
    W = HGRN_WIDTH
    hq = _dot(u, wh_ref[:, 0:W]) + bh_ref[:, 0:W]
    hq_ref[0] = (_silu(hq) * (HGRN_DK ** -0.5)).astype(BF16)
    z = _dot(u, wh_ref[:, W:2 * W]) + bh_ref[:, W:2 * W]
    lb = lb_ref[...]
    log_sig = jnp.minimum(z, 0.0) - jnp.log1p(jnp.exp(-jnp.abs(z)))
    a = jnp.log(lb)
    bb = jnp.log1p(-lb) + log_sig
    hlf_ref[0] = jnp.maximum(a, bb) + jnp.log1p(jnp.exp(-jnp.abs(a - bb)))
    hv_ref[0] = (_dot(u, wh_ref[:, 2 * W:3 * W]) + bh_ref[:, 2 * W:3 * W]).astype(BF16)
    hg_ref[0] = _silu(_dot(u, wh_ref[:, 3 * W:4 * W]) + bh_ref[:, 3 * W:4 * W]).astype(BF16)
    for i, ref in enumerate((gma_ref, gmb_ref)):
        for j in range(2):
            c0 = i * D_MODEL + j * (D_MODEL // 2)
            c1 = c0 + D_MODEL // 2
            ref[0, :, j * (D_MODEL // 2):(j + 1) * (D_MODEL // 2)] = _sigmoid(
                _dot(u, wm_ref[:, c0:c1]) + bm_ref[:, c0:c1]).astype(BF16)


def in_proj(x, sc, sh, cos_t, sin_t, lb, wts):
    B, S, D = x.shape
    tm = min(TOKEN_TILE, S)
    assert S // SEL_LEN <= LANES - NSA_DH
    wn, bn, wg, bg, wh, bh, wm, bm = wts
    tok = lambda n: pl.BlockSpec((1, tm, n), lambda b, i: (b, i, 0))
    per_b = pl.BlockSpec((1, 1, D), lambda b, i: (b, 0, 0))
    tab = pl.BlockSpec((tm, LANES), lambda b, i: (i, 0))
    out_shape = (
        jax.ShapeDtypeStruct((B, NSA_HEADS, S, LANES), BF16),
        jax.ShapeDtypeStruct((B, 2 * NSA_GROUPS, S, LANES), BF16),
        jax.ShapeDtypeStruct((B, 2 * NSA_GROUPS, S, NSA_DH), BF16),
        jax.ShapeDtypeStruct((B, 2 * NSA_GROUPS, VT_ROWS, S), BF16),
        jax.ShapeDtypeStruct((B, NSA_GROUPS, 2 * SUBLANES, S), F32),
        jax.ShapeDtypeStruct((B, S // tm, SUBLANES, LANES), F32),
        jax.ShapeDtypeStruct((B, S, HGRN_WIDTH), BF16),
        jax.ShapeDtypeStruct((B, S, HGRN_WIDTH), F32),
        jax.ShapeDtypeStruct((B, S, HGRN_VWIDTH), BF16),
        jax.ShapeDtypeStruct((B, S, HGRN_VWIDTH), BF16),
        jax.ShapeDtypeStruct((B, S, D), BF16),
        jax.ShapeDtypeStruct((B, S, D), BF16),
    )
    out_specs = (
        pl.BlockSpec((1, NSA_HEADS, tm, LANES), lambda b, i: (b, 0, i, 0)),
        pl.BlockSpec((1, 2 * NSA_GROUPS, tm, LANES), lambda b, i: (b, 0, i, 0)),
        pl.BlockSpec((1, 2 * NSA_GROUPS, tm, NSA_DH), lambda b, i: (b, 0, i, 0)),
        pl.BlockSpec((1, 2 * NSA_GROUPS, VT_ROWS, tm), lambda b, i: (b, 0, 0, i)),
        pl.BlockSpec((1, NSA_GROUPS, 2 * SUBLANES, tm), lambda b, i: (b, 0, 0, i)),
        pl.BlockSpec((1, 1, SUBLANES, LANES), lambda b, i: (b, i, 0, 0)),
        tok(HGRN_WIDTH), tok(HGRN_WIDTH), tok(HGRN_VWIDTH), tok(HGRN_VWIDTH), tok(D), tok(D),
    )
    return pl.pallas_call(
        _in_proj_kernel,
        grid=(B, S // tm),
        in_specs=[tok(D), per_b, per_b, tab, tab, _resident(lb.shape),
                  _resident(wn.shape), _resident(bn.shape), _resident(wg.shape), _resident(bg.shape),
                  _resident(wh.shape), _resident(bh.shape), _resident(wm.shape), _resident(bm.shape)],
        out_specs=out_specs,
        out_shape=out_shape,
        compiler_params=_cparams(2),
        name="in_proj",
    )(x, sc, sh, cos_t, sin_t, lb, wn, bn, wg, bg, wh, bh, wm, bm)


def _prep_in_proj_weights(w_in_l, b_in_l):
    o = IN_OFFSETS
    col = lambda i: (w_in_l[:, o[i]:o[i + 1]], b_in_l[o[i]:o[i + 1]])
    q_a, k_c, v_c, k_s, v_s, k_w, v_w, g_a, q_b, f_b, i_b, g_b, gm_a, gm_b = [col(i) for i in range(14)]

    def cat(parts):
        return (jnp.concatenate([p[0] for p in parts], axis=1).astype(BF16),
                jnp.concatenate([p[1] for p in parts], axis=0).reshape(1, -1).astype(F32))

    wn, bn = cat([q_a, k_c, k_s, k_w, v_c, v_s, v_w])
    per_group = 3 * NSA_REP
    gw = jnp.zeros((w_in_l.shape[0], N_GATE), w_in_l.dtype)
    gb = jnp.zeros((N_GATE,), b_in_l.dtype)
    for g in range(NSA_GROUPS):
        gw = gw.at[:, g * LANES:g * LANES + per_group].set(g_a[0][:, g * per_group:(g + 1) * per_group])
        gb = gb.at[g * LANES:g * LANES + per_group].set(g_a[1][g * per_group:(g + 1) * per_group])
    wg, bg = gw.astype(BF16), gb.reshape(1, -1).astype(F32)
    wh, bh = cat([q_b, f_b, i_b, g_b])
    wm, bm = cat([gm_a, gm_b])
    return wn, bn, wg, bg, wh, bh, wm, bm


def _compress_kernel(t_ref, pe_ref, w1_ref, w2_ref, o_ref):
    half = CMP_STRIDE * NSA_DH
    t = t_ref[0, 0]
    nrow = t.shape[0]
    a = _dot(t, w1_ref[0, 0:half, :])
    b = _dot(t, w1_ref[0, half:2 * half, :])
    pe = jnp.broadcast_to(pe_ref[0], (8, 2 * half)).astype(BF16)
    c = _dot(pe, w1_ref[0])[0:1]
    h = a + pltpu.roll(b, nrow - 1, 0) + c
    o_ref[0, 0] = _dot(_silu(h).astype(BF16), w2_ref[0]).astype(BF16)


def nsa_compress(kv, pe, w1, w2):
    B, _, S, dh = kv.shape
    nrow = S // CMP_STRIDE
    G = NSA_GROUPS
    kv_rows = kv.reshape(B, 2 * G, nrow, CMP_STRIDE * dh)
    return pl.pallas_call(
        _compress_kernel,
        grid=(B, 2, G),
        in_specs=[
            pl.BlockSpec((1, 1, nrow, CMP_STRIDE * dh), lambda b, s, g: (b, s * G + g, 0, 0)),
            pl.BlockSpec((1, 1, CMP_LEN * dh), lambda b, s, g: (s, 0, 0)),
            pl.BlockSpec((1, CMP_LEN * dh, CMP_HIDDEN), lambda b, s, g: (s, 0, 0)),
            pl.BlockSpec((1, CMP_HIDDEN, LANES), lambda b, s, g: (s, 0, 0)),
        ],
        out_specs=pl.BlockSpec((1, 1, nrow, LANES), lambda b, s, g: (b, s * G + g, 0, 0)),
        out_shape=jax.ShapeDtypeStruct((B, 2 * G, nrow, LANES), BF16),
        compiler_params=_cparams(3),
        name="nsa_compress",
    )(kv_rows, pe, w1, w2)


def _nsa_kernel(q_ref, kc_ref, vc_ref, ks_ref, kw_ref, vst_ref, vwt_ref, gt_ref, ovt_ref,
                o_ref, *, n_sel, n_tiles, bounded_scores):
    for qs in range(n_tiles):
        pl.when(pl.program_id(0) == qs)(functools.partial(
            _nsa_tile, qs, q_ref, kc_ref, vc_ref, ks_ref, kw_ref, vst_ref, vwt_ref, gt_ref, ovt_ref,
            o_ref, n_sel, bounded_scores))


def _round_up(x, m):
    return -(-x // m) * m


def _nsa_tile(qb, q_ref, kc_ref, vc_ref, ks_ref, kw_ref, vst_ref, vwt_ref, gt_ref, ovt_ref, o_ref, n_sel,
              bounded_scores):
    R, TQ, dh = NSA_REP, Q_TILE, NSA_DH
    cols = R * TQ
    s0 = qb * TQ
    q = q_ref[0].reshape(cols, LANES)
    t_lane = s0 + (lax.broadcasted_iota(jnp.int32, (1, cols), 1) % TQ)
    kidx = lax.broadcasted_iota(jnp.int32, (K_CHUNK, cols), 0)
    tq = lax.broadcasted_iota(jnp.int32, (K_CHUNK, cols), 1) % TQ
    causal = kidx <= tq

    def branch_blocks(k_ref, vt_ref, c0, n, band_first):
        return [(k_ref, vt_ref, c0 + b0, min(ATT_BLOCK, n - b0), band_first and b0 == 0, b0 + ATT_BLOCK >= n)
                for b0 in range(0, n, ATT_BLOCK)]

    def score_block(blk):
        k_ref, _, c, n, band, diag = blk
        s = _dot_nt(k_ref[0, 0, c * K_CHUNK:(c + n) * K_CHUNK, :], q)
        parts = [s[i * K_CHUNK:(i + 1) * K_CHUNK] for i in range(n)]
        if band:
            parts[0] = jnp.where(tq < kidx, parts[0], NEG_BIG)
        if diag:
            parts[-1] = jnp.where(causal, parts[-1], NEG_BIG)
        return jnp.concatenate(parts, axis=0) if n > 1 else parts[0]

    def finish_block(s, blk, block_bias=None):
        _, vt_ref, c, n, _, _ = blk
        nk = n * K_CHUNK
        m = None
        if block_bias is not None:
            j0 = c * K_CHUNK // SEL_LEN
            subs = [s[i * SEL_LEN:(i + 1) * SEL_LEN] for i in range(nk // SEL_LEN)]
            bias = [block_bias[j0 + i:j0 + i + 1, :] for i in range(nk // SEL_LEN)]
        if bounded_scores:
            p = jnp.exp2(s) if block_bias is None else jnp.concatenate(
                [jnp.exp2(s_i + b_i) for s_i, b_i in zip(subs, bias)], axis=0)
        elif block_bias is None:
            m = jnp.max(s, axis=0, keepdims=True)
            p = jnp.exp2(s - m)
        else:
            for s_i, b_i in zip(subs, bias):
                m_i = jnp.max(s_i, axis=0, keepdims=True) + b_i
                m = m_i if m is None else jnp.maximum(m, m_i)
            shift = jnp.where(m < 0.5 * NEG_BIG, 0.0, m)
            p = jnp.concatenate([jnp.exp2(s_i + (b_i - shift)) for s_i, b_i in zip(subs, bias)], axis=0)
        vt = vt_ref[0, 0, :, c * K_CHUNK:c * K_CHUNK + nk]
        return m, _dot(vt, p.astype(BF16))

    def combine(stats):
        total = stats[0][1]
        if bounded_scores:
            for _, acc_i in stats[1:]:
                total = total + acc_i
        elif len(stats) > 1:
            m = stats[0][0]
            for st in stats[1:]:
                m = jnp.maximum(m, st[0])
            total = None
            for m_i, acc_i in stats:
                w = jnp.exp2(m_i - m)
                total = w * acc_i if total is None else total + w * acc_i
        return total[0:dh] * (1.0 / total[dh:dh + 1])

    n_win = WINDOW // K_CHUNK
    win_blocks = branch_blocks(kw_ref, vwt_ref, max(qb - n_win, 0), min(qb, n_win) + 1, qb >= n_win)
    blocks = win_blocks + branch_blocks(ks_ref, vst_ref, 0, qb + 1, False)

    ncb = min(kc_ref.shape[2], _round_up((s0 + TQ - CMP_LEN) // CMP_STRIDE + 1, 2 * SUBLANES))
    sc = _dot_nt(kc_ref[0, 0, 0:ncb, :], q)
    pending = {i: score_block(blocks[i]) for i in range(min(ATT_LOOKAHEAD, len(blocks)))}
    n_sub = lax.broadcasted_iota(jnp.int32, (ncb, cols), 0)
    mask_c = n_sub * CMP_STRIDE + (CMP_LEN - 1) <= t_lane
    sc = jnp.where(mask_c, sc, NEG_BIG)
    mc = jnp.max(sc, axis=0, keepdims=True)
    ec = jnp.where(mask_c, jnp.exp2(sc - mc), 0.0)
    pc = ec * (1.0 / jnp.maximum(jnp.sum(ec, axis=0, keepdims=True), 1e-30))
    o_c = _dot_tn(vc_ref[0, 0, 0:ncb, :], pc.astype(BF16))[0:dh]

    nb_live = (s0 + TQ) // SEL_LEN
    nb = min(ovt_ref.shape[0], _round_up(nb_live, 2 * SUBLANES))
    psum = pc[:, 0:TQ]
    for r in range(1, R):
        psum = psum + pc[:, r * TQ:(r + 1) * TQ]
    p_hi = psum.astype(BF16)
    p_lo = (psum - p_hi.astype(F32)).astype(BF16)
    ovt = ovt_ref[0:nb, 0:ncb]
    imp = _dot(ovt, p_hi) + _dot(ovt, p_lo)
    jb = lax.broadcasted_iota(jnp.int32, (nb, TQ), 0)
    tb = (s0 + lax.broadcasted_iota(jnp.int32, (nb, TQ), 1)) // SEL_LEN
    valid = jb <= tb
    forced = jnp.where(valid, jnp.where(jb == 0, 1.0, jnp.where(jb >= tb - 1, 1.0, 0.0)), 0.0)
    score = jnp.where(forced > 0.5, FORCE_SCORE, jnp.where(valid, imp, -1.0))
    rank = jnp.zeros((nb, TQ), F32)
    for i in range(nb_live):
        si = score[i:i + 1, :]
        tie_first = jnp.where(jb > i, 1.0, 0.0)
        rank = rank + jnp.where(si > score, 1.0, jnp.where(si == score, tie_first, 0.0))
    sel_bias = jnp.where(rank < n_sel, 0.0, NEG_BIG)
    sel_bias = jnp.concatenate([sel_bias] * R, axis=1)

    stats = []
    for i, blk in enumerate(blocks):
        is_sel = i >= len(win_blocks)
        stats.append(finish_block(pending.pop(i), blk, sel_bias if is_sel else None))
        if i + ATT_LOOKAHEAD < len(blocks):
            pending[i + ATT_LOOKAHEAD] = score_block(blocks[i + ATT_LOOKAHEAD])
    o_w = combine(stats[:len(win_blocks)])
    o_s = combine(stats[len(win_blocks):])

    gate = gt_ref[0, 0]
    pieces = []
    for r in range(R):
        sl = slice(r * TQ, (r + 1) * TQ)
        o_r = (gate[3 * r:3 * r + 1, :] * o_c[:, sl] + gate[3 * r + 1:3 * r + 2, :] * o_s[:, sl]
               + gate[3 * r + 2:3 * r + 3, :] * o_w[:, sl])
        pieces.append(o_r.T)
    o_ref[0] = jnp.concatenate(pieces, axis=1).astype(BF16)


def nsa_attend(q, kx, vt, cmp, gates_t, bounded_scores):
    B, H, S, _ = q.shape
    G, R, dh = NSA_GROUPS, NSA_REP, NSA_DH
    ncb = S // CMP_STRIDE
    nb = S // SEL_LEN
    assert (S % Q_TILE == 0 and Q_TILE == K_CHUNK and WINDOW % K_CHUNK == 0 and K_CHUNK % SEL_LEN == 0
            and 3 * R <= 2 * SUBLANES)
    cstart = np.arange(ncb) * CMP_STRIDE
    sstart = np.arange(nb) * SEL_LEN
    overlap = ((cstart[:, None] < sstart[None, :] + SEL_LEN) & (cstart[:, None] + CMP_LEN > sstart[None, :]))
    ovt = jnp.asarray(overlap.T, BF16)
    kx_spec = lambda idx: pl.BlockSpec((1, 1, S, LANES), lambda i, b, g, idx=idx: (b, idx * G + g, 0, 0))
    vt_spec = lambda idx: pl.BlockSpec((1, 1, VT_ROWS, S), lambda i, b, g, idx=idx: (b, idx * G + g, 0, 0))
    cmp_spec = lambda idx: pl.BlockSpec((1, 1, ncb, LANES), lambda i, b, g, idx=idx: (b, idx * G + g, 0, 0))
    return pl.pallas_call(
        functools.partial(_nsa_kernel, n_sel=min(SEL_TOPK, nb), n_tiles=S // Q_TILE, bounded_scores=bounded_scores),
        grid=(S // Q_TILE, B, G),
        in_specs=[
            pl.BlockSpec((1, R, Q_TILE, LANES), lambda i, b, g: (b, g, i, 0)),
            cmp_spec(0), cmp_spec(1),
            kx_spec(0), kx_spec(1),
            vt_spec(0), vt_spec(1),
            pl.BlockSpec((1, 1, 2 * SUBLANES, Q_TILE), lambda i, b, g: (b, g, 0, i)),
            pl.BlockSpec(ovt.shape, lambda i, b, g: (0, 0)),
        ],
        out_specs=pl.BlockSpec((1, Q_TILE, R * dh), lambda i, b, g: (b, i, g)),
        out_shape=jax.ShapeDtypeStruct((B, S, H * dh), BF16),
        compiler_params=_cparams(3),
        name="nsa_attend",
    )(q, cmp, cmp, kx, kx, vt, vt, gates_t, ovt)


def _hgrn_kernel(q_ref, lf_ref, v_ref, g_ref, ng_ref, tri_ref, o_ref, st_ref):
    C, SB = HGRN_CHUNK, HGRN_SUB
    NBK = C // SB
    S = q_ref.shape[1]
    st_ref[...] = jnp.zeros(st_ref.shape, F32)
    row = lax.broadcasted_iota(jnp.int32, (C, HGRN_DK), 0)
    sub_row = lax.broadcasted_iota(jnp.int32, (SB, 1), 0)
    cr = lax.broadcasted_iota(jnp.int32, (C, C), 0)
    cc = lax.broadcasted_iota(jnp.int32, (C, C), 1)
    diag_mask = (cr // SB == cc // SB) & (cc <= cr)

    def head_chunk(h, r0, bounded_decay):
        hs = slice(h * HGRN_DK, (h + 1) * HGRN_DK)
        q = q_ref[0, pl.ds(r0, C), hs].astype(F32)
        lf = lf_ref[0, pl.ds(r0, C), hs]
        v_bf = v_ref[0, pl.ds(r0, C), hs]
        v = v_bf.astype(F32)
        kh = 1.0 - jnp.exp(lf)
        tri = tri_ref[...]
        lf0 = lf.astype(BF16)
        lf1 = (lf - lf0.astype(F32)).astype(BF16)
        lf2 = (lf - lf0.astype(F32) - lf1.astype(F32)).astype(BF16)
        b = _dot(tri, lf0) + _dot(tri, lf1) + _dot(tri, lf2)
        yield
        b_last = b[C - 1:C, :]
        st = st_ref[h]
        o = _dot_nt((q * jnp.exp(b)).astype(BF16), st.astype(BF16))
        b_end = jnp.concatenate(
            [jnp.broadcast_to(b[(j + 1) * SB - 1:(j + 1) * SB, :], (SB, HGRN_DK)) for j in range(NBK)], axis=0)
        k_end = kh * jnp.exp(b_end - b)
        q_parts, k_parts = [], []
        for j in range(NBK - 1):
            lo = (j + 1) * SB
            qj = q[lo:] * jnp.exp(b[lo:] - b[lo - 1:lo, :])
            q_parts.append(jnp.concatenate([jnp.zeros((lo, HGRN_DK), F32), qj], axis=0))
            k_parts.append(jnp.where((row >= j * SB) & (row < lo), k_end, 0.0))
        q_cat = jnp.concatenate(q_parts, axis=1).astype(BF16)
        k_cat = jnp.concatenate(k_parts, axis=1).astype(BF16)
        a_off = _dot_nt(q_cat, k_cat)
        k_last = (kh * jnp.exp(b_last - b)).astype(BF16)
        st_ref[h] = st * jnp.exp(b_last) + _dot_tn(v_bf, k_last)
        if bounded_decay:
            b_start = jnp.concatenate([jnp.zeros((SB, HGRN_DK), F32), b_end[:C - SB]], axis=0)
            a_dg = _dot_nt((q * jnp.exp(b - b_start)).astype(BF16), (kh * jnp.exp(b_start - b)).astype(BF16))
            yield
            o = o + _dot(jnp.where(diag_mask, a_dg, a_off).astype(BF16), v_bf)
            yield
        else:
            yield
            o = o + _dot(a_off.astype(BF16), v_bf)
            diag = []
            for j in range(NBK):
                sl = slice(j * SB, (j + 1) * SB)
                qj, bj, kj, vj = q[sl], b[sl], kh[sl], v[sl]
                oj = jnp.zeros((SB, HGRN_DV), F32)
                for s in range(SB):
                    w = jnp.exp(jnp.minimum(bj - bj[s:s + 1, :], 0.0))
                    a = jnp.sum(qj * kj[s:s + 1, :] * w, axis=-1, keepdims=True)
                    a = jnp.where(sub_row >= s, a, 0.0)
                    oj = oj + a * vj[s:s + 1, :]
                diag.append(oj)
            o = o + jnp.concatenate(diag, axis=0)
            yield
        o = o * lax.rsqrt(jnp.mean(o * o, axis=-1, keepdims=True) + RMS_EPS) * ng_ref[...]
        o_ref[0, pl.ds(r0, C), hs] = (o * g_ref[0, pl.ds(r0, C), hs].astype(F32)).astype(BF16)

    def chunk(bounded_decay, ci, carry):
        r0 = pl.multiple_of(ci * (C * HGRN_UNROLL), C * HGRN_UNROLL)
        heads = [head_chunk(h, r0 + u * C, bounded_decay) for u in range(HGRN_UNROLL) for h in range(HGRN_HEADS)]
        for _ in range(3):
            for gen in heads:
                next(gen)
        for gen in heads:
            next(gen, None)
        return carry

    bounded = jnp.min(lf_ref[0]) >= -HGRN_MAX_STEP_DECAY

    @pl.when(bounded)
    def _():
        lax.fori_loop(0, S // (C * HGRN_UNROLL), functools.partial(chunk, True), 0)

    @pl.when(jnp.logical_not(bounded))
    def _():
        lax.fori_loop(0, S // (C * HGRN_UNROLL), functools.partial(chunk, False), 0)


def hgrn2(hq, hlf, hv, hg, norm_g):
    B, S, _ = hq.shape
    C = HGRN_CHUNK
    assert S % (C * HGRN_UNROLL) == 0
    tri = jnp.asarray(np.tril(np.ones((C, C))), BF16)
    spec = lambda n: pl.BlockSpec((1, S, n), lambda b: (b, 0, 0))
    return pl.pallas_call(
        _hgrn_kernel,
        grid=(B,),
        in_specs=[spec(HGRN_WIDTH), spec(HGRN_WIDTH), spec(HGRN_VWIDTH), spec(HGRN_VWIDTH),
                  pl.BlockSpec((1, HGRN_DV), lambda b: (0, 0)),
                  pl.BlockSpec((C, C), lambda b: (0, 0))],
        out_specs=spec(HGRN_VWIDTH),
        out_shape=jax.ShapeDtypeStruct((B, S, HGRN_VWIDTH), BF16),
        scratch_shapes=[pltpu.VMEM((HGRN_HEADS, HGRN_DV, HGRN_DK), F32)],
        compiler_params=_cparams(1),
        name="hgrn2",
    )(hq, hlf, hv, hg, norm_g.reshape(1, HGRN_DV).astype(F32), tri)


def _merge_kernel(ya_ref, yb_ref, gma_ref, gmb_ref, x_ref, gt_ref, lg_ref, lbias_ref,
                  wa_ref, wb_ref, wo_ref, o_ref):
    pa = _dot(ya_ref[0], wa_ref[...])
    pb = _dot(yb_ref[0], wb_ref[...])
    merged = gma_ref[0].astype(F32) * pa + gmb_ref[0].astype(F32) * pb
    y = _dot(merged.astype(BF16), wo_ref[...])
    z = DEEPNORM_ALPHA * x_ref[0] + (1.0 + gt_ref[0]) * y
    o_ref[0] = _layer_norm(z, lg_ref[...], lbias_ref[...])


def merge_out(ya, yb, gma, gmb, x, gt, ln_g, ln_b, wa, wb, wo):
    B, S, D = x.shape
    tm = min(TOKEN_TILE, S)
    tok = lambda n: pl.BlockSpec((1, tm, n), lambda b, i: (b, i, 0))
    return pl.pallas_call(
        _merge_kernel,
        grid=(B, S // tm),
        in_specs=[tok(NSA_WIDTH), tok(HGRN_VWIDTH), tok(D), tok(D), tok(D),
                  pl.BlockSpec((1, 1, D), lambda b, i: (b, 0, 0)),
                  _resident((1, D)), _resident((1, D)),
                  _resident(wa.shape), _resident(wb.shape), _resident(wo.shape)],
        out_specs=tok(D),
        out_shape=jax.ShapeDtypeStruct((B, S, D), F32),
        compiler_params=_cparams(2),
        name="merge_out",
    )(ya, yb, gma, gmb, x, gt, ln_g.reshape(1, D), ln_b.reshape(1, D), wa, wb, wo)


MLP_COLS = 1024


def _mlp_kernel(x_ref, sc_ref, sh_ref, gt_ref, lg_ref, lbias_ref, w1_ref, w2_ref, o_ref):
    x = x_ref[0]
    u = (x * (1.0 + sc_ref[0]) + sh_ref[0]).astype(BF16)
    y = jnp.zeros(x.shape, F32)
    for c in range(MLP_HIDDEN // MLP_COLS):
        h = jnp.maximum(_dot(u, w1_ref[:, c * MLP_COLS:(c + 1) * MLP_COLS]), 0.0)
        y = y + _dot((h * h).astype(BF16), w2_ref[c * MLP_COLS:(c + 1) * MLP_COLS, :])
    z = DEEPNORM_ALPHA * x + (1.0 + gt_ref[0]) * y
    o_ref[0] = _layer_norm(z, lg_ref[...], lbias_ref[...])


def mlp(x, sc, sh, gt, ln_g, ln_b, w1, w2):
    B, S, D = x.shape
    tm = min(TOKEN_TILE, S)
    tok = pl.BlockSpec((1, tm, D), lambda b, i: (b, i, 0))
    per_b = pl.BlockSpec((1, 1, D), lambda b, i: (b, 0, 0))
    return pl.pallas_call(
        _mlp_kernel,
        grid=(B, S // tm),
        in_specs=[tok, per_b, per_b, per_b, _resident((1, D)), _resident((1, D)),
                  _resident(w1.shape), _resident(w2.shape)],
        out_specs=tok,
        out_shape=jax.ShapeDtypeStruct((B, S, D), F32),
        compiler_params=_cparams(2),
        name="mlp",
    )(x, sc, sh, gt, ln_g.reshape(1, D), ln_b.reshape(1, D), w1, w2)


def _rope_tables(S):
    inv = 1.0 / (ROPE_THETA ** (jnp.arange(0, NSA_DH, 2, dtype=F32) / NSA_DH))
    ang = jnp.arange(S, dtype=F32)[:, None] * inv[None, :]
    cos, sin = jnp.cos(ang), jnp.sin(ang)
    reps = LANES // NSA_DH
    return (jnp.tile(jnp.concatenate([cos, cos], axis=1), (1, reps)),
            jnp.tile(jnp.concatenate([-sin, sin], axis=1), (1, reps)))


def kernel(x, c, w_in, b_in, cmp_pe_k, cmp_pe_v, cmp_wk1, cmp_wk2, cmp_wv1, cmp_wv2, hgrn_lb_logits, hgrn_norm_g, w_branch_a, w_branch_b, w_out, w_ada, b_ada, ln1_g, ln1_b, w_mlp1, w_mlp2, ln2_g, ln2_b):
    B, S, D = x.shape
    G = NSA_GROUPS
    lb_all = jnp.cumsum(jax.nn.softmax(hgrn_lb_logits.astype(F32), axis=0), axis=0)
    lb_all = lb_all - lb_all[0:1]
    cos_t, sin_t = _rope_tables(S)
    mod = adaln_mod(c, w_ada, b_ada)
    for l in range(DEPTH):
        sh1, sc1, gt1, sh2, sc2, gt2 = [mod[l, :, None, i * D:(i + 1) * D] for i in range(6)]
        wts = _prep_in_proj_weights(w_in[l], b_in[l])
        q, kx, kv, vt, gates_t, nrm, hq, hlf, hv, hg, gma, gmb = in_proj(x, sc1, sh1, cos_t, sin_t, lb_all[l].reshape(1, -1), wts)
        pe = jnp.stack([cmp_pe_k[l].reshape(1, -1), cmp_pe_v[l].reshape(1, -1)])
        w1 = jnp.stack([cmp_wk1[l], cmp_wv1[l]]).astype(BF16)
        w2 = jnp.pad(jnp.stack([cmp_wk2[l], cmp_wv2[l]]), ((0, 0), (0, 0), (0, LANES - NSA_DH))).astype(BF16)
        cmp = nsa_compress(kv, pe, w1, w2)
        bounded = jnp.max(nrm[:, :, 0, 0]) * jnp.max(nrm[:, :, 1, 0]) <= SCORE_BOUND ** 2
        ya = lax.cond(bounded,
                      functools.partial(nsa_attend, bounded_scores=True),
                      functools.partial(nsa_attend, bounded_scores=False),
                      q, kx, vt, cmp, gates_t)
        yb = hgrn2(hq, hlf, hv, hg, hgrn_norm_g[l])
        x = merge_out(ya, yb, gma, gmb, x, gt1, ln1_g[l], ln1_b[l],
                      w_branch_a[l].astype(BF16), w_branch_b[l].astype(BF16), w_out[l].astype(BF16))
        x = mlp(x, sc2, sh2, gt2, ln2_g[l], ln2_b[l], w_mlp1[l].astype(BF16), w_mlp2[l].astype(BF16))
    return x
```

```python
import functools

import numpy as np
import jax
import jax.numpy as jnp
from jax import lax
from jax.experimental import pallas as pl
from jax.experimental.pallas import tpu as pltpu

D_MODEL = 1024
DEPTH = 2
NSA_HEADS = 8
NSA_GROUPS = 2
NSA_REP = NSA_HEADS // NSA_GROUPS
NSA_DH = 64
NSA_WIDTH = NSA_HEADS * NSA_DH
NSA_KV_WIDTH = NSA_GROUPS * NSA_DH
CMP_LEN = 32
CMP_STRIDE = 16
CMP_HIDDEN = 2 * NSA_DH
SEL_LEN = 64
SEL_TOPK = 8
FORCE_SCORE = 1.0e4
WINDOW = 512
HGRN_HEADS = 4
HGRN_DK = 128
HGRN_DV = 128
HGRN_WIDTH = HGRN_HEADS * HGRN_DK
HGRN_VWIDTH = HGRN_HEADS * HGRN_DV
MLP_HIDDEN = 4 * D_MODEL
ROPE_THETA = 10000.0
LN_EPS = 1e-5
RMS_EPS = 1e-6
DEEPNORM_ALPHA = (2 * DEPTH) ** 0.25
IN_SIZES = (NSA_WIDTH,) + (NSA_KV_WIDTH,) * 6 + (3 * NSA_HEADS,) + (HGRN_WIDTH, HGRN_WIDTH, HGRN_VWIDTH, HGRN_VWIDTH) + (D_MODEL, D_MODEL)
IN_OFFSETS = [0] + [int(v) for v in np.cumsum(IN_SIZES)]

LANES = 128
SUBLANES = 8
VMEM_LIMIT = 48 * 1024 * 1024
TOKEN_TILE = 512
Q_TILE = 128
K_CHUNK = 128
ATT_BLOCK = 4
ATT_LOOKAHEAD = 2
HGRN_CHUNK = 64
HGRN_UNROLL = 4
HGRN_SUB = 8
HGRN_MAX_STEP_DECAY = 7.5
NEG_BIG = -1e30
LOG2E = 1.4426950408889634
Q_SCALE = NSA_DH ** -0.5 * LOG2E
SCORE_BOUND = 96.0
VT_ROWS = NSA_DH + 16

F32 = jnp.float32
BF16 = jnp.bfloat16


def _cparams(n_grid):
    return pltpu.CompilerParams(dimension_semantics=("arbitrary",) * n_grid, vmem_limit_bytes=VMEM_LIMIT)


def _resident(shape):
    nd = len(shape)
    return pl.BlockSpec(shape, lambda *_: (0,) * nd, pipeline_mode=pl.Buffered(1))


def _dot(a, b):
    return jnp.dot(a, b, preferred_element_type=F32)


def _dot_nt(a, b):
    return lax.dot_general(a, b, (((1,), (1,)), ((), ())), preferred_element_type=F32)


def _dot_tn(a, b):
    return lax.dot_general(a, b, (((0,), (0,)), ((), ())), preferred_element_type=F32)


def _sigmoid(x):
    return 1.0 / (1.0 + jnp.exp(-x))


def _silu(x):
    return x * _sigmoid(x)


def _layer_norm(z, g, b):
    mu = jnp.mean(z, axis=-1, keepdims=True)
    zc = z - mu
    var = jnp.mean(zc * zc, axis=-1, keepdims=True)
    return zc * lax.rsqrt(var + LN_EPS) * g + b


def _adaln_kernel(c_ref, w_ref, b_ref, o_ref):
    cond = _silu(c_ref[...]).astype(BF16)
    o_ref[0] = _dot(cond, w_ref[0]) + b_ref[0]


def adaln_mod(c, w_ada, b_ada):
    L, D, N = w_ada.shape
    B = c.shape[0]
    tn = D
    return pl.pallas_call(
        _adaln_kernel,
        grid=(L, N // tn),
        in_specs=[
            pl.BlockSpec((B, D), lambda l, j: (0, 0)),
            pl.BlockSpec((1, D, tn), lambda l, j: (l, 0, j)),
            pl.BlockSpec((1, 1, tn), lambda l, j: (l, 0, j)),
        ],
        out_specs=pl.BlockSpec((1, B, tn), lambda l, j: (l, 0, j)),
        out_shape=jax.ShapeDtypeStruct((L, B, N), F32),
        compiler_params=_cparams(2),
        name="adaln_mod",
    )(c, w_ada.astype(BF16), b_ada.reshape(L, 1, N))


N_ROPE = NSA_WIDTH + 3 * NSA_KV_WIDTH
N_NSA = N_ROPE + 3 * NSA_KV_WIDTH
N_GATE = NSA_GROUPS * LANES
N_HGRN = 2 * HGRN_WIDTH + 2 * HGRN_VWIDTH
N_MERGE = 2 * D_MODEL


def _in_proj_kernel(x_ref, sc_ref, sh_ref, cos_ref, sin_ref, lb_ref,
                    wn_ref, bn_ref, wg_ref, bg_ref, wh_ref, bh_ref, wm_ref, bm_ref,
                    q_ref, kx_ref, kv_ref, vt_ref, ga_ref, nrm_ref, hq_ref, hlf_ref, hv_ref, hg_ref, gma_ref, gmb_ref):
    u = (x_ref[0] * (1.0 + sc_ref[0]) + sh_ref[0]).astype(BF16)
    cos = cos_ref[...]
    sin = sin_ref[...]
    lane = lax.broadcasted_iota(jnp.int32, cos.shape, 1)
    first_half = (lane % NSA_DH) < (NSA_DH // 2)
    low = lane < NSA_DH

    def heads(t, upper):
        return jnp.where(low, t, upper), jnp.where(low, pltpu.roll(t, NSA_DH, 1), upper)

    n_q, n_rope = NSA_HEADS // 2, N_ROPE // LANES
    tm = u.shape[0]
    ones_rows = jnp.where(lax.broadcasted_iota(jnp.int32, (VT_ROWS - NSA_DH, tm), 0) == 0, 1.0, 0.0).astype(BF16)

    def max_sq_norm(t, acc):
        n = jnp.max(jnp.sum(t * t, axis=1, keepdims=True), axis=0, keepdims=True)
        return n if acc is None else jnp.maximum(acc, n)

    q_sq = k_sq = None
    for i2 in range(0, N_NSA // LANES, 2):
        t2 = _dot(u, wn_ref[:, i2 * LANES:(i2 + 2) * LANES]) + bn_ref[:, i2 * LANES:(i2 + 2) * LANES]
        for i in (i2, i2 + 1):
            t = t2[:, (i - i2) * LANES:(i - i2 + 1) * LANES]
            if i < n_rope:
                rot = jnp.where(first_half, pltpu.roll(t, LANES - NSA_DH // 2, 1), pltpu.roll(t, NSA_DH // 2, 1))
                t = t * cos + rot * sin
            if i < n_q:
                t = t * Q_SCALE
                q_sq = max_sq_norm(t, q_sq)
                for j, piece in enumerate(heads(t, 0.0)):
                    q_ref[0, 2 * i + j] = piece.astype(BF16)
            elif i == n_q or i == n_rope:
                for j in range(2):
                    kv_ref[0, (2 if i == n_rope else 0) + j] = t[:, j * NSA_DH:(j + 1) * NSA_DH].astype(BF16)
            elif i < n_rope:
                kind = i - n_q - 1
                k_sq = max_sq_norm(t, k_sq)
                for j, piece in enumerate(heads(t, 0.0)):
                    kx_ref[0, 2 * kind + j] = piece.astype(BF16)
            else:
                kind = i - n_rope - 1
                tt = t.T.astype(BF16)
                for j in range(2):
                    vt_ref[0, 2 * kind + j] = jnp.concatenate([tt[j * NSA_DH:(j + 1) * NSA_DH], ones_rows], axis=0)
    gates = _sigmoid(_dot(u, wg_ref[...]) + bg_ref[...])
    for g in range(NSA_GROUPS):
        ga_ref[0, g] = gates[:, g * LANES:(g + 1) * LANES].T[0:2 * SUBLANES]
    W = HGRN_WIDTH
    hq = _dot(u, wh_ref[:, 0:W]) + bh_ref[:, 0:W]
    hq_ref[0] = (_silu(hq) * (HGRN_DK ** -0.5)).astype(BF16)
    z = _dot(u, wh_ref[:, W:2 * W]) + bh_ref[:, W:2 * W]
    lb = lb_ref[...]
    log_sig = jnp.minimum(z, 0.0) - jnp.log1p(jnp.exp(-jnp.abs(z)))
    a = jnp.log(lb)
    bb = jnp.log1p(-lb) + log_sig
    log_f = jnp.maximum(a, bb) + jnp.log1p(jnp.exp(-jnp.abs(a - bb)))
    hlf_ref[0] = log_f
    lf_min = jnp.min(jnp.min(log_f, axis=1, keepdims=True), axis=0, keepdims=True)
    srow = lax.broadcasted_iota(jnp.int32, (SUBLANES, LANES), 0)
    nrm_ref[0, 0] = jnp.where(srow == 0, q_sq, jnp.where(srow == 1, k_sq, jnp.where(srow == 2, lf_min, 0.0)))
    hv_ref[0] = (_dot(u, wh_ref[:, 2 * W:3 * W]) + bh_ref[:, 2 * W:3 * W]).astype(BF16)
    hg_ref[0] = _silu(_dot(u, wh_ref[:, 3 * W:4 * W]) + bh_ref[:, 3 * W:4 * W]).astype(BF16)
    for i, ref in enumerate((gma_ref, gmb_ref)):
        for j in range(2):
            c0 = i * D_MODEL + j * (D_MODEL // 2)
            c1 = c0 + D_MODEL // 2
            ref[0, :, j * (D_MODEL // 2):(j + 1) * (D_MODEL // 2)] = _sigmoid(
                _dot(u, wm_ref[:, c0:c1]) + bm_ref[:, c0:c1]).astype(BF16)


def in_proj(x, sc, sh, cos_t, sin_t, lb, wts):
    B, S, D = x.shape
    tm = min(TOKEN_TILE, S)
    assert S // SEL_LEN <= LANES - NSA_DH
    wn, bn, wg, bg, wh, bh, wm, bm = wts
    tok = lambda n: pl.BlockSpec((1, tm, n), lambda b, i: (b, i, 0))
    per_b = pl.BlockSpec((1, 1, D), lambda b, i: (b, 0, 0))
    tab = pl.BlockSpec((tm, LANES), lambda b, i: (i, 0))
    out_shape = (
        jax.ShapeDtypeStruct((B, NSA_HEADS, S, LANES), BF16),
        jax.ShapeDtypeStruct((B, 2 * NSA_GROUPS, S, LANES), BF16),
        jax.ShapeDtypeStruct((B, 2 * NSA_GROUPS, S, NSA_DH), BF16),
        jax.ShapeDtypeStruct((B, 2 * NSA_GROUPS, VT_ROWS, S), BF16),
        jax.ShapeDtypeStruct((B, NSA_GROUPS, 2 * SUBLANES, S), F32),
        jax.ShapeDtypeStruct((B, S // tm, SUBLANES, LANES), F32),
        jax.ShapeDtypeStruct((B, S, HGRN_WIDTH), BF16),
        jax.ShapeDtypeStruct((B, S, HGRN_WIDTH), F32),
        jax.ShapeDtypeStruct((B, S, HGRN_VWIDTH), BF16),
        jax.ShapeDtypeStruct((B, S, HGRN_VWIDTH), BF16),
        jax.ShapeDtypeStruct((B, S, D), BF16),
        jax.ShapeDtypeStruct((B, S, D), BF16),
    )
    out_specs = (
        pl.BlockSpec((1, NSA_HEADS, tm, LANES), lambda b, i: (b, 0, i, 0)),
        pl.BlockSpec((1, 2 * NSA_GROUPS, tm, LANES), lambda b, i: (b, 0, i, 0)),
        pl.BlockSpec((1, 2 * NSA_GROUPS, tm, NSA_DH), lambda b, i: (b, 0, i, 0)),
        pl.BlockSpec((1, 2 * NSA_GROUPS, VT_ROWS, tm), lambda b, i: (b, 0, 0, i)),
        pl.BlockSpec((1, NSA_GROUPS, 2 * SUBLANES, tm), lambda b, i: (b, 0, 0, i)),
        pl.BlockSpec((1, 1, SUBLANES, LANES), lambda b, i: (b, i, 0, 0)),
        tok(HGRN_WIDTH), tok(HGRN_WIDTH), tok(HGRN_VWIDTH), tok(HGRN_VWIDTH), tok(D), tok(D),
    )
    return pl.pallas_call(
        _in_proj_kernel,
        grid=(B, S // tm),
        in_specs=[tok(D), per_b, per_b, tab, tab, _resident(lb.shape),
                  _resident(wn.shape), _resident(bn.shape), _resident(wg.shape), _resident(bg.shape),
                  _resident(wh.shape), _resident(bh.shape), _resident(wm.shape), _resident(bm.shape)],
        out_specs=out_specs,
        out_shape=out_shape,
        compiler_params=_cparams(2),
        name="in_proj",
    )(x, sc, sh, cos_t, sin_t, lb, wn, bn, wg, bg, wh, bh, wm, bm)


def _prep_in_proj_weights(w_in_l, b_in_l):
    o = IN_OFFSETS
    col = lambda i: (w_in_l[:, o[i]:o[i + 1]], b_in_l[o[i]:o[i + 1]])
    q_a, k_c, v_c, k_s, v_s, k_w, v_w, g_a, q_b, f_b, i_b, g_b, gm_a, gm_b = [col(i) for i in range(14)]

    def cat(parts):
        return (jnp.concatenate([p[0] for p in parts], axis=1).astype(BF16),
                jnp.concatenate([p[1] for p in parts], axis=0).reshape(1, -1).astype(F32))

    wn, bn = cat([q_a, k_c, k_s, k_w, v_c, v_s, v_w])
    per_group = 3 * NSA_REP
    gw = jnp.zeros((w_in_l.shape[0], N_GATE), w_in_l.dtype)
    gb = jnp.zeros((N_GATE,), b_in_l.dtype)
    for g in range(NSA_GROUPS):
        gw = gw.at[:, g * LANES:g * LANES + per_group].set(g_a[0][:, g * per_group:(g + 1) * per_group])
        gb = gb.at[g * LANES:g * LANES + per_group].set(g_a[1][g * per_group:(g + 1) * per_group])
    wg, bg = gw.astype(BF16), gb.reshape(1, -1).astype(F32)
    wh, bh = cat([q_b, f_b, i_b, g_b])
    wm, bm = cat([gm_a, gm_b])
    return wn, bn, wg, bg, wh, bh, wm, bm


def _compress_kernel(t_ref, pe_ref, w1_ref, w2_ref, o_ref):
    half = CMP_STRIDE * NSA_DH
    t = t_ref[0, 0]
    nrow = t.shape[0]
    a = _dot(t, w1_ref[0, 0:half, :])
    b = _dot(t, w1_ref[0, half:2 * half, :])
    pe = jnp.broadcast_to(pe_ref[0], (8, 2 * half)).astype(BF16)
    c = _dot(pe, w1_ref[0])[0:1]
    h = a + pltpu.roll(b, nrow - 1, 0) + c
    o_ref[0, 0] = _dot(_silu(h).astype(BF16), w2_ref[0]).astype(BF16)


def nsa_compress(kv, pe, w1, w2):
    B, _, S, dh = kv.shape
    nrow = S // CMP_STRIDE
    G = NSA_GROUPS
    kv_rows = kv.reshape(B, 2 * G, nrow, CMP_STRIDE * dh)
    return pl.pallas_call(
        _compress_kernel,
        grid=(B, 2, G),
        in_specs=[
            pl.BlockSpec((1, 1, nrow, CMP_STRIDE * dh), lambda b, s, g: (b, s * G + g, 0, 0)),
            pl.BlockSpec((1, 1, CMP_LEN * dh), lambda b, s, g: (s, 0, 0)),
            pl.BlockSpec((1, CMP_LEN * dh, CMP_HIDDEN), lambda b, s, g: (s, 0, 0)),
            pl.BlockSpec((1, CMP_HIDDEN, LANES), lambda b, s, g: (s, 0, 0)),
        ],
        out_specs=pl.BlockSpec((1, 1, nrow, LANES), lambda b, s, g: (b, s * G + g, 0, 0)),
        out_shape=jax.ShapeDtypeStruct((B, 2 * G, nrow, LANES), BF16),
        compiler_params=_cparams(3),
        name="nsa_compress",
    )(kv_rows, pe, w1, w2)


def _nsa_kernel(q_ref, kc_ref, vc_ref, ks_ref, kw_ref, vst_ref, vwt_ref, gt_ref, ovt_ref,
                o_ref, *, n_sel, n_tiles, bounded_scores):
    for qs in range(n_tiles):
        pl.when(pl.program_id(2) == qs)(functools.partial(
            _nsa_tile, qs, q_ref, kc_ref, vc_ref, ks_ref, kw_ref, vst_ref, vwt_ref, gt_ref, ovt_ref,
            o_ref, n_sel, bounded_scores))


def _round_up(x, m):
    return -(-x // m) * m


def _nsa_tile(qb, q_ref, kc_ref, vc_ref, ks_ref, kw_ref, vst_ref, vwt_ref, gt_ref, ovt_ref, o_ref, n_sel,
              bounded_scores):
    R, TQ, dh = NSA_REP, Q_TILE, NSA_DH
    cols = R * TQ
    s0 = qb * TQ
    q = q_ref[0].reshape(cols, LANES)
    t_lane = s0 + (lax.broadcasted_iota(jnp.int32, (1, cols), 1) % TQ)
    kidx = lax.broadcasted_iota(jnp.int32, (K_CHUNK, cols), 0)
    tq = lax.broadcasted_iota(jnp.int32, (K_CHUNK, cols), 1) % TQ
    causal = kidx <= tq

    def branch_blocks(k_ref, vt_ref, c0, n, band_first):
        return [(k_ref, vt_ref, c0 + b0, min(ATT_BLOCK, n - b0), band_first and b0 == 0, b0 + ATT_BLOCK >= n)
                for b0 in range(0, n, ATT_BLOCK)]

    def score_block(blk):
        k_ref, _, c, n, band, diag = blk
        s = _dot_nt(k_ref[0, 0, c * K_CHUNK:(c + n) * K_CHUNK, :], q)
        parts = [s[i * K_CHUNK:(i + 1) * K_CHUNK] for i in range(n)]
        if band:
            parts[0] = jnp.where(tq < kidx, parts[0], NEG_BIG)
        if diag:
            parts[-1] = jnp.where(causal, parts[-1], NEG_BIG)
        return jnp.concatenate(parts, axis=0) if n > 1 else parts[0]

    def finish_block(s, blk, block_bias=None):
        _, vt_ref, c, n, _, _ = blk
        nk = n * K_CHUNK
        m = None
        if block_bias is not None:
            j0 = c * K_CHUNK // SEL_LEN
            subs = [s[i * SEL_LEN:(i + 1) * SEL_LEN] for i in range(nk // SEL_LEN)]
            bias = [block_bias[j0 + i:j0 + i + 1, :] for i in range(nk // SEL_LEN)]
        if bounded_scores:
            p = jnp.exp2(s) if block_bias is None else jnp.concatenate(
                [jnp.exp2(s_i + b_i) for s_i, b_i in zip(subs, bias)], axis=0)
        elif block_bias is None:
            m = jnp.max(s, axis=0, keepdims=True)
            p = jnp.exp2(s - m)
        else:
            for s_i, b_i in zip(subs, bias):
                m_i = jnp.max(s_i, axis=0, keepdims=True) + b_i
                m = m_i if m is None else jnp.maximum(m, m_i)
            shift = jnp.where(m < 0.5 * NEG_BIG, 0.0, m)
            p = jnp.concatenate([jnp.exp2(s_i + (b_i - shift)) for s_i, b_i in zip(subs, bias)], axis=0)
        vt = vt_ref[0, 0, :, c * K_CHUNK:c * K_CHUNK + nk]
        return m, _dot(vt, p.astype(BF16))

    def combine(stats):
        total = stats[0][1]
        if bounded_scores:
            for _, acc_i in stats[1:]:
                total = total + acc_i
        elif len(stats) > 1:
            m = stats[0][0]
            for st in stats[1:]:
                m = jnp.maximum(m, st[0])
            total = None
            for m_i, acc_i in stats:
                w = jnp.exp2(m_i - m)
                total = w * acc_i if total is None else total + w * acc_i
        return total[0:dh] * (1.0 / total[dh:dh + 1])

    n_win = WINDOW // K_CHUNK
    win_blocks = branch_blocks(kw_ref, vwt_ref, max(qb - n_win, 0), min(qb, n_win) + 1, qb >= n_win)
    blocks = win_blocks + branch_blocks(ks_ref, vst_ref, 0, qb + 1, False)

    ncb = min(kc_ref.shape[2], _round_up((s0 + TQ - CMP_LEN) // CMP_STRIDE + 1, 2 * SUBLANES))
    sc = _dot_nt(kc_ref[0, 0, 0:ncb, :], q)
    pending = {i: score_block(blocks[i]) for i in range(min(ATT_LOOKAHEAD, len(blocks)))}
    n_sub = lax.broadcasted_iota(jnp.int32, (ncb, cols), 0)
    mask_c = n_sub * CMP_STRIDE + (CMP_LEN - 1) <= t_lane
    sc = jnp.where(mask_c, sc, NEG_BIG)
    mc = jnp.max(sc, axis=0, keepdims=True)
    ec = jnp.where(mask_c, jnp.exp2(sc - mc), 0.0)
    pc = ec * (1.0 / jnp.maximum(jnp.sum(ec, axis=0, keepdims=True), 1e-30))
    o_c = _dot_tn(vc_ref[0, 0, 0:ncb, :], pc.astype(BF16))[0:dh]

    nb_live = (s0 + TQ) // SEL_LEN
    nb = min(ovt_ref.shape[0], _round_up(nb_live, 2 * SUBLANES))
    psum = pc[:, 0:TQ]
    for r in range(1, R):
        psum = psum + pc[:, r * TQ:(r + 1) * TQ]
    p_hi = psum.astype(BF16)
    p_lo = (psum - p_hi.astype(F32)).astype(BF16)
    ovt = ovt_ref[0:nb, 0:ncb]
    imp = _dot(ovt, p_hi) + _dot(ovt, p_lo)
    jb = lax.broadcasted_iota(jnp.int32, (nb, TQ), 0)
    tb = (s0 + lax.broadcasted_iota(jnp.int32, (nb, TQ), 1)) // SEL_LEN
    valid = jb <= tb
    forced = jnp.where(valid, jnp.where(jb == 0, 1.0, jnp.where(jb >= tb - 1, 1.0, 0.0)), 0.0)
    score = jnp.where(forced > 0.5, FORCE_SCORE, jnp.where(valid, imp, -1.0))
    rank = jnp.zeros((nb, TQ), F32)
    for i in range(nb_live):
        si = score[i:i + 1, :]
        tie_first = jnp.where(jb > i, 1.0, 0.0)
        rank = rank + jnp.where(si > score, 1.0, jnp.where(si == score, tie_first, 0.0))
    sel_bias = jnp.where(rank < n_sel, 0.0, NEG_BIG)
    sel_bias = jnp.concatenate([sel_bias] * R, axis=1)

    stats = []
    for i, blk in enumerate(blocks):
        is_sel = i >= len(win_blocks)
        stats.append(finish_block(pending.pop(i), blk, sel_bias if is_sel else None))
        if i + ATT_LOOKAHEAD < len(blocks):
            pending[i + ATT_LOOKAHEAD] = score_block(blocks[i + ATT_LOOKAHEAD])
    o_w = combine(stats[:len(win_blocks)])
    o_s = combine(stats[len(win_blocks):])

    gate = gt_ref[0, 0]
    pieces = []
    for r in range(R):
        sl = slice(r * TQ, (r + 1) * TQ)
        o_r = (gate[3 * r:3 * r + 1, :] * o_c[:, sl] + gate[3 * r + 1:3 * r + 2, :] * o_s[:, sl]
               + gate[3 * r + 2:3 * r + 3, :] * o_w[:, sl])
        pieces.append(o_r.T)
    o_ref[0] = jnp.concatenate(pieces, axis=1).astype(BF16)


def nsa_attend(q, kx, vt, cmp, gates_t, bounded_scores):
    B, H, S, _ = q.shape
    G, R, dh = NSA_GROUPS, NSA_REP, NSA_DH
    ncb = S // CMP_STRIDE
    nb = S // SEL_LEN
    assert (S % Q_TILE == 0 and Q_TILE == K_CHUNK and WINDOW % K_CHUNK == 0 and K_CHUNK % SEL_LEN == 0
            and 3 * R <= 2 * SUBLANES)
    cstart = np.arange(ncb) * CMP_STRIDE
    sstart = np.arange(nb) * SEL_LEN
    overlap = ((cstart[:, None] < sstart[None, :] + SEL_LEN) & (cstart[:, None] + CMP_LEN > sstart[None, :]))
    ovt = jnp.asarray(overlap.T, BF16)
    kx_spec = lambda idx: pl.BlockSpec((1, 1, S, LANES), lambda b, g, i, idx=idx: (b, idx * G + g, 0, 0))
    vt_spec = lambda idx: pl.BlockSpec((1, 1, VT_ROWS, S), lambda b, g, i, idx=idx: (b, idx * G + g, 0, 0))
    cmp_spec = lambda idx: pl.BlockSpec((1, 1, ncb, LANES), lambda b, g, i, idx=idx: (b, idx * G + g, 0, 0))
    return pl.pallas_call(
        functools.partial(_nsa_kernel, n_sel=min(SEL_TOPK, nb), n_tiles=S // Q_TILE, bounded_scores=bounded_scores),
        grid=(B, G, S // Q_TILE),
        in_specs=[
            pl.BlockSpec((1, R, Q_TILE, LANES), lambda b, g, i: (b, g, i, 0)),
            cmp_spec(0), cmp_spec(1),
            kx_spec(0), kx_spec(1),
            vt_spec(0), vt_spec(1),
            pl.BlockSpec((1, 1, 2 * SUBLANES, Q_TILE), lambda b, g, i: (b, g, 0, i)),
            pl.BlockSpec(ovt.shape, lambda b, g, i: (0, 0)),
        ],
        out_specs=pl.BlockSpec((1, Q_TILE, R * dh), lambda b, g, i: (b, i, g)),
        out_shape=jax.ShapeDtypeStruct((B, S, H * dh), BF16),
        compiler_params=_cparams(3),
        name="nsa_attend",
    )(q, cmp, cmp, kx, kx, vt, vt, gates_t, ovt)


def _hgrn_kernel(q_ref, lf_ref, v_ref, g_ref, ng_ref, tri_ref, o_ref, st_ref, *, bounded_decay):
    C, SB = HGRN_CHUNK, HGRN_SUB
    NBK = C // SB
    S = q_ref.shape[1]
    st_ref[...] = jnp.zeros(st_ref.shape, F32)
    row = lax.broadcasted_iota(jnp.int32, (C, HGRN_DK), 0)
    sub_row = lax.broadcasted_iota(jnp.int32, (SB, 1), 0)
    cr = lax.broadcasted_iota(jnp.int32, (C, C), 0)
    cc = lax.broadcasted_iota(jnp.int32, (C, C), 1)
    diag_mask = (cr // SB == cc // SB) & (cc <= cr)

    def head_chunk(h, r0):
        hs = slice(h * HGRN_DK, (h + 1) * HGRN_DK)
        q = q_ref[0, pl.ds(r0, C), hs].astype(F32)
        lf = lf_ref[0, pl.ds(r0, C), hs]
        v_bf = v_ref[0, pl.ds(r0, C), hs]
        v = v_bf.astype(F32)
        kh = 1.0 - jnp.exp(lf)
        tri = tri_ref[...]
        lf0 = lf.astype(BF16)
        lf1 = (lf - lf0.astype(F32)).astype(BF16)
        lf2 = (lf - lf0.astype(F32) - lf1.astype(F32)).astype(BF16)
        b = _dot(tri, lf0) + _dot(tri, lf1) + _dot(tri, lf2)
        yield
        b_last = b[C - 1:C, :]
        st = st_ref[h]
        o = _dot_nt((q * jnp.exp(b)).astype(BF16), st.astype(BF16))
        b_end = jnp.concatenate(
            [jnp.broadcast_to(b[(j + 1) * SB - 1:(j + 1) * SB, :], (SB, HGRN_DK)) for j in range(NBK)], axis=0)
        k_end = kh * jnp.exp(b_end - b)
        q_parts, k_parts = [], []
        for j in range(NBK - 1):
            lo = (j + 1) * SB
            qj = q[lo:] * jnp.exp(b[lo:] - b[lo - 1:lo, :])
            q_parts.append(jnp.concatenate([jnp.zeros((lo, HGRN_DK), F32), qj], axis=0))
            k_parts.append(jnp.where((row >= j * SB) & (row < lo), k_end, 0.0))
        q_cat = jnp.concatenate(q_parts, axis=1).astype(BF16)
        k_cat = jnp.concatenate(k_parts, axis=1).astype(BF16)
        a_off = _dot_nt(q_cat, k_cat)
        k_last = (kh * jnp.exp(b_last - b)).astype(BF16)
        st_ref[h] = st * jnp.exp(b_last) + _dot_tn(v_bf, k_last)
        if bounded_decay:
            b_start = jnp.concatenate([jnp.zeros((SB, HGRN_DK), F32), b_end[:C - SB]], axis=0)
            a_dg = _dot_nt((q * jnp.exp(b - b_start)).astype(BF16), (kh * jnp.exp(b_start - b)).astype(BF16))
            yield
            o = o + _dot(jnp.where(diag_mask, a_dg, a_off).astype(BF16), v_bf)
            yield
        else:
            yield
            o = o + _dot(a_off.astype(BF16), v_bf)
            diag = []
            for j in range(NBK):
                sl = slice(j * SB, (j + 1) * SB)
                qj, bj, kj, vj = q[sl], b[sl], kh[sl], v[sl]
                oj = jnp.zeros((SB, HGRN_DV), F32)
                for s in range(SB):
                    w = jnp.exp(jnp.minimum(bj - bj[s:s + 1, :], 0.0))
                    a = jnp.sum(qj * kj[s:s + 1, :] * w, axis=-1, keepdims=True)
                    a = jnp.where(sub_row >= s, a, 0.0)
                    oj = oj + a * vj[s:s + 1, :]
                diag.append(oj)
            o = o + jnp.concatenate(diag, axis=0)
            yield
        o = o * lax.rsqrt(jnp.mean(o * o, axis=-1, keepdims=True) + RMS_EPS) * ng_ref[...]
        o_ref[0, pl.ds(r0, C), hs] = (o * g_ref[0, pl.ds(r0, C), hs].astype(F32)).astype(BF16)

    def chunk(ci, carry):
        r0 = pl.multiple_of(ci * (C * HGRN_UNROLL), C * HGRN_UNROLL)
        heads = [head_chunk(h, r0 + u * C) for u in range(HGRN_UNROLL) for h in range(HGRN_HEADS)]
        for _ in range(3):
            for gen in heads:
                next(gen)
        for gen in heads:
            next(gen, None)
        return carry

    lax.fori_loop(0, S // (C * HGRN_UNROLL), chunk, 0)


def hgrn2(hq, hlf, hv, hg, norm_g, bounded_decay):
    B, S, _ = hq.shape
    C = HGRN_CHUNK
    assert S % (C * HGRN_UNROLL) == 0
    tri = jnp.asarray(np.tril(np.ones((C, C))), BF16)
    spec = lambda n: pl.BlockSpec((1, S, n), lambda b: (b, 0, 0))
    return pl.pallas_call(
        functools.partial(_hgrn_kernel, bounded_decay=bounded_decay),
        grid=(B,),
        in_specs=[spec(HGRN_WIDTH), spec(HGRN_WIDTH), spec(HGRN_VWIDTH), spec(HGRN_VWIDTH),
                  pl.BlockSpec((1, HGRN_DV), lambda b: (0, 0)),
                  pl.BlockSpec((C, C), lambda b: (0, 0))],
        out_specs=spec(HGRN_VWIDTH),
        out_shape=jax.ShapeDtypeStruct((B, S, HGRN_VWIDTH), BF16),
        scratch_shapes=[pltpu.VMEM((HGRN_HEADS, HGRN_DV, HGRN_DK), F32)],
        compiler_params=_cparams(1),
        name="hgrn2",
    )(hq, hlf, hv, hg, norm_g.reshape(1, HGRN_DV).astype(F32), tri)


def _merge_kernel(ya_ref, yb_ref, gma_ref, gmb_ref, x_ref, gt_ref, lg_ref, lbias_ref,
                  wa_ref, wb_ref, wo_ref, o_ref):
    pa = _dot(ya_ref[0], wa_ref[...])
    pb = _dot(yb_ref[0], wb_ref[...])
    merged = gma_ref[0].astype(F32) * pa + gmb_ref[0].astype(F32) * pb
    y = _dot(merged.astype(BF16), wo_ref[...])
    z = DEEPNORM_ALPHA * x_ref[0] + (1.0 + gt_ref[0]) * y
    o_ref[0] = _layer_norm(z, lg_ref[...], lbias_ref[...])


def merge_out(ya, yb, gma, gmb, x, gt, ln_g, ln_b, wa, wb, wo):
    B, S, D = x.shape
    tm = min(TOKEN_TILE, S)
    tok = lambda n: pl.BlockSpec((1, tm, n), lambda b, i: (b, i, 0))
    return pl.pallas_call(
        _merge_kernel,
        grid=(B, S // tm),
        in_specs=[tok(NSA_WIDTH), tok(HGRN_VWIDTH), tok(D), tok(D), tok(D),
                  pl.BlockSpec((1, 1, D), lambda b, i: (b, 0, 0)),
                  _resident((1, D)), _resident((1, D)),
                  _resident(wa.shape), _resident(wb.shape), _resident(wo.shape)],
        out_specs=tok(D),
        out_shape=jax.ShapeDtypeStruct((B, S, D), F32),
        compiler_params=_cparams(2),
        name="merge_out",
    )(ya, yb, gma, gmb, x, gt, ln_g.reshape(1, D), ln_b.reshape(1, D), wa, wb, wo)


MLP_COLS = 1024


def _mlp_kernel(x_ref, sc_ref, sh_ref, gt_ref, lg_ref, lbias_ref, w1_ref, w2_ref, o_ref):
    x = x_ref[0]
    u = (x * (1.0 + sc_ref[0]) + sh_ref[0]).astype(BF16)
    y = jnp.zeros(x.shape, F32)
    for c in range(MLP_HIDDEN // MLP_COLS):
        h = jnp.maximum(_dot(u, w1_ref[:, c * MLP_COLS:(c + 1) * MLP_COLS]), 0.0)
        y = y + _dot((h * h).astype(BF16), w2_ref[c * MLP_COLS:(c + 1) * MLP_COLS, :])
    z = DEEPNORM_ALPHA * x + (1.0 + gt_ref[0]) * y
    o_ref[0] = _layer_norm(z, lg_ref[...], lbias_ref[...])


def mlp(x, sc, sh, gt, ln_g, ln_b, w1, w2):
    B, S, D = x.shape
    tm = min(TOKEN_TILE, S)
    tok = pl.BlockSpec((1, tm, D), lambda b, i: (b, i, 0))
    per_b = pl.BlockSpec((1, 1, D), lambda b, i: (b, 0, 0))
    return pl.pallas_call(
        _mlp_kernel,
        grid=(B, S // tm),
        in_specs=[tok, per_b, per_b, per_b, _resident((1, D)), _resident((1, D)),
                  _resident(w1.shape), _resident(w2.shape)],
        out_specs=tok,
        out_shape=jax.ShapeDtypeStruct((B, S, D), F32),
        compiler_params=_cparams(2),
        name="mlp",
    )(x, sc, sh, gt, ln_g.reshape(1, D), ln_b.reshape(1, D), w1, w2)


def _rope_tables(S):
    inv = 1.0 / (ROPE_THETA ** (jnp.arange(0, NSA_DH, 2, dtype=F32) / NSA_DH))
    ang = jnp.arange(S, dtype=F32)[:, None] * inv[None, :]
    cos, sin = jnp.cos(ang), jnp.sin(ang)
    reps = LANES // NSA_DH
    return (jnp.tile(jnp.concatenate([cos, cos], axis=1), (1, reps)),
            jnp.tile(jnp.concatenate([-sin, sin], axis=1), (1, reps)))


def kernel(x, c, w_in, b_in, cmp_pe_k, cmp_pe_v, cmp_wk1, cmp_wk2, cmp_wv1, cmp_wv2, hgrn_lb_logits, hgrn_norm_g, w_branch_a, w_branch_b, w_out, w_ada, b_ada, ln1_g, ln1_b, w_mlp1, w_mlp2, ln2_g, ln2_b):
    B, S, D = x.shape
    G = NSA_GROUPS
    lb_all = jnp.cumsum(jax.nn.softmax(hgrn_lb_logits.astype(F32), axis=0), axis=0)
    lb_all = lb_all - lb_all[0:1]
    cos_t, sin_t = _rope_tables(S)
    mod = adaln_mod(c, w_ada, b_ada)
    for l in range(DEPTH):
        sh1, sc1, gt1, sh2, sc2, gt2 = [mod[l, :, None, i * D:(i + 1) * D] for i in range(6)]
        wts = _prep_in_proj_weights(w_in[l], b_in[l])
        q, kx, kv, vt, gates_t, nrm, hq, hlf, hv, hg, gma, gmb = in_proj(x, sc1, sh1, cos_t, sin_t, lb_all[l].reshape(1, -1), wts)
        pe = jnp.stack([cmp_pe_k[l].reshape(1, -1), cmp_pe_v[l].reshape(1, -1)])
        w1 = jnp.stack([cmp_wk1[l], cmp_wv1[l]]).astype(BF16)
        w2 = jnp.pad(jnp.stack([cmp_wk2[l], cmp_wv2[l]]), ((0, 0), (0, 0), (0, LANES - NSA_DH))).astype(BF16)
        cmp = nsa_compress(kv, pe, w1, w2)
        bounded = jnp.max(nrm[:, :, 0, 0]) * jnp.max(nrm[:, :, 1, 0]) <= SCORE_BOUND ** 2
        ya = lax.cond(bounded,
                      functools.partial(nsa_attend, bounded_scores=True),
                      functools.partial(nsa_attend, bounded_scores=False),
                      q, kx, vt, cmp, gates_t)
        yb = lax.cond(jnp.min(nrm[:, :, 2, 0]) >= -HGRN_MAX_STEP_DECAY,
                      functools.partial(hgrn2, bounded_decay=True),
                      functools.partial(hgrn2, bounded_decay=False),
                      hq, hlf, hv, hg, hgrn_norm_g[l])
        x = merge_out(ya, yb, gma, gmb, x, gt1, ln1_g[l], ln1_b[l],
                      w_branch_a[l].astype(BF16), w_branch_b[l].astype(BF16), w_out[l].astype(BF16))
        x = mlp(x, sc2, sh2, gt2, ln2_g[l], ln2_b[l], w_mlp1[l].astype(BF16), w_mlp2[l].astype(BF16))
    return x
```

```python
import functools

import numpy as np
import jax
import jax.numpy as jnp
from jax import lax
from jax.experimental import pallas as pl
from jax.experimental.pallas import tpu as pltpu

D_MODEL = 1024
DEPTH = 2
NSA_HEADS = 8
NSA_GROUPS = 2
NSA_REP = NSA_HEADS // NSA_GROUPS
NSA_DH = 64
NSA_WIDTH = NSA_HEADS * NSA_DH
NSA_KV_WIDTH = NSA_GROUPS * NSA_DH
CMP_LEN = 32
CMP_STRIDE = 16
CMP_HIDDEN = 2 * NSA_DH
SEL_LEN = 64
SEL_TOPK = 8
FORCE_SCORE = 1.0e4
WINDOW = 512
HGRN_HEADS = 4
HGRN_DK = 128
HGRN_DV = 128
HGRN_WIDTH = HGRN_HEADS * HGRN_DK
HGRN_VWIDTH = HGRN_HEADS * HGRN_DV
MLP_HIDDEN = 4 * D_MODEL
ROPE_THETA = 10000.0
LN_EPS = 1e-5
RMS_EPS = 1e-6
DEEPNORM_ALPHA = (2 * DEPTH) ** 0.25
IN_SIZES = (NSA_WIDTH,) + (NSA_KV_WIDTH,) * 6 + (3 * NSA_HEADS,) + (HGRN_WIDTH, HGRN_WIDTH, HGRN_VWIDTH, HGRN_VWIDTH) + (D_MODEL, D_MODEL)
IN_OFFSETS = [0] + [int(v) for v in np.cumsum(IN_SIZES)]

LANES = 128
SUBLANES = 8
VMEM_LIMIT = 48 * 1024 * 1024
TOKEN_TILE = 512
Q_TILE = 128
K_CHUNK = 128
ATT_BLOCK = 4
ATT_LOOKAHEAD = 2
HGRN_CHUNK = 64
HGRN_UNROLL = 4
HGRN_SUB = 8
HGRN_MAX_STEP_DECAY = 7.5
NEG_BIG = -1e30
LOG2E = 1.4426950408889634
Q_SCALE = NSA_DH ** -0.5 * LOG2E
SCORE_BOUND = 96.0
VT_ROWS = NSA_DH + 16

F32 = jnp.float32
BF16 = jnp.bfloat16


def _cparams(n_grid):
    return pltpu.CompilerParams(dimension_semantics=("arbitrary",) * n_grid, vmem_limit_bytes=VMEM_LIMIT)


def _resident(shape):
    nd = len(shape)
    return pl.BlockSpec(shape, lambda *_: (0,) * nd, pipeline_mode=pl.Buffered(1))


def _dot(a, b):
    return jnp.dot(a, b, preferred_element_type=F32)


def _dot_nt(a, b):
    return lax.dot_general(a, b, (((1,), (1,)), ((), ())), preferred_element_type=F32)


def _dot_tn(a, b):
    return lax.dot_general(a, b, (((0,), (0,)), ((), ())), preferred_element_type=F32)


def _sigmoid(x):
    return 1.0 / (1.0 + jnp.exp(-x))


def _silu(x):
    return x * _sigmoid(x)


def _layer_norm(z, g, b):
    mu = jnp.mean(z, axis=-1, keepdims=True)
    zc = z - mu
    var = jnp.mean(zc * zc, axis=-1, keepdims=True)
    return zc * lax.rsqrt(var + LN_EPS) * g + b


def _adaln_kernel(c_ref, w_ref, b_ref, o_ref):
    cond = _silu(c_ref[...]).astype(BF16)
    o_ref[0] = _dot(cond, w_ref[0]) + b_ref[0]


def adaln_mod(c, w_ada, b_ada):
    L, D, N = w_ada.shape
    B = c.shape[0]
    tn = D
    return pl.pallas_call(
        _adaln_kernel,
        grid=(L, N // tn),
        in_specs=[
            pl.BlockSpec((B, D), lambda l, j: (0, 0)),
            pl.BlockSpec((1, D, tn), lambda l, j: (l, 0, j)),
            pl.BlockSpec((1, 1, tn), lambda l, j: (l, 0, j)),
        ],
        out_specs=pl.BlockSpec((1, B, tn), lambda l, j: (l, 0, j)),
        out_shape=jax.ShapeDtypeStruct((L, B, N), F32),
        compiler_params=_cparams(2),
        name="adaln_mod",
    )(c, w_ada.astype(BF16), b_ada.reshape(L, 1, N))


N_ROPE = NSA_WIDTH + 3 * NSA_KV_WIDTH
N_NSA = N_ROPE + 3 * NSA_KV_WIDTH
N_GATE = NSA_GROUPS * LANES
N_HGRN = 2 * HGRN_WIDTH + 2 * HGRN_VWIDTH
N_MERGE = 2 * D_MODEL


def _in_proj_kernel(x_ref, sc_ref, sh_ref, cos_ref, sin_ref, lb_ref,
                    wn_ref, bn_ref, wg_ref, bg_ref, wh_ref, bh_ref, wm_ref, bm_ref,
                    q_ref, kx_ref, kv_ref, vt_ref, ga_ref, nrm_ref, hq_ref, hlf_ref, hv_ref, hg_ref, gma_ref, gmb_ref):
    u = (x_ref[0] * (1.0 + sc_ref[0]) + sh_ref[0]).astype(BF16)
    cos = cos_ref[...]
    sin = sin_ref[...]
    lane = lax.broadcasted_iota(jnp.int32, cos.shape, 1)
    first_half = (lane % NSA_DH) < (NSA_DH // 2)
    low = lane < NSA_DH

    def heads(t, upper):
        return jnp.where(low, t, upper), jnp.where(low, pltpu.roll(t, NSA_DH, 1), upper)

    n_q, n_rope = NSA_HEADS // 2, N_ROPE // LANES
    tm = u.shape[0]
    ones_rows = jnp.where(lax.broadcasted_iota(jnp.int32, (VT_ROWS - NSA_DH, tm), 0) == 0, 1.0, 0.0).astype(BF16)

    def max_sq_norm(t, acc):
        n = jnp.max(jnp.sum(t * t, axis=1, keepdims=True), axis=0, keepdims=True)
        return n if acc is None else jnp.maximum(acc, n)

    q_sq = k_sq = None
    for i2 in range(0, N_NSA // LANES, 2):
        t2 = _dot(u, wn_ref[:, i2 * LANES:(i2 + 2) * LANES]) + bn_ref[:, i2 * LANES:(i2 + 2) * LANES]
        for i in (i2, i2 + 1):
            t = t2[:, (i - i2) * LANES:(i - i2 + 1) * LANES]
            if i < n_rope:
                rot = jnp.where(first_half, pltpu.roll(t, LANES - NSA_DH // 2, 1), pltpu.roll(t, NSA_DH // 2, 1))
                t = t * cos + rot * sin
            if i < n_q:
                t = t * Q_SCALE
                q_sq = max_sq_norm(t, q_sq)
                for j, piece in enumerate(heads(t, 0.0)):
                    q_ref[0, 2 * i + j] = piece.astype(BF16)
            elif i == n_q or i == n_rope:
                for j in range(2):
                    kv_ref[0, (2 if i == n_rope else 0) + j] = t[:, j * NSA_DH:(j + 1) * NSA_DH].astype(BF16)
            elif i < n_rope:
                kind = i - n_q - 1
                k_sq = max_sq_norm(t, k_sq)
                for j, piece in enumerate(heads(t, 0.0)):
                    kx_ref[0, 2 * kind + j] = piece.astype(BF16)
            else:
                kind = i - n_rope - 1
                tt = t.T.astype(BF16)
                for j in range(2):
                    vt_ref[0, 2 * kind + j] = jnp.concatenate([tt[j * NSA_DH:(j + 1) * NSA_DH], ones_rows], axis=0)
    gates = _sigmoid(_dot(u, wg_ref[...]) + bg_ref[...])
    for g in range(NSA_GROUPS):
        ga_ref[0, g] = gates[:, g * LANES:(g + 1) * LANES].T[0:2 * SUBLANES]
    W = HGRN_WIDTH
    hq = _dot(u, wh_ref[:, 0:W]) + bh_ref[:, 0:W]
    hq_ref[0] = (_silu(hq) * (HGRN_DK ** -0.5)).astype(BF16)
    z = _dot(u, wh_ref[:, W:2 * W]) + bh_ref[:, W:2 * W]
    lb = lb_ref[...]
    log_sig = jnp.minimum(z, 0.0) - jnp.log1p(jnp.exp(-jnp.abs(z)))
    a = jnp.log(lb)
    bb = jnp.log1p(-lb) + log_sig
    log_f = jnp.maximum(a, bb) + jnp.log1p(jnp.exp(-jnp.abs(a - bb)))
    hlf_ref[0] = log_f
    lf_min = jnp.min(jnp.min(log_f, axis=1, keepdims=True), axis=0, keepdims=True)
    srow = lax.broadcasted_iota(jnp.int32, (SUBLANES, LANES), 0)
    nrm_ref[0, 0] = jnp.where(srow == 0, q_sq, jnp.where(srow == 1, k_sq, jnp.where(srow == 2, lf_min, 0.0)))
    hv_ref[0] = (_dot(u, wh_ref[:, 2 * W:3 * W]) + bh_ref[:, 2 * W:3 * W]).astype(BF16)
    hg_ref[0] = _silu(_dot(u, wh_ref[:, 3 * W:4 * W]) + bh_ref[:, 3 * W:4 * W]).astype(BF16)
    for i, ref in enumerate((gma_ref, gmb_ref)):
        for j in range(2):
            c0 = i * D_MODEL + j * (D_MODEL // 2)
            c1 = c0 + D_MODEL // 2
            ref[0, :, j * (D_MODEL // 2):(j + 1) * (D_MODEL // 2)] = _sigmoid(
                _dot(u, wm_ref[:, c0:c1]) + bm_ref[:, c0:c1]).astype(BF16)


def in_proj(x, sc, sh, cos_t, sin_t, lb, wts):
    B, S, D = x.shape
    tm = min(TOKEN_TILE, S)
    assert S // SEL_LEN <= LANES - NSA_DH
    wn, bn, wg, bg, wh, bh, wm, bm = wts
    tok = lambda n: pl.BlockSpec((1, tm, n), lambda b, i: (b, i, 0))
    per_b = pl.BlockSpec((1, 1, D), lambda b, i: (b, 0, 0))
    tab = pl.BlockSpec((tm, LANES), lambda b, i: (i, 0))
    out_shape = (
        jax.ShapeDtypeStruct((B, NSA_HEADS, S, LANES), BF16),
        jax.ShapeDtypeStruct((B, 2 * NSA_GROUPS, S, LANES), BF16),
        jax.ShapeDtypeStruct((B, 2 * NSA_GROUPS, S, NSA_DH), BF16),
        jax.ShapeDtypeStruct((B, 2 * NSA_GROUPS, VT_ROWS, S), BF16),
        jax.ShapeDtypeStruct((B, NSA_GROUPS, 2 * SUBLANES, S), F32),
        jax.ShapeDtypeStruct((B, S // tm, SUBLANES, LANES), F32),
        jax.ShapeDtypeStruct((B, S, HGRN_WIDTH), BF16),
        jax.ShapeDtypeStruct((B, S, HGRN_WIDTH), F32),
        jax.ShapeDtypeStruct((B, S, HGRN_VWIDTH), BF16),
        jax.ShapeDtypeStruct((B, S, HGRN_VWIDTH), BF16),
        jax.ShapeDtypeStruct((B, S, D), BF16),
        jax.ShapeDtypeStruct((B, S, D), BF16),
    )
    out_specs = (
        pl.BlockSpec((1, NSA_HEADS, tm, LANES), lambda b, i: (b, 0, i, 0)),
        pl.BlockSpec((1, 2 * NSA_GROUPS, tm, LANES), lambda b, i: (b, 0, i, 0)),
        pl.BlockSpec((1, 2 * NSA_GROUPS, tm, NSA_DH), lambda b, i: (b, 0, i, 0)),
        pl.BlockSpec((1, 2 * NSA_GROUPS, VT_ROWS, tm), lambda b, i: (b, 0, 0, i)),
        pl.BlockSpec((1, NSA_GROUPS, 2 * SUBLANES, tm), lambda b, i: (b, 0, 0, i)),
        pl.BlockSpec((1, 1, SUBLANES, LANES), lambda b, i: (b, i, 0, 0)),
        tok(HGRN_WIDTH), tok(HGRN_WIDTH), tok(HGRN_VWIDTH), tok(HGRN_VWIDTH), tok(D), tok(D),
    )
    return pl.pallas_call(
        _in_proj_kernel,
        grid=(B, S // tm),
        in_specs=[tok(D), per_b, per_b, tab, tab, _resident(lb.shape),
                  _resident(wn.shape), _resident(bn.shape), _resident(wg.shape), _resident(bg.shape),
                  _resident(wh.shape), _resident(bh.shape), _resident(wm.shape), _resident(bm.shape)],
        out_specs=out_specs,
        out_shape=out_shape,
        compiler_params=_cparams(2),
        name="in_proj",
    )(x, sc, sh, cos_t, sin_t, lb, wn, bn, wg, bg, wh, bh, wm, bm)


def _prep_in_proj_weights(w_in_l, b_in_l):
    o = IN_OFFSETS
    col = lambda i: (w_in_l[:, o[i]:o[i + 1]], b_in_l[o[i]:o[i + 1]])
    q_a, k_c, v_c, k_s, v_s, k_w, v_w, g_a, q_b, f_b, i_b, g_b, gm_a, gm_b = [col(i) for i in range(14)]

    def cat(parts):
        return (jnp.concatenate([p[0] for p in parts], axis=1).astype(BF16),
                jnp.concatenate([p[1] for p in parts], axis=0).reshape(1, -1).astype(F32))

    wn, bn = cat([q_a, k_c, k_s, k_w, v_c, v_s, v_w])
    per_group = 3 * NSA_REP
    gw = jnp.zeros((w_in_l.shape[0], N_GATE), w_in_l.dtype)
    gb = jnp.zeros((N_GATE,), b_in_l.dtype)
    for g in range(NSA_GROUPS):
        gw = gw.at[:, g * LANES:g * LANES + per_group].set(g_a[0][:, g * per_group:(g + 1) * per_group])
        gb = gb.at[g * LANES:g * LANES + per_group].set(g_a[1][g * per_group:(g + 1) * per_group])
    wg, bg = gw.astype(BF16), gb.reshape(1, -1).astype(F32)
    wh, bh = cat([q_b, f_b, i_b, g_b])
    wm, bm = cat([gm_a, gm_b])
    return wn, bn, wg, bg, wh, bh, wm, bm


def _compress_kernel(t_ref, pe_ref, w1_ref, w2_ref, o_ref):
    half = CMP_STRIDE * NSA_DH
    t = t_ref[0, 0]
    nrow = t.shape[0]
    a = _dot(t, w1_ref[0, 0:half, :])
    b = _dot(t, w1_ref[0, half:2 * half, :])
    pe = jnp.broadcast_to(pe_ref[0], (8, 2 * half)).astype(BF16)
    c = _dot(pe, w1_ref[0])[0:1]
    h = a + pltpu.roll(b, nrow - 1, 0) + c
    o_ref[0, 0] = _dot(_silu(h).astype(BF16), w2_ref[0]).astype(BF16)


def nsa_compress(kv, pe, w1, w2):
    B, _, S, dh = kv.shape
    nrow = S // CMP_STRIDE
    G = NSA_GROUPS
    kv_rows = kv.reshape(B, 2 * G, nrow, CMP_STRIDE * dh)
    return pl.pallas_call(
        _compress_kernel,
        grid=(B, 2, G),
        in_specs=[
            pl.BlockSpec((1, 1, nrow, CMP_STRIDE * dh), lambda b, s, g: (b, s * G + g, 0, 0)),
            pl.BlockSpec((1, 1, CMP_LEN * dh), lambda b, s, g: (s, 0, 0)),
            pl.BlockSpec((1, CMP_LEN * dh, CMP_HIDDEN), lambda b, s, g: (s, 0, 0)),
            pl.BlockSpec((1, CMP_HIDDEN, LANES), lambda b, s, g: (s, 0, 0)),
        ],
        out_specs=pl.BlockSpec((1, 1, nrow, LANES), lambda b, s, g: (b, s * G + g, 0, 0)),
        out_shape=jax.ShapeDtypeStruct((B, 2 * G, nrow, LANES), BF16),
        compiler_params=_cparams(3),
        name="nsa_compress",
    )(kv_rows, pe, w1, w2)


def _nsa_kernel(q_ref, cmp_ref, kx_ref, vt_ref, gt_ref, ovt_ref, o_ref, *, n_sel, n_tiles, bounded_scores):
    def tile(qs):
        gens = [_nsa_tile(qs, g, q_ref, cmp_ref, kx_ref, vt_ref, gt_ref, ovt_ref, o_ref, n_sel, bounded_scores)
                for g in range(NSA_GROUPS)]
        while gens:
            gens = [gen for gen in gens if next(gen, "done") != "done"]

    for qs in range(n_tiles):
        pl.when(pl.program_id(1) == qs)(functools.partial(tile, qs))


def _round_up(x, m):
    return -(-x // m) * m


def _nsa_tile(qb, g, q_ref, cmp_ref, kx_ref, vt_ref, gt_ref, ovt_ref, o_ref, n_sel, bounded_scores):
    G, R, TQ, dh = NSA_GROUPS, NSA_REP, Q_TILE, NSA_DH
    cols = R * TQ
    s0 = qb * TQ
    kc_ref, vc_ref = cmp_ref.at[0, g], cmp_ref.at[0, G + g]
    ks_ref, kw_ref = kx_ref.at[0, g], kx_ref.at[0, G + g]
    vst_ref, vwt_ref = vt_ref.at[0, g], vt_ref.at[0, G + g]
    q = q_ref[0, g * R:(g + 1) * R].reshape(cols, LANES)
    t_lane = s0 + (lax.broadcasted_iota(jnp.int32, (1, cols), 1) % TQ)
    kidx = lax.broadcasted_iota(jnp.int32, (K_CHUNK, cols), 0)
    tq = lax.broadcasted_iota(jnp.int32, (K_CHUNK, cols), 1) % TQ
    causal = kidx <= tq

    def branch_blocks(k_ref, vt_ref, c0, n, band_first):
        return [(k_ref, vt_ref, c0 + b0, min(ATT_BLOCK, n - b0), band_first and b0 == 0, b0 + ATT_BLOCK >= n)
                for b0 in range(0, n, ATT_BLOCK)]

    def score_block(blk):
        k_ref, _, c, n, band, diag = blk
        s = _dot_nt(k_ref[c * K_CHUNK:(c + n) * K_CHUNK, :], q)
        parts = [s[i * K_CHUNK:(i + 1) * K_CHUNK] for i in range(n)]
        if band:
            parts[0] = jnp.where(tq < kidx, parts[0], NEG_BIG)
        if diag:
            parts[-1] = jnp.where(causal, parts[-1], NEG_BIG)
        return jnp.concatenate(parts, axis=0) if n > 1 else parts[0]

    def finish_block(s, blk, block_bias=None):
        _, vt_ref, c, n, _, _ = blk
        nk = n * K_CHUNK
        m = None
        if block_bias is not None:
            j0 = c * K_CHUNK // SEL_LEN
            subs = [s[i * SEL_LEN:(i + 1) * SEL_LEN] for i in range(nk // SEL_LEN)]
            bias = [block_bias[j0 + i:j0 + i + 1, :] for i in range(nk // SEL_LEN)]
        if bounded_scores:
            p = jnp.exp2(s) if block_bias is None else jnp.concatenate(
                [jnp.exp2(s_i + b_i) for s_i, b_i in zip(subs, bias)], axis=0)
        elif block_bias is None:
            m = jnp.max(s, axis=0, keepdims=True)
            p = jnp.exp2(s - m)
        else:
            for s_i, b_i in zip(subs, bias):
                m_i = jnp.max(s_i, axis=0, keepdims=True) + b_i
                m = m_i if m is None else jnp.maximum(m, m_i)
            shift = jnp.where(m < 0.5 * NEG_BIG, 0.0, m)
            p = jnp.concatenate([jnp.exp2(s_i + (b_i - shift)) for s_i, b_i in zip(subs, bias)], axis=0)
        vt = vt_ref[:, c * K_CHUNK:c * K_CHUNK + nk]
        return m, _dot(vt, p.astype(BF16))

    def combine(stats):
        total = stats[0][1]
        if bounded_scores:
            for _, acc_i in stats[1:]:
                total = total + acc_i
        elif len(stats) > 1:
            m = stats[0][0]
            for st in stats[1:]:
                m = jnp.maximum(m, st[0])
            total = None
            for m_i, acc_i in stats:
                w = jnp.exp2(m_i - m)
                total = w * acc_i if total is None else total + w * acc_i
        return total[0:dh] * (1.0 / total[dh:dh + 1])

    n_win = WINDOW // K_CHUNK
    win_blocks = branch_blocks(kw_ref, vwt_ref, max(qb - n_win, 0), min(qb, n_win) + 1, qb >= n_win)
    blocks = win_blocks + branch_blocks(ks_ref, vst_ref, 0, qb + 1, False)

    ncb = min(kc_ref.shape[0], _round_up((s0 + TQ - CMP_LEN) // CMP_STRIDE + 1, 2 * SUBLANES))
    sc = _dot_nt(kc_ref[0:ncb, :], q)
    pending = {i: score_block(blocks[i]) for i in range(min(ATT_LOOKAHEAD, len(blocks)))}
    yield
    n_sub = lax.broadcasted_iota(jnp.int32, (ncb, cols), 0)
    mask_c = n_sub * CMP_STRIDE + (CMP_LEN - 1) <= t_lane
    sc = jnp.where(mask_c, sc, NEG_BIG)
    mc = jnp.max(sc, axis=0, keepdims=True)
    ec = jnp.where(mask_c, jnp.exp2(sc - mc), 0.0)
    pc = ec * (1.0 / jnp.maximum(jnp.sum(ec, axis=0, keepdims=True), 1e-30))
    o_c = _dot_tn(vc_ref[0:ncb, :], pc.astype(BF16))[0:dh]

    nb_live = (s0 + TQ) // SEL_LEN
    nb = min(ovt_ref.shape[0], _round_up(nb_live, 2 * SUBLANES))
    psum = pc[:, 0:TQ]
    for r in range(1, R):
        psum = psum + pc[:, r * TQ:(r + 1) * TQ]
    p_hi = psum.astype(BF16)
    p_lo = (psum - p_hi.astype(F32)).astype(BF16)
    ovt = ovt_ref[0:nb, 0:ncb]
    imp = _dot(ovt, p_hi) + _dot(ovt, p_lo)
    yield
    jb =lax.broadcasted_iota(jnp.int32, (nb, TQ), 0)
    tb = (s0 + lax.broadcasted_iota(jnp.int32, (nb, TQ), 1)) // SEL_LEN
    valid = jb <= tb
    forced = jnp.where(valid, jnp.where(jb == 0, 1.0, jnp.where(jb >= tb - 1, 1.0, 0.0)), 0.0)
    score = jnp.where(forced > 0.5, FORCE_SCORE, jnp.where(valid, imp, -1.0))
    rank = jnp.zeros((nb, TQ), F32)
    for i in range(nb_live):
        si = score[i:i + 1, :]
        tie_first = jnp.where(jb > i, 1.0, 0.0)
        rank = rank + jnp.where(si > score, 1.0, jnp.where(si == score, tie_first, 0.0))
    sel_bias = jnp.where(rank < n_sel, 0.0, NEG_BIG)
    sel_bias = jnp.concatenate([sel_bias] * R, axis=1)

    stats = []
    for i, blk in enumerate(blocks):
        is_sel = i >= len(win_blocks)
        stats.append(finish_block(pending.pop(i), blk, sel_bias if is_sel else None))
        if i + ATT_LOOKAHEAD < len(blocks):
            pending[i + ATT_LOOKAHEAD] = score_block(blocks[i + ATT_LOOKAHEAD])
        yield
    o_w = combine(stats[:len(win_blocks)])
    o_s = combine(stats[len(win_blocks):])

    gate = gt_ref[0, g]
    pieces = []
    for r in range(R):
        sl = slice(r * TQ, (r + 1) * TQ)
        o_r = (gate[3 * r:3 * r + 1, :] * o_c[:, sl] + gate[3 * r + 1:3 * r + 2, :] * o_s[:, sl]
               + gate[3 * r + 2:3 * r + 3, :] * o_w[:, sl])
        pieces.append(o_r.T)
    o_ref[0, :, g * R * dh:(g + 1) * R * dh] = jnp.concatenate(pieces, axis=1).astype(BF16)


def nsa_attend(q, kx, vt, cmp, gates_t, bounded_scores):
    B, H, S, _ = q.shape
    G, R, dh = NSA_GROUPS, NSA_REP, NSA_DH
    ncb = S // CMP_STRIDE
    nb = S // SEL_LEN
    assert (S % Q_TILE == 0 and Q_TILE == K_CHUNK and WINDOW % K_CHUNK == 0 and K_CHUNK % SEL_LEN == 0
            and 3 * R <= 2 * SUBLANES)
    cstart = np.arange(ncb) * CMP_STRIDE
    sstart = np.arange(nb) * SEL_LEN
    overlap = ((cstart[:, None] < sstart[None, :] + SEL_LEN) & (cstart[:, None] + CMP_LEN > sstart[None, :]))
    ovt = jnp.asarray(overlap.T, BF16)
    per_batch = lambda a: pl.BlockSpec((1,) + a.shape[1:], lambda b, i: (b, 0, 0, 0))
    return pl.pallas_call(
        functools.partial(_nsa_kernel, n_sel=min(SEL_TOPK, nb), n_tiles=S // Q_TILE, bounded_scores=bounded_scores),
        grid=(B, S // Q_TILE),
        in_specs=[
            pl.BlockSpec((1, H, Q_TILE, LANES), lambda b, i: (b, 0, i, 0)),
            per_batch(cmp), per_batch(kx), per_batch(vt),
            pl.BlockSpec((1, G, 2 * SUBLANES, Q_TILE), lambda b, i: (b, 0, 0, i)),
            pl.BlockSpec(ovt.shape, lambda b, i: (0, 0)),
        ],
        out_specs=pl.BlockSpec((1, Q_TILE, H * dh), lambda b, i: (b, i, 0)),
        out_shape=jax.ShapeDtypeStruct((B, S, H * dh), BF16),
        compiler_params=_cparams(2),
        name="nsa_attend",
    )(q, cmp, kx, vt, gates_t, ovt)


def _hgrn_kernel(q_ref, lf_ref, v_ref, g_ref, ng_ref, tri_ref, o_ref, st_ref, *, bounded_decay):
    C, SB = HGRN_CHUNK, HGRN_SUB
    NBK = C // SB
    S = q_ref.shape[1]
    st_ref[...] = jnp.zeros(st_ref.shape, F32)
    row = lax.broadcasted_iota(jnp.int32, (C, HGRN_DK), 0)
    sub_row = lax.broadcasted_iota(jnp.int32, (SB, 1), 0)
    cr = lax.broadcasted_iota(jnp.int32, (C, C), 0)
    cc = lax.broadcasted_iota(jnp.int32, (C, C), 1)
    diag_mask = (cr // SB == cc // SB) & (cc <= cr)

    def head_chunk(h, r0):
        hs = slice(h * HGRN_DK, (h + 1) * HGRN_DK)
        q = q_ref[0, pl.ds(r0, C), hs].astype(F32)
        lf = lf_ref[0, pl.ds(r0, C), hs]
        v_bf = v_ref[0, pl.ds(r0, C), hs]
        v = v_bf.astype(F32)
        kh = 1.0 - jnp.exp(lf)
        tri = tri_ref[...]
        lf0 = lf.astype(BF16)
        lf1 = (lf - lf0.astype(F32)).astype(BF16)
        lf2 = (lf - lf0.astype(F32) - lf1.astype(F32)).astype(BF16)
        b = _dot(tri, lf0) + _dot(tri, lf1) + _dot(tri, lf2)
        yield
        b_last = b[C - 1:C, :]
        st = st_ref[h]
        o = _dot_nt((q * jnp.exp(b)).astype(BF16), st.astype(BF16))
        b_end = jnp.concatenate(
            [jnp.broadcast_to(b[(j + 1) * SB - 1:(j + 1) * SB, :], (SB, HGRN_DK)) for j in range(NBK)], axis=0)
        k_end = kh * jnp.exp(b_end - b)
        q_parts, k_parts = [], []
        for j in range(NBK - 1):
            lo = (j + 1) * SB
            qj = q[lo:] * jnp.exp(b[lo:] - b[lo - 1:lo, :])
            q_parts.append(jnp.concatenate([jnp.zeros((lo, HGRN_DK), F32), qj], axis=0))
            k_parts.append(jnp.where((row >= j * SB) & (row < lo), k_end, 0.0))
        q_cat = jnp.concatenate(q_parts, axis=1).astype(BF16)
        k_cat = jnp.concatenate(k_parts, axis=1).astype(BF16)
        a_off = _dot_nt(q_cat, k_cat)
        k_last = (kh * jnp.exp(b_last - b)).astype(BF16)
        st_ref[h] = st * jnp.exp(b_last) + _dot_tn(v_bf, k_last)
        if bounded_decay:
            b_start = jnp.concatenate([jnp.zeros((SB, HGRN_DK), F32), b_end[:C - SB]], axis=0)
            a_dg = _dot_nt((q * jnp.exp(b - b_start)).astype(BF16), (kh * jnp.exp(b_start - b)).astype(BF16))
            yield
            o = o + _dot(jnp.where(diag_mask, a_dg, a_off).astype(BF16), v_bf)
            yield
        else:
            yield
            o = o + _dot(a_off.astype(BF16), v_bf)
            diag = []
            for j in range(NBK):
                sl = slice(j * SB, (j + 1) * SB)
                qj, bj, kj, vj = q[sl], b[sl], kh[sl], v[sl]
                oj = jnp.zeros((SB, HGRN_DV), F32)
                for s in range(SB):
                    w = jnp.exp(jnp.minimum(bj - bj[s:s + 1, :], 0.0))
                    a = jnp.sum(qj * kj[s:s + 1, :] * w, axis=-1, keepdims=True)
                    a = jnp.where(sub_row >= s, a, 0.0)
                    oj = oj + a * vj[s:s + 1, :]
                diag.append(oj)
            o = o + jnp.concatenate(diag, axis=0)
            yield
        o = o * lax.rsqrt(jnp.mean(o * o, axis=-1, keepdims=True) + RMS_EPS) * ng_ref[...]
        o_ref[0, pl.ds(r0, C), hs] = (o * g_ref[0, pl.ds(r0, C), hs].astype(F32)).astype(BF16)

    def chunk(ci, carry):
        r0 = pl.multiple_of(ci * (C * HGRN_UNROLL), C * HGRN_UNROLL)
        heads = [head_chunk(h, r0 + u * C) for u in range(HGRN_UNROLL) for h in range(HGRN_HEADS)]
        for _ in range(3):
            for gen in heads:
                next(gen)
        for gen in heads:
            next(gen, None)
        return carry

    lax.fori_loop(0, S // (C * HGRN_UNROLL), chunk, 0)


def hgrn2(hq, hlf, hv, hg, norm_g, bounded_decay):
    B, S, _ = hq.shape
    C = HGRN_CHUNK
    assert S % (C * HGRN_UNROLL) == 0
    tri = jnp.asarray(np.tril(np.ones((C, C))), BF16)
    spec = lambda n: pl.BlockSpec((1, S, n), lambda b: (b, 0, 0))
    return pl.pallas_call(
        functools.partial(_hgrn_kernel, bounded_decay=bounded_decay),
        grid=(B,),
        in_specs=[spec(HGRN_WIDTH), spec(HGRN_WIDTH), spec(HGRN_VWIDTH), spec(HGRN_VWIDTH),
                  pl.BlockSpec((1, HGRN_DV), lambda b: (0, 0)),
                  pl.BlockSpec((C, C), lambda b: (0, 0))],
        out_specs=spec(HGRN_VWIDTH),
        out_shape=jax.ShapeDtypeStruct((B, S, HGRN_VWIDTH), BF16),
        scratch_shapes=[pltpu.VMEM((HGRN_HEADS, HGRN_DV, HGRN_DK), F32)],
        compiler_params=_cparams(1),
        name="hgrn2",
    )(hq, hlf, hv, hg, norm_g.reshape(1, HGRN_DV).astype(F32), tri)


def _merge_kernel(ya_ref, yb_ref, gma_ref, gmb_ref, x_ref, gt_ref, lg_ref, lbias_ref,
                  wa_ref, wb_ref, wo_ref, o_ref):
    pa = _dot(ya_ref[0], wa_ref[...])
    pb = _dot(yb_ref[0], wb_ref[...])
    merged = gma_ref[0].astype(F32) * pa + gmb_ref[0].astype(F32) * pb
    y = _dot(merged.astype(BF16), wo_ref[...])
    z = DEEPNORM_ALPHA * x_ref[0] + (1.0 + gt_ref[0]) * y
    o_ref[0] = _layer_norm(z, lg_ref[...], lbias_ref[...])


def merge_out(ya, yb, gma, gmb, x, gt, ln_g, ln_b, wa, wb, wo):
    B, S, D = x.shape
    tm = min(TOKEN_TILE, S)
    tok = lambda n: pl.BlockSpec((1, tm, n), lambda b, i: (b, i, 0))
    return pl.pallas_call(
        _merge_kernel,
        grid=(B, S // tm),
        in_specs=[tok(NSA_WIDTH), tok(HGRN_VWIDTH), tok(D), tok(D), tok(D),
                  pl.BlockSpec((1, 1, D), lambda b, i: (b, 0, 0)),
                  _resident((1, D)), _resident((1, D)),
                  _resident(wa.shape), _resident(wb.shape), _resident(wo.shape)],
        out_specs=tok(D),
        out_shape=jax.ShapeDtypeStruct((B, S, D), F32),
        compiler_params=_cparams(2),
        name="merge_out",
    )(ya, yb, gma, gmb, x, gt, ln_g.reshape(1, D), ln_b.reshape(1, D), wa, wb, wo)


MLP_COLS = 1024


def _mlp_kernel(x_ref, sc_ref, sh_ref, gt_ref, lg_ref, lbias_ref, w1_ref, w2_ref, o_ref):
    x = x_ref[0]
    u = (x * (1.0 + sc_ref[0]) + sh_ref[0]).astype(BF16)
    y = jnp.zeros(x.shape, F32)
    for c in range(MLP_HIDDEN // MLP_COLS):
        h = jnp.maximum(_dot(u, w1_ref[:, c * MLP_COLS:(c + 1) * MLP_COLS]), 0.0)
        y = y + _dot((h * h).astype(BF16), w2_ref[c * MLP_COLS:(c + 1) * MLP_COLS, :])
    z = DEEPNORM_ALPHA * x + (1.0 + gt_ref[0]) * y
    o_ref[0] = _layer_norm(z, lg_ref[...], lbias_ref[...])


def mlp(x, sc, sh, gt, ln_g, ln_b, w1, w2):
    B, S, D = x.shape
    tm = min(TOKEN_TILE, S)
    tok = pl.BlockSpec((1, tm, D), lambda b, i: (b, i, 0))
    per_b = pl.BlockSpec((1, 1, D), lambda b, i: (b, 0, 0))
    return pl.pallas_call(
        _mlp_kernel,
        grid=(B, S // tm),
        in_specs=[tok, per_b, per_b, per_b, _resident((1, D)), _resident((1, D)),
                  _resident(w1.shape), _resident(w2.shape)],
        out_specs=tok,
        out_shape=jax.ShapeDtypeStruct((B, S, D), F32),
        compiler_params=_cparams(2),
        name="mlp",
    )(x, sc, sh, gt, ln_g.reshape(1, D), ln_b.reshape(1, D), w1, w2)


def _rope_tables(S):
    inv = 1.0 / (ROPE_THETA ** (jnp.arange(0, NSA_DH, 2, dtype=F32) / NSA_DH))
    ang = jnp.arange(S, dtype=F32)[:, None] * inv[None, :]
    cos, sin = jnp.cos(ang), jnp.sin(ang)
    reps = LANES // NSA_DH
    return (jnp.tile(jnp.concatenate([cos, cos], axis=1), (1, reps)),
            jnp.tile(jnp.concatenate([-sin, sin], axis=1), (1, reps)))


def kernel(x, c, w_in, b_in, cmp_pe_k, cmp_pe_v, cmp_wk1, cmp_wk2, cmp_wv1, cmp_wv2, hgrn_lb_logits, hgrn_norm_g, w_branch_a, w_branch_b, w_out, w_ada, b_ada, ln1_g, ln1_b, w_mlp1, w_mlp2, ln2_g, ln2_b):
    B, S, D = x.shape
    G = NSA_GROUPS
    lb_all = jnp.cumsum(jax.nn.softmax(hgrn_lb_logits.astype(F32), axis=0), axis=0)
    lb_all = lb_all - lb_all[0:1]
    cos_t, sin_t = _rope_tables(S)
    mod = adaln_mod(c, w_ada, b_ada)
    for l in range(DEPTH):
        sh1, sc1, gt1, sh2, sc2, gt2 = [mod[l, :, None, i * D:(i + 1) * D] for i in range(6)]
        wts = _prep_in_proj_weights(w_in[l], b_in[l])
        q, kx, kv, vt, gates_t, nrm, hq, hlf, hv, hg, gma, gmb = in_proj(x, sc1, sh1, cos_t, sin_t, lb_all[l].reshape(1, -1), wts)
        pe = jnp.stack([cmp_pe_k[l].reshape(1, -1), cmp_pe_v[l].reshape(1, -1)])
        w1 = jnp.stack([cmp_wk1[l], cmp_wv1[l]]).astype(BF16)
        w2 = jnp.pad(jnp.stack([cmp_wk2[l], cmp_wv2[l]]), ((0, 0), (0, 0), (0, LANES - NSA_DH))).astype(BF16)
        cmp = nsa_compress(kv, pe, w1, w2)
        bounded = jnp.max(nrm[:, :, 0, 0]) * jnp.max(nrm[:, :, 1, 0]) <= SCORE_BOUND ** 2
        ya = lax.cond(bounded,
                      functools.partial(nsa_attend, bounded_scores=True),
                      functools.partial(nsa_attend, bounded_scores=False),
                      q, kx, vt, cmp, gates_t)
        yb = lax.cond(jnp.min(nrm[:, :, 2, 0]) >= -HGRN_MAX_STEP_DECAY,
                      functools.partial(hgrn2, bounded_decay=True),
                      functools.partial(hgrn2, bounded_decay=False),
                      hq, hlf, hv, hg, hgrn_norm_g[l])
        x = merge_out(ya, yb, gma, gmb, x, gt1, ln1_g[l], ln1_b[l],
                      w_branch_a[l].astype(BF16), w_branch_b[l].astype(BF16), w_out[l].astype(BF16))
        x = mlp(x, sc2, sh2, gt2, ln2_g[l], ln2_b[l], w_mlp1[l].astype(BF16), w_mlp2[l].astype(BF16))
    return x
```

```python
import functools

import numpy as np
import jax
import jax.numpy as jnp
from jax import lax
from jax.experimental import pallas as pl
from jax.experimental.pallas import tpu as pltpu

D_MODEL = 1024
DEPTH = 2
NSA_HEADS = 8
NSA_GROUPS = 2
NSA_REP = NSA_HEADS // NSA_GROUPS
NSA_DH = 64
NSA_WIDTH = NSA_HEADS * NSA_DH
NSA_KV_WIDTH = NSA_GROUPS * NSA_DH
CMP_LEN = 32
CMP_STRIDE = 16
CMP_HIDDEN = 2 * NSA_DH
SEL_LEN = 64
SEL_TOPK = 8
FORCE_SCORE = 1.0e4
WINDOW = 512
HGRN_HEADS = 4
HGRN_DK = 128
HGRN_DV = 128
HGRN_WIDTH = HGRN_HEADS * HGRN_DK
HGRN_VWIDTH = HGRN_HEADS * HGRN_DV
MLP_HIDDEN = 4 * D_MODEL
ROPE_THETA = 10000.0
LN_EPS = 1e-5
RMS_EPS = 1e-6
DEEPNORM_ALPHA = (2 * DEPTH) ** 0.25
IN_SIZES = (NSA_WIDTH,) + (NSA_KV_WIDTH,) * 6 + (3 * NSA_HEADS,) + (HGRN_WIDTH, HGRN_WIDTH, HGRN_VWIDTH, HGRN_VWIDTH) + (D_MODEL, D_MODEL)
IN_OFFSETS = [0] + [int(v) for v in np.cumsum(IN_SIZES)]

LANES = 128
SUBLANES = 8
VMEM_LIMIT = 48 * 1024 * 1024
MERGE_MLP_VMEM = 56 * 1024 * 1024
TOKEN_TILE = 512
Q_TILE = 128
K_CHUNK = 128
ATT_BLOCK = 4
NSA_BATCH = 1
ATT_LOOKAHEAD = 2
HGRN_CHUNK = 64
HGRN_UNROLL = 4
HGRN_SUB = 8
HGRN_MAX_STEP_DECAY = 7.5
NEG_BIG = -1e30
LOG2E = 1.4426950408889634
Q_SCALE = NSA_DH ** -0.5 * LOG2E
SCORE_BOUND = 96.0
VT_ROWS = NSA_DH + 16

F32 = jnp.float32
BF16 = jnp.bfloat16


def _cparams(n_grid, vmem=VMEM_LIMIT):
    return pltpu.CompilerParams(dimension_semantics=("arbitrary",) * n_grid, vmem_limit_bytes=vmem)


def _resident(shape):
    nd = len(shape)
    return pl.BlockSpec(shape, lambda *_: (0,) * nd, pipeline_mode=pl.Buffered(1))


def _dot(a, b):
    return jnp.dot(a, b, preferred_element_type=F32)


def _dot_nt(a, b):
    return lax.dot_general(a, b, (((1,), (1,)), ((), ())), preferred_element_type=F32)


def _dot_tn(a, b):
    return lax.dot_general(a, b, (((0,), (0,)), ((), ())), preferred_element_type=F32)


def _sigmoid(x):
    return 1.0 / (1.0 + jnp.exp(-x))


def _silu(x):
    return x * _sigmoid(x)


def _layer_norm(z, g, b):
    mu = jnp.mean(z, axis=-1, keepdims=True)
    zc = z - mu
    var = jnp.mean(zc * zc, axis=-1, keepdims=True)
    return zc * lax.rsqrt(var + LN_EPS) * g + b


def _adaln_kernel(c_ref, w_ref, b_ref, o_ref):
    cond = _silu(c_ref[...]).astype(BF16)
    o_ref[0] = _dot(cond, w_ref[0]) + b_ref[0]


def adaln_mod(c, w_ada, b_ada):
    L, D, N = w_ada.shape
    B = c.shape[0]
    tn = D
    return pl.pallas_call(
        _adaln_kernel,
        grid=(L, N // tn),
        in_specs=[
            pl.BlockSpec((B, D), lambda l, j: (0, 0)),
            pl.BlockSpec((1, D, tn), lambda l, j: (l, 0, j)),
            pl.BlockSpec((1, 1, tn), lambda l, j: (l, 0, j)),
        ],
        out_specs=pl.BlockSpec((1, B, tn), lambda l, j: (l, 0, j)),
        out_shape=jax.ShapeDtypeStruct((L, B, N), F32),
        compiler_params=_cparams(2),
        name="adaln_mod",
    )(c, w_ada.astype(BF16), b_ada.reshape(L, 1, N))


N_ROPE = NSA_WIDTH + 3 * NSA_KV_WIDTH
N_NSA = N_ROPE + 3 * NSA_KV_WIDTH
N_GATE = NSA_GROUPS * LANES
N_HGRN = 2 * HGRN_WIDTH + 2 * HGRN_VWIDTH
N_MERGE = 2 * D_MODEL


def _in_proj_kernel(x_ref, sc_ref, sh_ref, cos_ref, sin_ref, lb_ref,
                    wn_ref, bn_ref, wg_ref, bg_ref, wh_ref, bh_ref, wm_ref, bm_ref,
                    q_ref, kx_ref, kv_ref, vt_ref, ga_ref, nrm_ref, hq_ref, hlf_ref, hv_ref, hg_ref, gma_ref, gmb_ref):
    u = (x_ref[0] * (1.0 + sc_ref[0]) + sh_ref[0]).astype(BF16)
    cos = cos_ref[...]
    sin = sin_ref[...]
    lane = lax.broadcasted_iota(jnp.int32, cos.shape, 1)
    first_half = (lane % NSA_DH) < (NSA_DH // 2)
    low = lane < NSA_DH

    def heads(t, upper):
        return jnp.where(low, t, upper), jnp.where(low, pltpu.roll(t, NSA_DH, 1), upper)

    n_q, n_rope = NSA_HEADS // 2, N_ROPE // LANES
    tm = u.shape[0]
    ones_rows = jnp.where(lax.broadcasted_iota(jnp.int32, (VT_ROWS - NSA_DH, tm), 0) == 0, 1.0, 0.0).astype(BF16)

    def max_sq_norm(t, acc):
        n = jnp.max(jnp.sum(t * t, axis=1, keepdims=True), axis=0, keepdims=True)
        return n if acc is None else jnp.maximum(acc, n)

    q_sq = k_sq = None
    for i2 in range(0, N_NSA // LANES, 2):
        t2 = _dot(u, wn_ref[:, i2 * LANES:(i2 + 2) * LANES]) + bn_ref[:, i2 * LANES:(i2 + 2) * LANES]
        for i in (i2, i2 + 1):
            t = t2[:, (i - i2) * LANES:(i - i2 + 1) * LANES]
            if i < n_rope:
                rot = jnp.where(first_half, pltpu.roll(t, LANES - NSA_DH // 2, 1), pltpu.roll(t, NSA_DH // 2, 1))
                t = t * cos + rot * sin
            if i < n_q:
                t = t * Q_SCALE
                q_sq = max_sq_norm(t, q_sq)
                for j, piece in enumerate(heads(t, 0.0)):
                    q_ref[0, 2 * i + j] = piece.astype(BF16)
            elif i == n_q or i == n_rope:
                for j in range(2):
                    kv_ref[0, (2 if i == n_rope else 0) + j] = t[:, j * NSA_DH:(j + 1) * NSA_DH].astype(BF16)
            elif i < n_rope:
                kind = i - n_q - 1
                k_sq = max_sq_norm(t, k_sq)
                for j, piece in enumerate(heads(t, 0.0)):
                    kx_ref[0, 2 * kind + j] = piece.astype(BF16)
            else:
                kind = i - n_rope - 1
                tt = t.T.astype(BF16)
                for j in range(2):
                    vt_ref[0, 2 * kind + j] = jnp.concatenate([tt[j * NSA_DH:(j + 1) * NSA_DH], ones_rows], axis=0)
    gates = _sigmoid(_dot(u, wg_ref[...]) + bg_ref[...])
    for g in range(NSA_GROUPS):
        ga_ref[0, g] = gates[:, g * LANES:(g + 1) * LANES].T[0:2 * SUBLANES]
    W = HGRN_WIDTH
    hq = _dot(u, wh_ref[:, 0:W]) + bh_ref[:, 0:W]
    hq_ref[0] = (_silu(hq) * (HGRN_DK ** -0.5)).astype(BF16)
    z = _dot(u, wh_ref[:, W:2 * W]) + bh_ref[:, W:2 * W]
    lb = lb_ref[...]
    log_sig = jnp.minimum(z, 0.0) - jnp.log1p(jnp.exp(-jnp.abs(z)))
    a = jnp.log(lb)
    bb = jnp.log1p(-lb) + log_sig
    log_f = jnp.maximum(a, bb) + jnp.log1p(jnp.exp(-jnp.abs(a - bb)))
    hlf_ref[0] = log_f
    lf_min = jnp.min(jnp.min(log_f, axis=1, keepdims=True), axis=0, keepdims=True)
    srow = lax.broadcasted_iota(jnp.int32, (SUBLANES, LANES), 0)
    nrm_ref[0, 0] = jnp.where(srow == 0, q_sq, jnp.where(srow == 1, k_sq, jnp.where(srow == 2, lf_min, 0.0)))
    hv_ref[0] = (_dot(u, wh_ref[:, 2 * W:3 * W]) + bh_ref[:, 2 * W:3 * W]).astype(BF16)
    hg_ref[0] = _silu(_dot(u, wh_ref[:, 3 * W:4 * W]) + bh_ref[:, 3 * W:4 * W]).astype(BF16)
    for i, ref in enumerate((gma_ref, gmb_ref)):
        for j in range(2):
            c0 = i * D_MODEL + j * (D_MODEL // 2)
            c1 = c0 + D_MODEL // 2
            ref[0, :, j * (D_MODEL // 2):(j + 1) * (D_MODEL // 2)] = _sigmoid(
                _dot(u, wm_ref[:, c0:c1]) + bm_ref[:, c0:c1]).astype(BF16)


def in_proj(x, sc, sh, cos_t, sin_t, lb, wts):
    B, S, D = x.shape
    tm = min(TOKEN_TILE, S)
    assert S // SEL_LEN <= LANES - NSA_DH
    wn, bn, wg, bg, wh, bh, wm, bm = wts
    tok = lambda n: pl.BlockSpec((1, tm, n), lambda b, i: (b, i, 0))
    per_b = pl.BlockSpec((1, 1, D), lambda b, i: (b, 0, 0))
    tab = pl.BlockSpec((tm, LANES), lambda b, i: (i, 0))
    out_shape = (
        jax.ShapeDtypeStruct((B, NSA_HEADS, S, LANES), BF16),
        jax.ShapeDtypeStruct((B, 2 * NSA_GROUPS, S, LANES), BF16),
        jax.ShapeDtypeStruct((B, 2 * NSA_GROUPS, S, NSA_DH), BF16),
        jax.ShapeDtypeStruct((B, 2 * NSA_GROUPS, VT_ROWS, S), BF16),
        jax.ShapeDtypeStruct((B, NSA_GROUPS, 2 * SUBLANES, S), F32),
        jax.ShapeDtypeStruct((B, S // tm, SUBLANES, LANES), F32),
        jax.ShapeDtypeStruct((B, S, HGRN_WIDTH), BF16),
        jax.ShapeDtypeStruct((B, S, HGRN_WIDTH), F32),
        jax.ShapeDtypeStruct((B, S, HGRN_VWIDTH), BF16),
        jax.ShapeDtypeStruct((B, S, HGRN_VWIDTH), BF16),
        jax.ShapeDtypeStruct((B, S, D), BF16),
        jax.ShapeDtypeStruct((B, S, D), BF16),
    )
    out_specs = (
        pl.BlockSpec((1, NSA_HEADS, tm, LANES), lambda b, i: (b, 0, i, 0)),
        pl.BlockSpec((1, 2 * NSA_GROUPS, tm, LANES), lambda b, i: (b, 0, i, 0)),
        pl.BlockSpec((1, 2 * NSA_GROUPS, tm, NSA_DH), lambda b, i: (b, 0, i, 0)),
        pl.BlockSpec((1, 2 * NSA_GROUPS, VT_ROWS, tm), lambda b, i: (b, 0, 0, i)),
        pl.BlockSpec((1, NSA_GROUPS, 2 * SUBLANES, tm), lambda b, i: (b, 0, 0, i)),
        pl.BlockSpec((1, 1, SUBLANES, LANES), lambda b, i: (b, i, 0, 0)),
        tok(HGRN_WIDTH), tok(HGRN_WIDTH), tok(HGRN_VWIDTH), tok(HGRN_VWIDTH), tok(D), tok(D),
    )
    return pl.pallas_call(
        _in_proj_kernel,
        grid=(B, S // tm),
        in_specs=[tok(D), per_b, per_b, tab, tab, _resident(lb.shape),
                  _resident(wn.shape), _resident(bn.shape), _resident(wg.shape), _resident(bg.shape),
                  _resident(wh.shape), _resident(bh.shape), _resident(wm.shape), _resident(bm.shape)],
        out_specs=out_specs,
        out_shape=out_shape,
        compiler_params=_cparams(2),
        name="in_proj",
    )(x, sc, sh, cos_t, sin_t, lb, wn, bn, wg, bg, wh, bh, wm, bm)


def _prep_in_proj_weights(w_in_l, b_in_l):
    o = IN_OFFSETS
    col = lambda i: (w_in_l[:, o[i]:o[i + 1]], b_in_l[o[i]:o[i + 1]])
    q_a, k_c, v_c, k_s, v_s, k_w, v_w, g_a, q_b, f_b, i_b, g_b, gm_a, gm_b = [col(i) for i in range(14)]

    def cat(parts):
        return (jnp.concatenate([p[0] for p in parts], axis=1).astype(BF16),
                jnp.concatenate([p[1] for p in parts], axis=0).reshape(1, -1).astype(F32))

    wn, bn = cat([q_a, k_c, k_s, k_w, v_c, v_s, v_w])
    per_group = 3 * NSA_REP
    gw = jnp.zeros((w_in_l.shape[0], N_GATE), w_in_l.dtype)
    gb = jnp.zeros((N_GATE,), b_in_l.dtype)
    for g in range(NSA_GROUPS):
        gw = gw.at[:, g * LANES:g * LANES + per_group].set(g_a[0][:, g * per_group:(g + 1) * per_group])
        gb = gb.at[g * LANES:g * LANES + per_group].set(g_a[1][g * per_group:(g + 1) * per_group])
    wg, bg = gw.astype(BF16), gb.reshape(1, -1).astype(F32)
    wh, bh = cat([q_b, f_b, i_b, g_b])
    wm, bm = cat([gm_a, gm_b])
    return wn, bn, wg, bg, wh, bh, wm, bm


def _compress_kernel(t_ref, pe_ref, w1_ref, w2_ref, o_ref):
    half = CMP_STRIDE * NSA_DH
    t = t_ref[0, 0]
    nrow = t.shape[0]
    a = _dot(t, w1_ref[0, 0:half, :])
    b = _dot(t, w1_ref[0, half:2 * half, :])
    pe = jnp.broadcast_to(pe_ref[0], (8, 2 * half)).astype(BF16)
    c = _dot(pe, w1_ref[0])[0:1]
    h = a + pltpu.roll(b, nrow - 1, 0) + c
    o_ref[0, 0] = _dot(_silu(h).astype(BF16), w2_ref[0]).astype(BF16)


def nsa_compress(kv, pe, w1, w2):
    B, _, S, dh = kv.shape
    nrow = S // CMP_STRIDE
    G = NSA_GROUPS
    kv_rows = kv.reshape(B, 2 * G, nrow, CMP_STRIDE * dh)
    return pl.pallas_call(
        _compress_kernel,
        grid=(B, 2, G),
        in_specs=[
            pl.BlockSpec((1, 1, nrow, CMP_STRIDE * dh), lambda b, s, g: (b, s * G + g, 0, 0)),
            pl.BlockSpec((1, 1, CMP_LEN * dh), lambda b, s, g: (s, 0, 0)),
            pl.BlockSpec((1, CMP_LEN * dh, CMP_HIDDEN), lambda b, s, g: (s, 0, 0)),
            pl.BlockSpec((1, CMP_HIDDEN, LANES), lambda b, s, g: (s, 0, 0)),
        ],
        out_specs=pl.BlockSpec((1, 1, nrow, LANES), lambda b, s, g: (b, s * G + g, 0, 0)),
        out_shape=jax.ShapeDtypeStruct((B, 2 * G, nrow, LANES), BF16),
        compiler_params=_cparams(3),
        name="nsa_compress",
    )(kv_rows, pe, w1, w2)


def _nsa_kernel(q_ref, cmp_ref, kx_ref, vt_ref, gt_ref, ovt_ref, o_ref, *, n_sel, n_tiles, bounded_scores):
    def tile(qs):
        gens = [_nsa_tile(qs, bi, g, q_ref, cmp_ref, kx_ref, vt_ref, gt_ref, ovt_ref, o_ref, n_sel, bounded_scores)
                for bi in range(q_ref.shape[0]) for g in range(NSA_GROUPS)]
        while gens:
            gens = [gen for gen in gens if next(gen, "done") != "done"]

    for qs in range(n_tiles):
        pl.when(pl.program_id(1) == qs)(functools.partial(tile, qs))


def _round_up(x, m):
    return -(-x // m) * m


def _nsa_tile(qb, bi, g, q_ref, cmp_ref, kx_ref, vt_ref, gt_ref, ovt_ref, o_ref, n_sel, bounded_scores):
    G, R, TQ, dh = NSA_GROUPS, NSA_REP, Q_TILE, NSA_DH
    cols = R * TQ
    s0 = qb * TQ
    kc_ref, vc_ref = cmp_ref.at[bi, g], cmp_ref.at[bi, G + g]
    ks_ref, kw_ref = kx_ref.at[bi, g], kx_ref.at[bi, G + g]
    vst_ref, vwt_ref = vt_ref.at[bi, g], vt_ref.at[bi, G + g]
    q = q_ref[bi, g * R:(g + 1) * R].reshape(cols, LANES)
    t_lane = s0 + (lax.broadcasted_iota(jnp.int32, (1, cols), 1) % TQ)
    kidx = lax.broadcasted_iota(jnp.int32, (K_CHUNK, cols), 0)
    tq = lax.broadcasted_iota(jnp.int32, (K_CHUNK, cols), 1) % TQ
    causal = kidx <= tq

    def branch_blocks(k_ref, vt_ref, c0, n, band_first):
        return [(k_ref, vt_ref, c0 + b0, min(ATT_BLOCK, n - b0), band_first and b0 == 0, b0 + ATT_BLOCK >= n)
                for b0 in range(0, n, ATT_BLOCK)]

    def score_block(blk):
        k_ref, _, c, n, band, diag = blk
        s = _dot_nt(k_ref[c * K_CHUNK:(c + n) * K_CHUNK, :], q)
        parts = [s[i * K_CHUNK:(i + 1) * K_CHUNK] for i in range(n)]
        if band:
            parts[0] = jnp.where(tq < kidx, parts[0], NEG_BIG)
        if diag:
            parts[-1] = jnp.where(causal, parts[-1], NEG_BIG)
        return jnp.concatenate(parts, axis=0) if n > 1 else parts[0]

    def finish_block(s, blk, block_bias=None):
        _, vt_ref, c, n, _, _ = blk
        nk = n * K_CHUNK
        m = None
        if block_bias is not None:
            j0 = c * K_CHUNK // SEL_LEN
            subs = [s[i * SEL_LEN:(i + 1) * SEL_LEN] for i in range(nk // SEL_LEN)]
            bias = [block_bias[j0 + i:j0 + i + 1, :] for i in range(nk // SEL_LEN)]
        if bounded_scores:
            p = jnp.exp2(s) if block_bias is None else jnp.concatenate(
                [jnp.exp2(s_i + b_i) for s_i, b_i in zip(subs, bias)], axis=0)
        elif block_bias is None:
            m = jnp.max(s, axis=0, keepdims=True)
            p = jnp.exp2(s - m)
        else:
            for s_i, b_i in zip(subs, bias):
                m_i = jnp.max(s_i, axis=0, keepdims=True) + b_i
                m = m_i if m is None else jnp.maximum(m, m_i)
            shift = jnp.where(m < 0.5 * NEG_BIG, 0.0, m)
            p = jnp.concatenate([jnp.exp2(s_i + (b_i - shift)) for s_i, b_i in zip(subs, bias)], axis=0)
        vt = vt_ref[:, c * K_CHUNK:c * K_CHUNK + nk]
        return m, _dot(vt, p.astype(BF16))

    def combine(stats):
        total = stats[0][1]
        if bounded_scores:
            for _, acc_i in stats[1:]:
                total = total + acc_i
        elif len(stats) > 1:
            m = stats[0][0]
            for st in stats[1:]:
                m = jnp.maximum(m, st[0])
            total = None
            for m_i, acc_i in stats:
                w = jnp.exp2(m_i - m)
                total = w * acc_i if total is None else total + w * acc_i
        return total[0:dh] * (1.0 / total[dh:dh + 1])

    n_win = WINDOW // K_CHUNK
    win_blocks = branch_blocks(kw_ref, vwt_ref, max(qb - n_win, 0), min(qb, n_win) + 1, qb >= n_win)
    blocks = win_blocks + branch_blocks(ks_ref, vst_ref, 0, qb + 1, False)

    ncb = min(kc_ref.shape[0], _round_up((s0 + TQ - CMP_LEN) // CMP_STRIDE + 1, 2 * SUBLANES))
    sc = _dot_nt(kc_ref[0:ncb, :], q)
    pending = {i: score_block(blocks[i]) for i in range(min(ATT_LOOKAHEAD, len(blocks)))}
    yield
    n_sub = lax.broadcasted_iota(jnp.int32, (ncb, cols), 0)
    mask_c = n_sub * CMP_STRIDE + (CMP_LEN - 1) <= t_lane
    sc = jnp.where(mask_c, sc, NEG_BIG)
    mc = jnp.max(sc, axis=0, keepdims=True)
    ec = jnp.where(mask_c, jnp.exp2(sc - mc), 0.0)
    pc = ec * (1.0 / jnp.maximum(jnp.sum(ec, axis=0, keepdims=True), 1e-30))
    o_c = _dot_tn(vc_ref[0:ncb, :], pc.astype(BF16))[0:dh]

    nb_live = (s0 + TQ) // SEL_LEN
    nb = min(ovt_ref.shape[0], _round_up(nb_live, 2 * SUBLANES))
    psum = pc[:, 0:TQ]
    for r in range(1, R):
        psum = psum + pc[:, r * TQ:(r + 1) * TQ]
    p_hi = psum.astype(BF16)
    p_lo = (psum - p_hi.astype(F32)).astype(BF16)
    ovt = ovt_ref[0:nb, 0:ncb]
    imp = _dot(ovt, p_hi) + _dot(ovt, p_lo)
    yield
    jb =lax.broadcasted_iota(jnp.int32, (nb, TQ), 0)
    tb = (s0 + lax.broadcasted_iota(jnp.int32, (nb, TQ), 1)) // SEL_LEN
    valid = jb <= tb
    forced = jnp.where(valid, jnp.where(jb == 0, 1.0, jnp.where(jb >= tb - 1, 1.0, 0.0)), 0.0)
    score = jnp.where(forced > 0.5, FORCE_SCORE, jnp.where(valid, imp, -1.0))
    rank = jnp.zeros((nb, TQ), F32)
    for i in range(nb_live):
        si = score[i:i + 1, :]
        tie_first = jnp.where(jb > i, 1.0, 0.0)
        rank = rank + jnp.where(si > score, 1.0, jnp.where(si == score, tie_first, 0.0))
    sel_bias = jnp.where(rank < n_sel, 0.0, NEG_BIG)
    sel_bias = jnp.concatenate([sel_bias] * R, axis=1)

    stats = []
    for i, blk in enumerate(blocks):
        is_sel = i >= len(win_blocks)
        stats.append(finish_block(pending.pop(i), blk, sel_bias if is_sel else None))
        if i + ATT_LOOKAHEAD < len(blocks):
            pending[i + ATT_LOOKAHEAD] = score_block(blocks[i + ATT_LOOKAHEAD])
        yield
    o_w = combine(stats[:len(win_blocks)])
    o_s = combine(stats[len(win_blocks):])

    gate = gt_ref[bi, g]
    pieces = []
    for r in range(R):
        sl = slice(r * TQ, (r + 1) * TQ)
        o_r = (gate[3 * r:3 * r + 1, :] * o_c[:, sl] + gate[3 * r + 1:3 * r + 2, :] * o_s[:, sl]
               + gate[3 * r + 2:3 * r + 3, :] * o_w[:, sl])
        pieces.append(o_r.T)
    o_ref[bi, :, g * R * dh:(g + 1) * R * dh] = jnp.concatenate(pieces, axis=1).astype(BF16)


def nsa_attend(q, kx, vt, cmp, gates_t, bounded_scores):
    B, H, S, _ = q.shape
    G, R, dh = NSA_GROUPS, NSA_REP, NSA_DH
    ncb = S // CMP_STRIDE
    nb = S // SEL_LEN
    assert (S % Q_TILE == 0 and Q_TILE == K_CHUNK and WINDOW % K_CHUNK == 0 and K_CHUNK % SEL_LEN == 0
            and 3 * R <= 2 * SUBLANES)
    cstart = np.arange(ncb) * CMP_STRIDE
    sstart = np.arange(nb) * SEL_LEN
    overlap = ((cstart[:, None] < sstart[None, :] + SEL_LEN) & (cstart[:, None] + CMP_LEN > sstart[None, :]))
    ovt = jnp.asarray(overlap.T, BF16)
    nbt = NSA_BATCH if B % NSA_BATCH == 0 else 1
    per_batch = lambda a: pl.BlockSpec((nbt,) + a.shape[1:], lambda b, i: (b, 0, 0, 0))
    return pl.pallas_call(
        functools.partial(_nsa_kernel, n_sel=min(SEL_TOPK, nb), n_tiles=S // Q_TILE, bounded_scores=bounded_scores),
        grid=(B // nbt, S // Q_TILE),
        in_specs=[
            pl.BlockSpec((nbt, H, Q_TILE, LANES), lambda b, i: (b, 0, i, 0)),
            per_batch(cmp), per_batch(kx), per_batch(vt),
            pl.BlockSpec((nbt, G, 2 * SUBLANES, Q_TILE), lambda b, i: (b, 0, 0, i)),
            pl.BlockSpec(ovt.shape, lambda b, i: (0, 0)),
        ],
        out_specs=pl.BlockSpec((nbt, Q_TILE, H * dh), lambda b, i: (b, i, 0)),
        out_shape=jax.ShapeDtypeStruct((B, S, H * dh), BF16),
        compiler_params=_cparams(2),
        name="nsa_attend",
    )(q, cmp, kx, vt, gates_t, ovt)


def _hgrn_kernel(q_ref, lf_ref, v_ref, g_ref, ng_ref, tri_ref, o_ref, st_ref, *, bounded_decay):
    C, SB = HGRN_CHUNK, HGRN_SUB
    NBK = C // SB
    S = q_ref.shape[1]
    st_ref[...] = jnp.zeros(st_ref.shape, F32)
    row = lax.broadcasted_iota(jnp.int32, (C, HGRN_DK), 0)
    sub_row = lax.broadcasted_iota(jnp.int32, (SB, 1), 0)
    cr = lax.broadcasted_iota(jnp.int32, (C, C), 0)
    cc = lax.broadcasted_iota(jnp.int32, (C, C), 1)
    diag_mask = (cr // SB == cc // SB) & (cc <= cr)

    def head_chunk(h, r0):
        hs = slice(h * HGRN_DK, (h + 1) * HGRN_DK)
        q = q_ref[0, pl.ds(r0, C), hs].astype(F32)
        lf = lf_ref[0, pl.ds(r0, C), hs]
        v_bf = v_ref[0, pl.ds(r0, C), hs]
        v = v_bf.astype(F32)
        kh = 1.0 - jnp.exp(lf)
        tri = tri_ref[...]
        lf0 = lf.astype(BF16)
        lf1 = (lf - lf0.astype(F32)).astype(BF16)
        lf2 = (lf - lf0.astype(F32) - lf1.astype(F32)).astype(BF16)
        b = _dot(tri, lf0) + _dot(tri, lf1) + _dot(tri, lf2)
        yield
        b_last = b[C - 1:C, :]
        st = st_ref[h]
        o = _dot_nt((q * jnp.exp(b)).astype(BF16), st.astype(BF16))
        b_end = jnp.concatenate(
            [jnp.broadcast_to(b[(j + 1) * SB - 1:(j + 1) * SB, :], (SB, HGRN_DK)) for j in range(NBK)], axis=0)
        k_end = kh * jnp.exp(b_end - b)
        q_parts, k_parts = [], []
        for j in range(NBK - 1):
            lo = (j + 1) * SB
            qj = q[lo:] * jnp.exp(b[lo:] - b[lo - 1:lo, :])
            q_parts.append(jnp.concatenate([jnp.zeros((lo, HGRN_DK), F32), qj], axis=0))
            k_parts.append(jnp.where((row >= j * SB) & (row < lo), k_end, 0.0))
        q_cat = jnp.concatenate(q_parts, axis=1).astype(BF16)
        k_cat = jnp.concatenate(k_parts, axis=1).astype(BF16)
        a_off = _dot_nt(q_cat, k_cat)
        k_last = (kh * jnp.exp(b_last - b)).astype(BF16)
        st_ref[h] = st * jnp.exp(b_last) + _dot_tn(v_bf, k_last)
        if bounded_decay:
            b_start = jnp.concatenate([jnp.zeros((SB, HGRN_DK), F32), b_end[:C - SB]], axis=0)
            a_dg = _dot_nt((q * jnp.exp(b - b_start)).astype(BF16), (kh * jnp.exp(b_start - b)).astype(BF16))
            yield
            o = o + _dot(jnp.where(diag_mask, a_dg, a_off).astype(BF16), v_bf)
            yield
        else:
            yield
            o = o + _dot(a_off.astype(BF16), v_bf)
            diag = []
            for j in range(NBK):
                sl = slice(j * SB, (j + 1) * SB)
                qj, bj, kj, vj = q[sl], b[sl], kh[sl], v[sl]
                oj = jnp.zeros((SB, HGRN_DV), F32)
                for s in range(SB):
                    w = jnp.exp(jnp.minimum(bj - bj[s:s + 1, :], 0.0))
                    a = jnp.sum(qj * kj[s:s + 1, :] * w, axis=-1, keepdims=True)
                    a = jnp.where(sub_row >= s, a, 0.0)
                    oj = oj + a * vj[s:s + 1, :]
                diag.append(oj)
            o = o + jnp.concatenate(diag, axis=0)
            yield
        o = o * lax.rsqrt(jnp.mean(o * o, axis=-1, keepdims=True) + RMS_EPS) * ng_ref[...]
        o_ref[0, pl.ds(r0, C), hs] = (o * g_ref[0, pl.ds(r0, C), hs].astype(F32)).astype(BF16)

    def chunk(ci, carry):
        r0 = pl.multiple_of(ci * (C * HGRN_UNROLL), C * HGRN_UNROLL)
        heads = [head_chunk(h, r0 + u * C) for u in range(HGRN_UNROLL) for h in range(HGRN_HEADS)]
        for _ in range(3):
            for gen in heads:
                next(gen)
        for gen in heads:
            next(gen, None)
        return carry

    lax.fori_loop(0, S // (C * HGRN_UNROLL), chunk, 0)


def hgrn2(hq, hlf, hv, hg, norm_g, bounded_decay):
    B, S, _ = hq.shape
    C = HGRN_CHUNK
    assert S % (C * HGRN_UNROLL) == 0
    tri = jnp.asarray(np.tril(np.ones((C, C))), BF16)
    spec = lambda n: pl.BlockSpec((1, S, n), lambda b: (b, 0, 0))
    return pl.pallas_call(
        functools.partial(_hgrn_kernel, bounded_decay=bounded_decay),
        grid=(B,),
        in_specs=[spec(HGRN_WIDTH), spec(HGRN_WIDTH), spec(HGRN_VWIDTH), spec(HGRN_VWIDTH),
                  pl.BlockSpec((1, HGRN_DV), lambda b: (0, 0)),
                  pl.BlockSpec((C, C), lambda b: (0, 0))],
        out_specs=spec(HGRN_VWIDTH),
        out_shape=jax.ShapeDtypeStruct((B, S, HGRN_VWIDTH), BF16),
        scratch_shapes=[pltpu.VMEM((HGRN_HEADS, HGRN_DV, HGRN_DK), F32)],
        compiler_params=_cparams(1),
        name="hgrn2",
    )(hq, hlf, hv, hg, norm_g.reshape(1, HGRN_DV).astype(F32), tri)


MLP_COLS = 1024


def _merge_mlp_kernel(ya_ref, yb_ref, gma_ref, gmb_ref, x_ref, mod_ref, ln_ref,
                      wa_ref, wb_ref, wo_ref, w1_ref, w2_ref, o_ref):
    gt1, sh2, sc2, gt2 = [mod_ref[0, i:i + 1, :] for i in range(4)]
    g1, b1, g2, b2 = [ln_ref[i:i + 1, :] for i in range(4)]
    pa = _dot(ya_ref[0], wa_ref[...])
    pb = _dot(yb_ref[0], wb_ref[...])
    merged = gma_ref[0].astype(F32) * pa + gmb_ref[0].astype(F32) * pb
    y = _dot(merged.astype(BF16), wo_ref[...])
    x = _layer_norm(DEEPNORM_ALPHA * x_ref[0] + (1.0 + gt1) * y, g1, b1)
    u = (x * (1.0 + sc2) + sh2).astype(BF16)
    y = jnp.zeros(x.shape, F32)
    for c in range(MLP_HIDDEN // MLP_COLS):
        h = jnp.maximum(_dot(u, w1_ref[:, c * MLP_COLS:(c + 1) * MLP_COLS]), 0.0)
        y = y + _dot((h * h).astype(BF16), w2_ref[c * MLP_COLS:(c + 1) * MLP_COLS, :])
    o_ref[0] = _layer_norm(DEEPNORM_ALPHA * x + (1.0 + gt2) * y, g2, b2)


def merge_mlp(ya, yb, gma, gmb, x, mod4, ln4, wa, wb, wo, w1, w2):
    B, S, D = x.shape
    tm = min(TOKEN_TILE, S)
    tok = lambda n: pl.BlockSpec((1, tm, n), lambda b, i: (b, i, 0))
    return pl.pallas_call(
        _merge_mlp_kernel,
        grid=(B, S // tm),
        in_specs=[tok(NSA_WIDTH), tok(HGRN_VWIDTH), tok(D), tok(D), tok(D),
                  pl.BlockSpec((1, 4, D), lambda b, i: (b, 0, 0)), _resident(ln4.shape),
                  _resident(wa.shape), _resident(wb.shape), _resident(wo.shape),
                  _resident(w1.shape), _resident(w2.shape)],
        out_specs=tok(D),
        out_shape=jax.ShapeDtypeStruct((B, S, D), F32),
        compiler_params=_cparams(2, vmem=MERGE_MLP_VMEM),
        name="merge_mlp",
    )(ya, yb, gma, gmb, x, mod4, ln4, wa, wb, wo, w1, w2)


def _rope_tables(S):
    inv = 1.0 / (ROPE_THETA ** (jnp.arange(0, NSA_DH, 2, dtype=F32) / NSA_DH))
    ang = jnp.arange(S, dtype=F32)[:, None] * inv[None, :]
    cos, sin = jnp.cos(ang), jnp.sin(ang)
    reps = LANES // NSA_DH
    return (jnp.tile(jnp.concatenate([cos, cos], axis=1), (1, reps)),
            jnp.tile(jnp.concatenate([-sin, sin], axis=1), (1, reps)))


def kernel(x, c, w_in, b_in, cmp_pe_k, cmp_pe_v, cmp_wk1, cmp_wk2, cmp_wv1, cmp_wv2, hgrn_lb_logits, hgrn_norm_g, w_branch_a, w_branch_b, w_out, w_ada, b_ada, ln1_g, ln1_b, w_mlp1, w_mlp2, ln2_g, ln2_b):
    B, S, D = x.shape
    G = NSA_GROUPS
    lb_all = jnp.cumsum(jax.nn.softmax(hgrn_lb_logits.astype(F32), axis=0), axis=0)
    lb_all = lb_all - lb_all[0:1]
    cos_t, sin_t = _rope_tables(S)
    mod = adaln_mod(c, w_ada, b_ada)
    for l in range(DEPTH):
        sh1, sc1, gt1, sh2, sc2, gt2 = [mod[l, :, None, i * D:(i + 1) * D] for i in range(6)]
        wts = _prep_in_proj_weights(w_in[l], b_in[l])
        q, kx, kv, vt, gates_t, nrm, hq, hlf, hv, hg, gma, gmb = in_proj(x, sc1, sh1, cos_t, sin_t, lb_all[l].reshape(1, -1), wts)
        pe = jnp.stack([cmp_pe_k[l].reshape(1, -1), cmp_pe_v[l].reshape(1, -1)])
        w1 = jnp.stack([cmp_wk1[l], cmp_wv1[l]]).astype(BF16)
        w2 = jnp.pad(jnp.stack([cmp_wk2[l], cmp_wv2[l]]), ((0, 0), (0, 0), (0, LANES - NSA_DH))).astype(BF16)
        cmp = nsa_compress(kv, pe, w1, w2)
        bounded = jnp.max(nrm[:, :, 0, 0]) * jnp.max(nrm[:, :, 1, 0]) <= SCORE_BOUND ** 2
        ya = lax.cond(bounded,
                      functools.partial(nsa_attend, bounded_scores=True),
                      functools.partial(nsa_attend, bounded_scores=False),
                      q, kx, vt, cmp, gates_t)
        yb = lax.cond(jnp.min(nrm[:, :, 2, 0]) >= -HGRN_MAX_STEP_DECAY,
                      functools.partial(hgrn2, bounded_decay=True),
                      functools.partial(hgrn2, bounded_decay=False),
                      hq, hlf, hv, hg, hgrn_norm_g[l])
        mod4 = jnp.concatenate([gt1, sh2, sc2, gt2], axis=1)
        ln4 = jnp.stack([ln1_g[l], ln1_b[l], ln2_g[l], ln2_b[l]])
        x = merge_mlp(ya, yb, gma, gmb, x, mod4, ln4,
                      w_branch_a[l].astype(BF16), w_branch_b[l].astype(BF16), w_out[l].astype(BF16),
                      w_mlp1[l].astype(BF16), w_mlp2[l].astype(BF16))
    return x
```

```python
import functools

import numpy as np
import jax
import jax.numpy as jnp
from jax import lax
from jax.experimental import pallas as pl
from jax.experimental.pallas import tpu as pltpu

D_MODEL = 1024
DEPTH = 2
NSA_HEADS = 8
NSA_GROUPS = 2
NSA_REP = NSA_HEADS // NSA_GROUPS
NSA_DH = 64
NSA_WIDTH = NSA_HEADS * NSA_DH
NSA_KV_WIDTH = NSA_GROUPS * NSA_DH
CMP_LEN = 32
CMP_STRIDE = 16
CMP_HIDDEN = 2 * NSA_DH
SEL_LEN = 64
SEL_TOPK = 8
FORCE_SCORE = 1.0e4
WINDOW = 512
HGRN_HEADS = 4
HGRN_DK = 128
HGRN_DV = 128
HGRN_WIDTH = HGRN_HEADS * HGRN_DK
HGRN_VWIDTH = HGRN_HEADS * HGRN_DV
MLP_HIDDEN = 4 * D_MODEL
ROPE_THETA = 10000.0
LN_EPS = 1e-5
RMS_EPS = 1e-6
DEEPNORM_ALPHA = (2 * DEPTH) ** 0.25
IN_SIZES = (NSA_WIDTH,) + (NSA_KV_WIDTH,) * 6 + (3 * NSA_HEADS,) + (HGRN_WIDTH, HGRN_WIDTH, HGRN_VWIDTH, HGRN_VWIDTH) + (D_MODEL, D_MODEL)
IN_OFFSETS = [0] + [int(v) for v in np.cumsum(IN_SIZES)]

LANES = 128
SUBLANES = 8
VMEM_LIMIT = 48 * 1024 * 1024
MERGE_MLP_VMEM = 56 * 1024 * 1024
TOKEN_TILE = 512
Q_TILE = 128
K_CHUNK = 128
ATT_BLOCK = 4
NSA_BATCH = 1
ATT_LOOKAHEAD = 2
HGRN_CHUNK = 64
HGRN_UNROLL = 4
HGRN_SUB = 8
HGRN_MAX_STEP_DECAY = 7.5
NEG_BIG = -1e30
LOG2E = 1.4426950408889634
Q_SCALE = NSA_DH ** -0.5 * LOG2E
SCORE_BOUND = 96.0
VT_ROWS = NSA_DH + 16

F32 = jnp.float32
BF16 = jnp.bfloat16


def _cparams(n_grid, vmem=VMEM_LIMIT):
    return pltpu.CompilerParams(dimension_semantics=("arbitrary",) * n_grid, vmem_limit_bytes=vmem)


def _resident(shape):
    nd = len(shape)
    return pl.BlockSpec(shape, lambda *_: (0,) * nd, pipeline_mode=pl.Buffered(1))


def _dot(a, b):
    return jnp.dot(a, b, preferred_element_type=F32)


def _dot_nt(a, b):
    return lax.dot_general(a, b, (((1,), (1,)), ((), ())), preferred_element_type=F32)


def _dot_tn(a, b):
    return lax.dot_general(a, b, (((0,), (0,)), ((), ())), preferred_element_type=F32)


def _sigmoid(x):
    return 1.0 / (1.0 + jnp.exp(-x))


def _silu(x):
    return x * _sigmoid(x)


def _layer_norm(z, g, b):
    mu = jnp.mean(z, axis=-1, keepdims=True)
    zc = z - mu
    var = jnp.mean(zc * zc, axis=-1, keepdims=True)
    return zc * lax.rsqrt(var + LN_EPS) * g + b


def _adaln_kernel(c_ref, w_ref, b_ref, o_ref):
    cond = _silu(c_ref[...]).astype(BF16)
    o_ref[0] = _dot(cond, w_ref[0]) + b_ref[0]


def adaln_mod(c, w_ada, b_ada):
    L, D, N = w_ada.shape
    B = c.shape[0]
    tn = D
    return pl.pallas_call(
        _adaln_kernel,
        grid=(L, N // tn),
        in_specs=[
            pl.BlockSpec((B, D), lambda l, j: (0, 0)),
            pl.BlockSpec((1, D, tn), lambda l, j: (l, 0, j)),
            pl.BlockSpec((1, 1, tn), lambda l, j: (l, 0, j)),
        ],
        out_specs=pl.BlockSpec((1, B, tn), lambda l, j: (l, 0, j)),
        out_shape=jax.ShapeDtypeStruct((L, B, N), F32),
        compiler_params=_cparams(2),
        name="adaln_mod",
    )(c, w_ada.astype(BF16), b_ada.reshape(L, 1, N))


N_ROPE = NSA_WIDTH + 3 * NSA_KV_WIDTH
N_NSA = N_ROPE + 3 * NSA_KV_WIDTH
N_GATE = NSA_GROUPS * LANES
N_HGRN = 2 * HGRN_WIDTH + 2 * HGRN_VWIDTH
N_MERGE = 2 * D_MODEL


def _in_proj_kernel(x_ref, sc_ref, sh_ref, cos_ref, sin_ref, lb_ref,
                    wn_ref, bn_ref, wg_ref, bg_ref, wh_ref, bh_ref, wm_ref, bm_ref,
                    q_ref, kx_ref, kv_ref, vt_ref, ga_ref, nrm_ref, hq_ref, hlf_ref, hv_ref, hg_ref, gma_ref, gmb_ref):
    u = (x_ref[0] * (1.0 + sc_ref[0]) + sh_ref[0]).astype(BF16)
    cos = cos_ref[...]
    sin = sin_ref[...]
    lane = lax.broadcasted_iota(jnp.int32, cos.shape, 1)
    first_half = (lane % NSA_DH) < (NSA_DH // 2)
    low = lane < NSA_DH

    def heads(t, upper):
        return jnp.where(low, t, upper), jnp.where(low, pltpu.roll(t, NSA_DH, 1), upper)

    n_q, n_rope = NSA_HEADS // 2, N_ROPE // LANES
    tm = u.shape[0]
    ones_rows = jnp.where(lax.broadcasted_iota(jnp.int32, (VT_ROWS - NSA_DH, tm), 0) == 0, 1.0, 0.0).astype(BF16)

    def max_sq_norm(t, acc):
        n = jnp.max(jnp.sum(t * t, axis=1, keepdims=True), axis=0, keepdims=True)
        return n if acc is None else jnp.maximum(acc, n)

    stat = {"q_sq": None, "k_sq": None}

    def attn_pair(i2):
        t2 = _dot(u, wn_ref[:, i2 * LANES:(i2 + 2) * LANES]) + bn_ref[:, i2 * LANES:(i2 + 2) * LANES]
        for i in (i2, i2 + 1):
            t = t2[:, (i - i2) * LANES:(i - i2 + 1) * LANES]
            if i < n_rope:
                rot = jnp.where(first_half, pltpu.roll(t, LANES - NSA_DH // 2, 1), pltpu.roll(t, NSA_DH // 2, 1))
                t = t * cos + rot * sin
            if i < n_q:
                t = t * Q_SCALE
                stat["q_sq"] = max_sq_norm(t, stat["q_sq"])
                for j, piece in enumerate(heads(t, 0.0)):
                    q_ref[0, 2 * i + j] = piece.astype(BF16)
            elif i == n_q or i == n_rope:
                for j in range(2):
                    kv_ref[0, (2 if i == n_rope else 0) + j] = t[:, j * NSA_DH:(j + 1) * NSA_DH].astype(BF16)
            elif i < n_rope:
                kind = i - n_q - 1
                stat["k_sq"] = max_sq_norm(t, stat["k_sq"])
                for j, piece in enumerate(heads(t, 0.0)):
                    kx_ref[0, 2 * kind + j] = piece.astype(BF16)
            else:
                kind = i - n_rope - 1
                tt = t.T.astype(BF16)
                for j in range(2):
                    vt_ref[0, 2 * kind + j] = jnp.concatenate([tt[j * NSA_DH:(j + 1) * NSA_DH], ones_rows], axis=0)

    def branch_gates():
        gates = _sigmoid(_dot(u, wg_ref[...]) + bg_ref[...])
        for g in range(NSA_GROUPS):
            ga_ref[0, g] = gates[:, g * LANES:(g + 1) * LANES].T[0:2 * SUBLANES]

    W = HGRN_WIDTH
    hgrn_cols = lambda i: _dot(u, wh_ref[:, i * W:(i + 1) * W]) + bh_ref[:, i * W:(i + 1) * W]

    def hgrn_q():
        hq_ref[0] = (_silu(hgrn_cols(0)) * (HGRN_DK ** -0.5)).astype(BF16)

    def hgrn_f():
        z = hgrn_cols(1)
        lb = lb_ref[...]
        log_sig = jnp.minimum(z, 0.0) - jnp.log(1.0 + jnp.exp(-jnp.abs(z)))
        a = jnp.log(lb)
        bb = jnp.log1p(-lb) + log_sig
        log_f = jnp.maximum(a, bb) + jnp.log(1.0 + jnp.exp(-jnp.abs(a - bb)))
        hlf_ref[0] = log_f
        stat["lf_min"] = jnp.min(jnp.min(log_f, axis=1, keepdims=True), axis=0, keepdims=True)

    def hgrn_v():
        hv_ref[0] = hgrn_cols(2).astype(BF16)

    def hgrn_g():
        hg_ref[0] = _silu(hgrn_cols(3)).astype(BF16)

    def merge_gate(k):
        ref, half = (gma_ref, gmb_ref)[k // 2], D_MODEL // 2
        c0 = k * half
        ref[0, :, (k % 2) * half:(k % 2 + 1) * half] = _sigmoid(
            _dot(u, wm_ref[:, c0:c0 + half]) + bm_ref[:, c0:c0 + half]).astype(BF16)

    P = functools.partial
    for group in (P(attn_pair, 0), P(merge_gate, 0), P(attn_pair, 2), P(merge_gate, 1), P(attn_pair, 4), hgrn_q,
                  hgrn_f, P(merge_gate, 2), P(attn_pair, 6), hgrn_g, P(attn_pair, 8), P(merge_gate, 3),
                  branch_gates, hgrn_v):
        group()
    srow = lax.broadcasted_iota(jnp.int32, (SUBLANES, LANES), 0)
    nrm_ref[0, 0] = jnp.where(srow == 0, stat["q_sq"], jnp.where(srow == 1, stat["k_sq"],
                                                                 jnp.where(srow == 2, stat["lf_min"], 0.0)))


def in_proj(x, sc, sh, cos_t, sin_t, lb, wts):
    B, S, D = x.shape
    tm = min(TOKEN_TILE, S)
    assert S // SEL_LEN <= LANES - NSA_DH
    wn, bn, wg, bg, wh, bh, wm, bm = wts
    tok = lambda n: pl.BlockSpec((1, tm, n), lambda b, i: (b, i, 0))
    per_b = pl.BlockSpec((1, 1, D), lambda b, i: (b, 0, 0))
    tab = pl.BlockSpec((tm, LANES), lambda b, i: (i, 0))
    out_shape = (
        jax.ShapeDtypeStruct((B, NSA_HEADS, S, LANES), BF16),
        jax.ShapeDtypeStruct((B, 2 * NSA_GROUPS, S, LANES), BF16),
        jax.ShapeDtypeStruct((B, 2 * NSA_GROUPS, S, NSA_DH), BF16),
        jax.ShapeDtypeStruct((B, 2 * NSA_GROUPS, VT_ROWS, S), BF16),
        jax.ShapeDtypeStruct((B, NSA_GROUPS, 2 * SUBLANES, S), F32),
        jax.ShapeDtypeStruct((B, S // tm, SUBLANES, LANES), F32),
        jax.ShapeDtypeStruct((B, S, HGRN_WIDTH), BF16),
        jax.ShapeDtypeStruct((B, S, HGRN_WIDTH), F32),
        jax.ShapeDtypeStruct((B, S, HGRN_VWIDTH), BF16),
        jax.ShapeDtypeStruct((B, S, HGRN_VWIDTH), BF16),
        jax.ShapeDtypeStruct((B, S, D), BF16),
        jax.ShapeDtypeStruct((B, S, D), BF16),
    )
    out_specs = (
        pl.BlockSpec((1, NSA_HEADS, tm, LANES), lambda b, i: (b, 0, i, 0)),
        pl.BlockSpec((1, 2 * NSA_GROUPS, tm, LANES), lambda b, i: (b, 0, i, 0)),
        pl.BlockSpec((1, 2 * NSA_GROUPS, tm, NSA_DH), lambda b, i: (b, 0, i, 0)),
        pl.BlockSpec((1, 2 * NSA_GROUPS, VT_ROWS, tm), lambda b, i: (b, 0, 0, i)),
        pl.BlockSpec((1, NSA_GROUPS, 2 * SUBLANES, tm), lambda b, i: (b, 0, 0, i)),
        pl.BlockSpec((1, 1, SUBLANES, LANES), lambda b, i: (b, i, 0, 0)),
        tok(HGRN_WIDTH), tok(HGRN_WIDTH), tok(HGRN_VWIDTH), tok(HGRN_VWIDTH), tok(D), tok(D),
    )
    return pl.pallas_call(
        _in_proj_kernel,
        grid=(B, S // tm),
        in_specs=[tok(D), per_b, per_b, tab, tab, _resident(lb.shape),
                  _resident(wn.shape), _resident(bn.shape), _resident(wg.shape), _resident(bg.shape),
                  _resident(wh.shape), _resident(bh.shape), _resident(wm.shape), _resident(bm.shape)],
        out_specs=out_specs,
        out_shape=out_shape,
        compiler_params=_cparams(2),
        name="in_proj",
    )(x, sc, sh, cos_t, sin_t, lb, wn, bn, wg, bg, wh, bh, wm, bm)


def _prep_in_proj_weights(w_in_l, b_in_l):
    o = IN_OFFSETS
    col = lambda i: (w_in_l[:, o[i]:o[i + 1]], b_in_l[o[i]:o[i + 1]])
    q_a, k_c, v_c, k_s, v_s, k_w, v_w, g_a, q_b, f_b, i_b, g_b, gm_a, gm_b = [col(i) for i in range(14)]

    def cat(parts):
        return (jnp.concatenate([p[0] for p in parts], axis=1).astype(BF16),
                jnp.concatenate([p[1] for p in parts], axis=0).reshape(1, -1).astype(F32))

    wn, bn = cat([q_a, k_c, k_s, k_w, v_c, v_s, v_w])
    per_group = 3 * NSA_REP
    gw = jnp.zeros((w_in_l.shape[0], N_GATE), w_in_l.dtype)
    gb = jnp.zeros((N_GATE,), b_in_l.dtype)
    for g in range(NSA_GROUPS):
        gw = gw.at[:, g * LANES:g * LANES + per_group].set(g_a[0][:, g * per_group:(g + 1) * per_group])
        gb = gb.at[g * LANES:g * LANES + per_group].set(g_a[1][g * per_group:(g + 1) * per_group])
    wg, bg = gw.astype(BF16), gb.reshape(1, -1).astype(F32)
    wh, bh = cat([q_b, f_b, i_b, g_b])
    wm, bm = cat([gm_a, gm_b])
    return wn, bn, wg, bg, wh, bh, wm, bm


def _compress_kernel(t_ref, pe_ref, w1_ref, w2_ref, o_ref):
    half = CMP_STRIDE * NSA_DH
    t = t_ref[0, 0]
    nrow = t.shape[0]
    a = _dot(t, w1_ref[0, 0:half, :])
    b = _dot(t, w1_ref[0, half:2 * half, :])
    pe = jnp.broadcast_to(pe_ref[0], (8, 2 * half)).astype(BF16)
    c = _dot(pe, w1_ref[0])[0:1]
    h = a + pltpu.roll(b, nrow - 1, 0) + c
    o_ref[0, 0] = _dot(_silu(h).astype(BF16), w2_ref[0]).astype(BF16)


def nsa_compress(kv, pe, w1, w2):
    B, _, S, dh = kv.shape
    nrow = S // CMP_STRIDE
    G = NSA_GROUPS
    kv_rows = kv.reshape(B, 2 * G, nrow, CMP_STRIDE * dh)
    return pl.pallas_call(
        _compress_kernel,
        grid=(B, 2, G),
        in_specs=[
            pl.BlockSpec((1, 1, nrow, CMP_STRIDE * dh), lambda b, s, g: (b, s * G + g, 0, 0)),
            pl.BlockSpec((1, 1, CMP_LEN * dh), lambda b, s, g: (s, 0, 0)),
            pl.BlockSpec((1, CMP_LEN * dh, CMP_HIDDEN), lambda b, s, g: (s, 0, 0)),
            pl.BlockSpec((1, CMP_HIDDEN, LANES), lambda b, s, g: (s, 0, 0)),
        ],
        out_specs=pl.BlockSpec((1, 1, nrow, LANES), lambda b, s, g: (b, s * G + g, 0, 0)),
        out_shape=jax.ShapeDtypeStruct((B, 2 * G, nrow, LANES), BF16),
        compiler_params=_cparams(3),
        name="nsa_compress",
    )(kv_rows, pe, w1, w2)


def _nsa_kernel(q_ref, cmp_ref, kx_ref, vt_ref, gt_ref, ovt_ref, o_ref, *, n_sel, n_tiles, bounded_scores):
    def tile(qs):
        gens = [_nsa_tile(qs, bi, g, q_ref, cmp_ref, kx_ref, vt_ref, gt_ref, ovt_ref, o_ref, n_sel, bounded_scores)
                for bi in range(q_ref.shape[0]) for g in range(NSA_GROUPS)]
        while gens:
            gens = [gen for gen in gens if next(gen, "done") != "done"]

    for qs in range(n_tiles):
        pl.when(pl.program_id(1) == qs)(functools.partial(tile, qs))


def _round_up(x, m):
    return -(-x // m) * m


def _nsa_tile(qb, bi, g, q_ref, cmp_ref, kx_ref, vt_ref, gt_ref, ovt_ref, o_ref, n_sel, bounded_scores):
    G, R, TQ, dh = NSA_GROUPS, NSA_REP, Q_TILE, NSA_DH
    cols = R * TQ
    s0 = qb * TQ
    kc_ref, vc_ref = cmp_ref.at[bi, g], cmp_ref.at[bi, G + g]
    ks_ref, kw_ref = kx_ref.at[bi, g], kx_ref.at[bi, G + g]
    vst_ref, vwt_ref = vt_ref.at[bi, g], vt_ref.at[bi, G + g]
    q = q_ref[bi, g * R:(g + 1) * R].reshape(cols, LANES)
    t_lane = s0 + (lax.broadcasted_iota(jnp.int32, (1, cols), 1) % TQ)
    kidx = lax.broadcasted_iota(jnp.int32, (K_CHUNK, cols), 0)
    tq = lax.broadcasted_iota(jnp.int32, (K_CHUNK, cols), 1) % TQ
    causal = kidx <= tq

    def branch_blocks(k_ref, vt_ref, c0, n, band_first):
        return [(k_ref, vt_ref, c0 + b0, min(ATT_BLOCK, n - b0), band_first and b0 == 0, b0 + ATT_BLOCK >= n)
                for b0 in range(0, n, ATT_BLOCK)]

    def score_block(blk):
        k_ref, _, c, n, band, diag = blk
        s = _dot_nt(k_ref[c * K_CHUNK:(c + n) * K_CHUNK, :], q)
        parts = [s[i * K_CHUNK:(i + 1) * K_CHUNK] for i in range(n)]
        if band:
            parts[0] = jnp.where(tq < kidx, parts[0], NEG_BIG)
        if diag:
            parts[-1] = jnp.where(causal, parts[-1], NEG_BIG)
        return jnp.concatenate(parts, axis=0) if n > 1 else parts[0]

    def finish_block(s, blk, block_bias=None):
        _, vt_ref, c, n, _, _ = blk
        nk = n * K_CHUNK
        m = None
        if block_bias is not None:
            j0 = c * K_CHUNK // SEL_LEN
            subs = [s[i * SEL_LEN:(i + 1) * SEL_LEN] for i in range(nk // SEL_LEN)]
            bias = [block_bias[j0 + i:j0 + i + 1, :] for i in range(nk // SEL_LEN)]
        if bounded_scores:
            p = jnp.exp2(s) if block_bias is None else jnp.concatenate(
                [jnp.exp2(s_i + b_i) for s_i, b_i in zip(subs, bias)], axis=0)
        elif block_bias is None:
            m = jnp.max(s, axis=0, keepdims=True)
            p = jnp.exp2(s - m)
        else:
            for s_i, b_i in zip(subs, bias):
                m_i = jnp.max(s_i, axis=0, keepdims=True) + b_i
                m = m_i if m is None else jnp.maximum(m, m_i)
            shift = jnp.where(m < 0.5 * NEG_BIG, 0.0, m)
            p = jnp.concatenate([jnp.exp2(s_i + (b_i - shift)) for s_i, b_i in zip(subs, bias)], axis=0)
        vt = vt_ref[:, c * K_CHUNK:c * K_CHUNK + nk]
        return m, _dot(vt, p.astype(BF16))

    def combine(stats):
        total = stats[0][1]
        if bounded_scores:
            for _, acc_i in stats[1:]:
                total = total + acc_i
        elif len(stats) > 1:
            m = stats[0][0]
            for st in stats[1:]:
                m = jnp.maximum(m, st[0])
            total = None
            for m_i, acc_i in stats:
                w = jnp.exp2(m_i - m)
                total = w * acc_i if total is None else total + w * acc_i
        return total[0:dh] * (1.0 / total[dh:dh + 1])

    n_win = WINDOW // K_CHUNK
    win_blocks = branch_blocks(kw_ref, vwt_ref, max(qb - n_win, 0), min(qb, n_win) + 1, qb >= n_win)
    blocks = win_blocks + branch_blocks(ks_ref, vst_ref, 0, qb + 1, False)

    ncb = min(kc_ref.shape[0], _round_up((s0 + TQ - CMP_LEN) // CMP_STRIDE + 1, 2 * SUBLANES))
    sc = _dot_nt(kc_ref[0:ncb, :], q)
    pending = {i: score_block(blocks[i]) for i in range(min(ATT_LOOKAHEAD, len(blocks)))}
    yield
    n_sub = lax.broadcasted_iota(jnp.int32, (ncb, cols), 0)
    mask_c = n_sub * CMP_STRIDE + (CMP_LEN - 1) <= t_lane
    sc = jnp.where(mask_c, sc, NEG_BIG)
    mc = jnp.max(sc, axis=0, keepdims=True)
    ec = jnp.where(mask_c, jnp.exp2(sc - mc), 0.0)
    pc = ec * (1.0 / jnp.maximum(jnp.sum(ec, axis=0, keepdims=True), 1e-30))
    o_c = _dot_tn(vc_ref[0:ncb, :], pc.astype(BF16))[0:dh]

    nb_live = (s0 + TQ) // SEL_LEN
    nb = min(ovt_ref.shape[0], _round_up(nb_live, 2 * SUBLANES))
    psum = pc[:, 0:TQ]
    for r in range(1, R):
        psum = psum + pc[:, r * TQ:(r + 1) * TQ]
    p_hi = psum.astype(BF16)
    p_lo = (psum - p_hi.astype(F32)).astype(BF16)
    ovt = ovt_ref[0:nb, 0:ncb]
    imp = _dot(ovt, p_hi) + _dot(ovt, p_lo)
    yield
    jb =lax.broadcasted_iota(jnp.int32, (nb, TQ), 0)
    tb = (s0 + lax.broadcasted_iota(jnp.int32, (nb, TQ), 1)) // SEL_LEN
    valid = jb <= tb
    forced = jnp.where(valid, jnp.where(jb == 0, 1.0, jnp.where(jb >= tb - 1, 1.0, 0.0)), 0.0)
    score = jnp.where(forced > 0.5, FORCE_SCORE, jnp.where(valid, imp, -1.0))
    rank = jnp.zeros((nb, TQ), F32)
    for i in range(nb_live):
        si = score[i:i + 1, :]
        tie_first = jnp.where(jb > i, 1.0, 0.0)
        rank = rank + jnp.where(si > score, 1.0, jnp.where(si == score, tie_first, 0.0))
    sel_bias = jnp.where(rank < n_sel, 0.0, NEG_BIG)
    sel_bias = jnp.concatenate([sel_bias] * R, axis=1)

    stats = []
    for i, blk in enumerate(blocks):
        is_sel = i >= len(win_blocks)
        stats.append(finish_block(pending.pop(i), blk, sel_bias if is_sel else None))
        if i + ATT_LOOKAHEAD < len(blocks):
            pending[i + ATT_LOOKAHEAD] = score_block(blocks[i + ATT_LOOKAHEAD])
        yield
    o_w = combine(stats[:len(win_blocks)])
    o_s = combine(stats[len(win_blocks):])

    gate = gt_ref[bi, g]
    pieces = []
    for r in range(R):
        sl = slice(r * TQ, (r + 1) * TQ)
        o_r = (gate[3 * r:3 * r + 1, :] * o_c[:, sl] + gate[3 * r + 1:3 * r + 2, :] * o_s[:, sl]
               + gate[3 * r + 2:3 * r + 3, :] * o_w[:, sl])
        pieces.append(o_r.T)
    o_ref[bi, :, g * R * dh:(g + 1) * R * dh] = jnp.concatenate(pieces, axis=1).astype(BF16)


def nsa_attend(q, kx, vt, cmp, gates_t, bounded_scores):
    B, H, S, _ = q.shape
    G, R, dh = NSA_GROUPS, NSA_REP, NSA_DH
    ncb = S // CMP_STRIDE
    nb = S // SEL_LEN
    assert (S % Q_TILE == 0 and Q_TILE == K_CHUNK and WINDOW % K_CHUNK == 0 and K_CHUNK % SEL_LEN == 0
            and 3 * R <= 2 * SUBLANES)
    cstart = np.arange(ncb) * CMP_STRIDE
    sstart = np.arange(nb) * SEL_LEN
    overlap = ((cstart[:, None] < sstart[None, :] + SEL_LEN) & (cstart[:, None] + CMP_LEN > sstart[None, :]))
    ovt = jnp.asarray(overlap.T, BF16)
    nbt = NSA_BATCH if B % NSA_BATCH == 0 else 1
    per_batch = lambda a: pl.BlockSpec((nbt,) + a.shape[1:], lambda b, i: (b, 0, 0, 0))
    return pl.pallas_call(
        functools.partial(_nsa_kernel, n_sel=min(SEL_TOPK, nb), n_tiles=S // Q_TILE, bounded_scores=bounded_scores),
        grid=(B // nbt, S // Q_TILE),
        in_specs=[
            pl.BlockSpec((nbt, H, Q_TILE, LANES), lambda b, i: (b, 0, i, 0)),
            per_batch(cmp), per_batch(kx), per_batch(vt),
            pl.BlockSpec((nbt, G, 2 * SUBLANES, Q_TILE), lambda b, i: (b, 0, 0, i)),
            pl.BlockSpec(ovt.shape, lambda b, i: (0, 0)),
        ],
        out_specs=pl.BlockSpec((nbt, Q_TILE, H * dh), lambda b, i: (b, i, 0)),
        out_shape=jax.ShapeDtypeStruct((B, S, H * dh), BF16),
        compiler_params=_cparams(2),
        name="nsa_attend",
    )(q, cmp, kx, vt, gates_t, ovt)


def _hgrn_kernel(q_ref, lf_ref, v_ref, g_ref, ng_ref, tri_ref, o_ref, st_ref, *, bounded_decay):
    C, SB = HGRN_CHUNK, HGRN_SUB
    NBK = C // SB
    S = q_ref.shape[1]
    st_ref[...] = jnp.zeros(st_ref.shape, F32)
    row = lax.broadcasted_iota(jnp.int32, (C, HGRN_DK), 0)
    sub_row = lax.broadcasted_iota(jnp.int32, (SB, 1), 0)
    cr = lax.broadcasted_iota(jnp.int32, (C, C), 0)
    cc = lax.broadcasted_iota(jnp.int32, (C, C), 1)
    diag_mask = (cr // SB == cc // SB) & (cc <= cr)

    def head_chunk(h, r0):
        hs = slice(h * HGRN_DK, (h + 1) * HGRN_DK)
        q = q_ref[0, pl.ds(r0, C), hs].astype(F32)
        lf = lf_ref[0, pl.ds(r0, C), hs]
        v_bf = v_ref[0, pl.ds(r0, C), hs]
        v = v_bf.astype(F32)
        kh = 1.0 - jnp.exp(lf)
        tri = tri_ref[...]
        lf0 = lf.astype(BF16)
        lf1 = (lf - lf0.astype(F32)).astype(BF16)
        lf2 = (lf - lf0.astype(F32) - lf1.astype(F32)).astype(BF16)
        b = _dot(tri, lf0) + _dot(tri, lf1) + _dot(tri, lf2)
        yield
        b_last = b[C - 1:C, :]
        st = st_ref[h]
        o = _dot_nt((q * jnp.exp(b)).astype(BF16), st.astype(BF16))
        b_end = jnp.concatenate(
            [jnp.broadcast_to(b[(j + 1) * SB - 1:(j + 1) * SB, :], (SB, HGRN_DK)) for j in range(NBK)], axis=0)
        k_end = kh * jnp.exp(b_end - b)
        q_parts, k_parts = [], []
        for j in range(NBK - 1):
            lo = (j + 1) * SB
            qj = q[lo:] * jnp.exp(b[lo:] - b[lo - 1:lo, :])
            q_parts.append(jnp.concatenate([jnp.zeros((lo, HGRN_DK), F32), qj], axis=0))
            k_parts.append(jnp.where((row >= j * SB) & (row < lo), k_end, 0.0))
        q_cat = jnp.concatenate(q_parts, axis=1).astype(BF16)
        k_cat = jnp.concatenate(k_parts, axis=1).astype(BF16)
        a_off = _dot_nt(q_cat, k_cat)
        k_last = (kh * jnp.exp(b_last - b)).astype(BF16)
        st_ref[h] = st * jnp.exp(b_last) + _dot_tn(v_bf, k_last)
        if bounded_decay:
            b_start = jnp.concatenate([jnp.zeros((SB, HGRN_DK), F32), b_end[:C - SB]], axis=0)
            a_dg = _dot_nt((q * jnp.exp(b - b_start)).astype(BF16), (kh * jnp.exp(b_start - b)).astype(BF16))
            yield
            o = o + _dot(jnp.where(diag_mask, a_dg, a_off).astype(BF16), v_bf)
            yield
        else:
            yield
            o = o + _dot(a_off.astype(BF16), v_bf)
            diag = []
            for j in range(NBK):
                sl = slice(j * SB, (j + 1) * SB)
                qj, bj, kj, vj = q[sl], b[sl], kh[sl], v[sl]
                oj = jnp.zeros((SB, HGRN_DV), F32)
                for s in range(SB):
                    w = jnp.exp(jnp.minimum(bj - bj[s:s + 1, :], 0.0))
                    a = jnp.sum(qj * kj[s:s + 1, :] * w, axis=-1, keepdims=True)
                    a = jnp.where(sub_row >= s, a, 0.0)
                    oj = oj + a * vj[s:s + 1, :]
                diag.append(oj)
            o = o + jnp.concatenate(diag, axis=0)
            yield
        o = o * lax.rsqrt(jnp.mean(o * o, axis=-1, keepdims=True) + RMS_EPS) * ng_ref[...]
        o_ref[0, pl.ds(r0, C), hs] = (o * g_ref[0, pl.ds(r0, C), hs].astype(F32)).astype(BF16)

    def chunk(ci, carry):
        r0 = pl.multiple_of(ci * (C * HGRN_UNROLL), C * HGRN_UNROLL)
        heads = [head_chunk(h, r0 + u * C) for u in range(HGRN_UNROLL) for h in range(HGRN_HEADS)]
        for _ in range(3):
            for gen in heads:
                next(gen)
        for gen in heads:
            next(gen, None)
        return carry

    lax.fori_loop(0, S // (C * HGRN_UNROLL), chunk, 0)


def hgrn2(hq, hlf, hv, hg, norm_g, bounded_decay):
    B, S, _ = hq.shape
    C = HGRN_CHUNK
    assert S % (C * HGRN_UNROLL) == 0
    tri = jnp.asarray(np.tril(np.ones((C, C))), BF16)
    spec = lambda n: pl.BlockSpec((1, S, n), lambda b: (b, 0, 0))
    return pl.pallas_call(
        functools.partial(_hgrn_kernel, bounded_decay=bounded_decay),
        grid=(B,),
        in_specs=[spec(HGRN_WIDTH), spec(HGRN_WIDTH), spec(HGRN_VWIDTH), spec(HGRN_VWIDTH),
                  pl.BlockSpec((1, HGRN_DV), lambda b: (0, 0)),
                  pl.BlockSpec((C, C), lambda b: (0, 0))],
        out_specs=spec(HGRN_VWIDTH),
        out_shape=jax.ShapeDtypeStruct((B, S, HGRN_VWIDTH), BF16),
        scratch_shapes=[pltpu.VMEM((HGRN_HEADS, HGRN_DV, HGRN_DK), F32)],
        compiler_params=_cparams(1),
        name="hgrn2",
    )(hq, hlf, hv, hg, norm_g.reshape(1, HGRN_DV).astype(F32), tri)


MLP_COLS = 1024


def _merge_mlp_kernel(ya_ref, yb_ref, gma_ref, gmb_ref, x_ref, mod_ref, ln_ref,
                      wa_ref, wb_ref, wo_ref, w1_ref, w2_ref, o_ref):
    gt1, sh2, sc2, gt2 = [mod_ref[0, i:i + 1, :] for i in range(4)]
    g1, b1, g2, b2 = [ln_ref[i:i + 1, :] for i in range(4)]
    pa = _dot(ya_ref[0], wa_ref[...])
    pb = _dot(yb_ref[0], wb_ref[...])
    merged = gma_ref[0].astype(F32) * pa + gmb_ref[0].astype(F32) * pb
    y = _dot(merged.astype(BF16), wo_ref[...])
    x = _layer_norm(DEEPNORM_ALPHA * x_ref[0] + (1.0 + gt1) * y, g1, b1)
    u = (x * (1.0 + sc2) + sh2).astype(BF16)
    y = jnp.zeros(x.shape, F32)
    for c in range(MLP_HIDDEN // MLP_COLS):
        h = jnp.maximum(_dot(u, w1_ref[:, c * MLP_COLS:(c + 1) * MLP_COLS]), 0.0)
        y = y + _dot((h * h).astype(BF16), w2_ref[c * MLP_COLS:(c + 1) * MLP_COLS, :])
    o_ref[0] = _layer_norm(DEEPNORM_ALPHA * x + (1.0 + gt2) * y, g2, b2)


def merge_mlp(ya, yb, gma, gmb, x, mod4, ln4, wa, wb, wo, w1, w2):
    B, S, D = x.shape
    tm = min(TOKEN_TILE, S)
    tok = lambda n: pl.BlockSpec((1, tm, n), lambda b, i: (b, i, 0))
    return pl.pallas_call(
        _merge_mlp_kernel,
        grid=(B, S // tm),
        in_specs=[tok(NSA_WIDTH), tok(HGRN_VWIDTH), tok(D), tok(D), tok(D),
                  pl.BlockSpec((1, 4, D), lambda b, i: (b, 0, 0)), _resident(ln4.shape),
                  _resident(wa.shape), _resident(wb.shape), _resident(wo.shape),
                  _resident(w1.shape), _resident(w2.shape)],
        out_specs=tok(D),
        out_shape=jax.ShapeDtypeStruct((B, S, D), F32),
        compiler_params=_cparams(2, vmem=MERGE_MLP_VMEM),
        name="merge_mlp",
    )(ya, yb, gma, gmb, x, mod4, ln4, wa, wb, wo, w1, w2)


def _rope_tables(S):
    inv = 1.0 / (ROPE_THETA ** (jnp.arange(0, NSA_DH, 2, dtype=F32) / NSA_DH))
    ang = jnp.arange(S, dtype=F32)[:, None] * inv[None, :]
    cos, sin = jnp.cos(ang), jnp.sin(ang)
    reps = LANES // NSA_DH
    return (jnp.tile(jnp.concatenate([cos, cos], axis=1), (1, reps)),
            jnp.tile(jnp.concatenate([-sin, sin], axis=1), (1, reps)))


def kernel(x, c, w_in, b_in, cmp_pe_k, cmp_pe_v, cmp_wk1, cmp_wk2, cmp_wv1, cmp_wv2, hgrn_lb_logits, hgrn_norm_g, w_branch_a, w_branch_b, w_out, w_ada, b_ada, ln1_g, ln1_b, w_mlp1, w_mlp2, ln2_g, ln2_b):
    B, S, D = x.shape
    G = NSA_GROUPS
    lb_all = jnp.cumsum(jax.nn.softmax(hgrn_lb_logits.astype(F32), axis=0), axis=0)
    lb_all = lb_all - lb_all[0:1]
    cos_t, sin_t = _rope_tables(S)
    mod = adaln_mod(c, w_ada, b_ada)
    for l in range(DEPTH):
        sh1, sc1, gt1, sh2, sc2, gt2 = [mod[l, :, None, i * D:(i + 1) * D] for i in range(6)]
        wts = _prep_in_proj_weights(w_in[l], b_in[l])
        q, kx, kv, vt, gates_t, nrm, hq, hlf, hv, hg, gma, gmb = in_proj(x, sc1, sh1, cos_t, sin_t, lb_all[l].reshape(1, -1), wts)
        pe = jnp.stack([cmp_pe_k[l].reshape(1, -1), cmp_pe_v[l].reshape(1, -1)])
        w1 = jnp.stack([cmp_wk1[l], cmp_wv1[l]]).astype(BF16)
        w2 = jnp.pad(jnp.stack([cmp_wk2[l], cmp_wv2[l]]), ((0, 0), (0, 0), (0, LANES - NSA_DH))).astype(BF16)
        cmp = nsa_compress(kv, pe, w1, w2)
        bounded = jnp.max(nrm[:, :, 0, 0]) * jnp.max(nrm[:, :, 1, 0]) <= SCORE_BOUND ** 2
        ya = lax.cond(bounded,
                      functools.partial(nsa_attend, bounded_scores=True),
                      functools.partial(nsa_attend, bounded_scores=False),
                      q, kx, vt, cmp, gates_t)
        yb = lax.cond(jnp.min(nrm[:, :, 2, 0]) >= -HGRN_MAX_STEP_DECAY,
                      functools.partial(hgrn2, bounded_decay=True),
                      functools.partial(hgrn2, bounded_decay=False),
                      hq, hlf, hv, hg, hgrn_norm_g[l])
        mod4 = jnp.concatenate([gt1, sh2, sc2, gt2], axis=1)
        ln4 = jnp.stack([ln1_g[l], ln1_b[l], ln2_g[l], ln2_b[l]])
        x = merge_mlp(ya, yb, gma, gmb, x, mod4, ln4,
                      w_branch_a[l].astype(BF16), w_branch_b[l].astype(BF16), w_out[l].astype(BF16),
                      w_mlp1[l].astype(BF16), w_mlp2[l].astype(BF16))
    return x
```

```python
import functools

import numpy as np
import jax
import jax.numpy as jnp
from jax import lax
from jax.experimental import pallas as pl
from jax.experimental.pallas import tpu as pltpu

D_MODEL = 1024
DEPTH = 2
NSA_HEADS = 8
NSA_GROUPS = 2
NSA_REP = NSA_HEADS // NSA_GROUPS
NSA_DH = 64
NSA_WIDTH = NSA_HEADS * NSA_DH
NSA_KV_WIDTH = NSA_GROUPS * NSA_DH
CMP_LEN = 32
CMP_STRIDE = 16
CMP_HIDDEN = 2 * NSA_DH
SEL_LEN = 64
SEL_TOPK = 8
FORCE_SCORE = 1.0e4
WINDOW = 512
HGRN_HEADS = 4
HGRN_DK = 128
HGRN_DV = 128
HGRN_WIDTH = HGRN_HEADS * HGRN_DK
HGRN_VWIDTH = HGRN_HEADS * HGRN_DV
MLP_HIDDEN = 4 * D_MODEL
ROPE_THETA = 10000.0
LN_EPS = 1e-5
RMS_EPS = 1e-6
DEEPNORM_ALPHA = (2 * DEPTH) ** 0.25
IN_SIZES = (NSA_WIDTH,) + (NSA_KV_WIDTH,) * 6 + (3 * NSA_HEADS,) + (HGRN_WIDTH, HGRN_WIDTH, HGRN_VWIDTH, HGRN_VWIDTH) + (D_MODEL, D_MODEL)
IN_OFFSETS = [0] + [int(v) for v in np.cumsum(IN_SIZES)]

LANES = 128
SUBLANES = 8
VMEM_LIMIT = 48 * 1024 * 1024
MERGE_MLP_VMEM = 56 * 1024 * 1024
TOKEN_TILE = 512
Q_TILE = 128
K_CHUNK = 128
ATT_BLOCK = 4
NSA_BATCH = 2
ATT_LOOKAHEAD = 2
HGRN_CHUNK = 64
HGRN_UNROLL = 4
HGRN_SUB = 8
HGRN_MAX_STEP_DECAY = 7.5
NEG_BIG = -1e30
LOG2E = 1.4426950408889634
Q_SCALE = NSA_DH ** -0.5 * LOG2E
SCORE_BOUND = 96.0
VT_ROWS = NSA_DH + 16

F32 = jnp.float32
BF16 = jnp.bfloat16


def _cparams(n_grid, vmem=VMEM_LIMIT):
    return pltpu.CompilerParams(dimension_semantics=("arbitrary",) * n_grid, vmem_limit_bytes=vmem)


def _resident(shape):
    nd = len(shape)
    return pl.BlockSpec(shape, lambda *_: (0,) * nd, pipeline_mode=pl.Buffered(1))


def _dot(a, b):
    return jnp.dot(a, b, preferred_element_type=F32)


def _dot_nt(a, b):
    return lax.dot_general(a, b, (((1,), (1,)), ((), ())), preferred_element_type=F32)


def _dot_tn(a, b):
    return lax.dot_general(a, b, (((0,), (0,)), ((), ())), preferred_element_type=F32)


def _sigmoid(x):
    return 1.0 / (1.0 + jnp.exp(-x))


def _silu(x):
    return x * _sigmoid(x)


def _layer_norm(z, g, b):
    mu = jnp.mean(z, axis=-1, keepdims=True)
    zc = z - mu
    var = jnp.mean(zc * zc, axis=-1, keepdims=True)
    return zc * lax.rsqrt(var + LN_EPS) * g + b


def _adaln_kernel(c_ref, w_ref, b_ref, o_ref):
    cond = _silu(c_ref[...]).astype(BF16)
    o_ref[0] = _dot(cond, w_ref[0]) + b_ref[0]


def adaln_mod(c, w_ada, b_ada):
    L, D, N = w_ada.shape
    B = c.shape[0]
    tn = D
    return pl.pallas_call(
        _adaln_kernel,
        grid=(L, N // tn),
        in_specs=[
            pl.BlockSpec((B, D), lambda l, j: (0, 0)),
            pl.BlockSpec((1, D, tn), lambda l, j: (l, 0, j)),
            pl.BlockSpec((1, 1, tn), lambda l, j: (l, 0, j)),
        ],
        out_specs=pl.BlockSpec((1, B, tn), lambda l, j: (l, 0, j)),
        out_shape=jax.ShapeDtypeStruct((L, B, N), F32),
        compiler_params=_cparams(2),
        name="adaln_mod",
    )(c, w_ada.astype(BF16), b_ada.reshape(L, 1, N))


N_ROPE = NSA_WIDTH + 3 * NSA_KV_WIDTH
N_NSA = N_ROPE + 3 * NSA_KV_WIDTH
N_GATE = NSA_GROUPS * LANES
N_HGRN = 2 * HGRN_WIDTH + 2 * HGRN_VWIDTH
N_MERGE = 2 * D_MODEL


def _in_proj_kernel(x_ref, sc_ref, sh_ref, cos_ref, sin_ref, lb_ref,
                    wn_ref, bn_ref, wg_ref, bg_ref, wh_ref, bh_ref, wm_ref, bm_ref,
                    q_ref, kx_ref, kv_ref, vt_ref, ga_ref, nrm_ref, hq_ref, hlf_ref, hv_ref, hg_ref, gma_ref, gmb_ref):
    u = (x_ref[0] * (1.0 + sc_ref[0]) + sh_ref[0]).astype(BF16)
    cos = cos_ref[...]
    sin = sin_ref[...]
    lane = lax.broadcasted_iota(jnp.int32, cos.shape, 1)
    first_half = (lane % NSA_DH) < (NSA_DH // 2)
    low = lane < NSA_DH

    def heads(t, upper):
        return jnp.where(low, t, upper), jnp.where(low, pltpu.roll(t, NSA_DH, 1), upper)

    n_q, n_rope = NSA_HEADS // 2, N_ROPE // LANES
    tm = u.shape[0]
    ones_rows = jnp.where(lax.broadcasted_iota(jnp.int32, (VT_ROWS - NSA_DH, tm), 0) == 0, 1.0, 0.0).astype(BF16)

    def max_sq_norm(t, acc):
        n = jnp.max(jnp.sum(t * t, axis=1, keepdims=True), axis=0, keepdims=True)
        return n if acc is None else jnp.maximum(acc, n)

    stat = {"q_sq": None, "k_sq": None}

    def attn_pair(i2):
        t2 = _dot(u, wn_ref[:, i2 * LANES:(i2 + 2) * LANES]) + bn_ref[:, i2 * LANES:(i2 + 2) * LANES]
        for i in (i2, i2 + 1):
            t = t2[:, (i - i2) * LANES:(i - i2 + 1) * LANES]
            if i < n_rope:
                rot = jnp.where(first_half, pltpu.roll(t, LANES - NSA_DH // 2, 1), pltpu.roll(t, NSA_DH // 2, 1))
                t = t * cos + rot * sin
            if i < n_q:
                t = t * Q_SCALE
                stat["q_sq"] = max_sq_norm(t, stat["q_sq"])
                for j, piece in enumerate(heads(t, 0.0)):
                    q_ref[0, 2 * i + j] = piece.astype(BF16)
            elif i == n_q or i == n_rope:
                for j in range(2):
                    kv_ref[0, (2 if i == n_rope else 0) + j] = t[:, j * NSA_DH:(j + 1) * NSA_DH].astype(BF16)
            elif i < n_rope:
                kind = i - n_q - 1
                stat["k_sq"] = max_sq_norm(t, stat["k_sq"])
                for j, piece in enumerate(heads(t, 0.0)):
                    kx_ref[0, 2 * kind + j] = piece.astype(BF16)
            else:
                kind = i - n_rope - 1
                tt = t.T.astype(BF16)
                for j in range(2):
                    vt_ref[0, 2 * kind + j] = jnp.concatenate([tt[j * NSA_DH:(j + 1) * NSA_DH], ones_rows], axis=0)

    def branch_gates():
        gates = _sigmoid(_dot(u, wg_ref[...]) + bg_ref[...])
        for g in range(NSA_GROUPS):
            ga_ref[0, g] = gates[:, g * LANES:(g + 1) * LANES].T[0:2 * SUBLANES]

    W = HGRN_WIDTH
    hgrn_cols = lambda i: _dot(u, wh_ref[:, i * W:(i + 1) * W]) + bh_ref[:, i * W:(i + 1) * W]

    def hgrn_q():
        hq_ref[0] = (_silu(hgrn_cols(0)) * (HGRN_DK ** -0.5)).astype(BF16)

    def hgrn_f():
        z = hgrn_cols(1)
        lb = lb_ref[...]
        log_sig = jnp.minimum(z, 0.0) - jnp.log(1.0 + jnp.exp(-jnp.abs(z)))
        a = jnp.log(lb)
        bb = jnp.log1p(-lb) + log_sig
        log_f = jnp.maximum(a, bb) + jnp.log(1.0 + jnp.exp(-jnp.abs(a - bb)))
        hlf_ref[0] = log_f
        stat["lf_min"] = jnp.min(jnp.min(log_f, axis=1, keepdims=True), axis=0, keepdims=True)

    def hgrn_v():
        hv_ref[0] = hgrn_cols(2).astype(BF16)

    def hgrn_g():
        hg_ref[0] = _silu(hgrn_cols(3)).astype(BF16)

    def merge_gate(k):
        ref, half = (gma_ref, gmb_ref)[k // 2], D_MODEL // 2
        c0 = k * half
        ref[0, :, (k % 2) * half:(k % 2 + 1) * half] = _sigmoid(
            _dot(u, wm_ref[:, c0:c0 + half]) + bm_ref[:, c0:c0 + half]).astype(BF16)

    P = functools.partial
    for group in (P(attn_pair, 0), P(merge_gate, 0), P(attn_pair, 2), P(merge_gate, 1), P(attn_pair, 4), hgrn_q,
                  hgrn_f, P(merge_gate, 2), P(attn_pair, 6), hgrn_g, P(attn_pair, 8), P(merge_gate, 3),
                  branch_gates, hgrn_v):
        group()
    srow = lax.broadcasted_iota(jnp.int32, (SUBLANES, LANES), 0)
    nrm_ref[0, 0] = jnp.where(srow == 0, stat["q_sq"], jnp.where(srow == 1, stat["k_sq"],
                                                                 jnp.where(srow == 2, stat["lf_min"], 0.0)))


def in_proj(x, sc, sh, cos_t, sin_t, lb, wts):
    B, S, D = x.shape
    tm = min(TOKEN_TILE, S)
    assert S // SEL_LEN <= LANES - NSA_DH
    wn, bn, wg, bg, wh, bh, wm, bm = wts
    tok = lambda n: pl.BlockSpec((1, tm, n), lambda b, i: (b, i, 0))
    per_b = pl.BlockSpec((1, 1, D), lambda b, i: (b, 0, 0))
    tab = pl.BlockSpec((tm, LANES), lambda b, i: (i, 0))
    out_shape = (
        jax.ShapeDtypeStruct((B, NSA_HEADS, S, LANES), BF16),
        jax.ShapeDtypeStruct((B, 2 * NSA_GROUPS, S, LANES), BF16),
        jax.ShapeDtypeStruct((B, 2 * NSA_GROUPS, S, NSA_DH), BF16),
        jax.ShapeDtypeStruct((B, 2 * NSA_GROUPS, VT_ROWS, S), BF16),
        jax.ShapeDtypeStruct((B, NSA_GROUPS, 2 * SUBLANES, S), F32),
        jax.ShapeDtypeStruct((B, S // tm, SUBLANES, LANES), F32),
        jax.ShapeDtypeStruct((B, S, HGRN_WIDTH), BF16),
        jax.ShapeDtypeStruct((B, S, HGRN_WIDTH), F32),
        jax.ShapeDtypeStruct((B, S, HGRN_VWIDTH), BF16),
        jax.ShapeDtypeStruct((B, S, HGRN_VWIDTH), BF16),
        jax.ShapeDtypeStruct((B, S, D), BF16),
        jax.ShapeDtypeStruct((B, S, D), BF16),
    )
    out_specs = (
        pl.BlockSpec((1, NSA_HEADS, tm, LANES), lambda b, i: (b, 0, i, 0)),
        pl.BlockSpec((1, 2 * NSA_GROUPS, tm, LANES), lambda b, i: (b, 0, i, 0)),
        pl.BlockSpec((1, 2 * NSA_GROUPS, tm, NSA_DH), lambda b, i: (b, 0, i, 0)),
        pl.BlockSpec((1, 2 * NSA_GROUPS, VT_ROWS, tm), lambda b, i: (b, 0, 0, i)),
        pl.BlockSpec((1, NSA_GROUPS, 2 * SUBLANES, tm), lambda b, i: (b, 0, 0, i)),
        pl.BlockSpec((1, 1, SUBLANES, LANES), lambda b, i: (b, i, 0, 0)),
        tok(HGRN_WIDTH), tok(HGRN_WIDTH), tok(HGRN_VWIDTH), tok(HGRN_VWIDTH), tok(D), tok(D),
    )
    return pl.pallas_call(
        _in_proj_kernel,
        grid=(B, S // tm),
        in_specs=[tok(D), per_b, per_b, tab, tab, _resident(lb.shape),
                  _resident(wn.shape), _resident(bn.shape), _resident(wg.shape), _resident(bg.shape),
                  _resident(wh.shape), _resident(bh.shape), _resident(wm.shape), _resident(bm.shape)],
        out_specs=out_specs,
        out_shape=out_shape,
        compiler_params=_cparams(2),
        name="in_proj",
    )(x, sc, sh, cos_t, sin_t, lb, wn, bn, wg, bg, wh, bh, wm, bm)


def _prep_in_proj_weights(w_in_l, b_in_l):
    o = IN_OFFSETS
    col = lambda i: (w_in_l[:, o[i]:o[i + 1]], b_in_l[o[i]:o[i + 1]])
    q_a, k_c, v_c, k_s, v_s, k_w, v_w, g_a, q_b, f_b, i_b, g_b, gm_a, gm_b = [col(i) for i in range(14)]

    def cat(parts):
        return (jnp.concatenate([p[0] for p in parts], axis=1).astype(BF16),
                jnp.concatenate([p[1] for p in parts], axis=0).reshape(1, -1).astype(F32))

    wn, bn = cat([q_a, k_c, k_s, k_w, v_c, v_s, v_w])
    per_group = 3 * NSA_REP
    gw = jnp.zeros((w_in_l.shape[0], N_GATE), w_in_l.dtype)
    gb = jnp.zeros((N_GATE,), b_in_l.dtype)
    for g in range(NSA_GROUPS):
        gw = gw.at[:, g * LANES:g * LANES + per_group].set(g_a[0][:, g * per_group:(g + 1) * per_group])
        gb = gb.at[g * LANES:g * LANES + per_group].set(g_a[1][g * per_group:(g + 1) * per_group])
    wg, bg = gw.astype(BF16), gb.reshape(1, -1).astype(F32)
    wh, bh = cat([q_b, f_b, i_b, g_b])
    wm, bm = cat([gm_a, gm_b])
    return wn, bn, wg, bg, wh, bh, wm, bm


def _compress_kernel(t_ref, pe_ref, w1_ref, w2_ref, o_ref):
    half = CMP_STRIDE * NSA_DH
    for s in range(2):
        pe = jnp.broadcast_to(pe_ref[s], (8, 2 * half)).astype(BF16)
        c = _dot(pe, w1_ref[s])[0:1]
        for g in range(NSA_GROUPS):
            t = t_ref[0, s * NSA_GROUPS + g]
            nrow = t.shape[0]
            a = _dot(t, w1_ref[s, 0:half, :])
            b = _dot(t, w1_ref[s, half:2 * half, :])
            h = a + pltpu.roll(b, nrow - 1, 0) + c
            o_ref[0, s * NSA_GROUPS + g] = _dot(_silu(h).astype(BF16), w2_ref[s]).astype(BF16)


def nsa_compress(kv, pe, w1, w2):
    B, _, S, dh = kv.shape
    nrow = S // CMP_STRIDE
    G = NSA_GROUPS
    kv_rows = kv.reshape(B, 2 * G, nrow, CMP_STRIDE * dh)
    return pl.pallas_call(
        _compress_kernel,
        grid=(B,),
        in_specs=[
            pl.BlockSpec((1, 2 * G, nrow, CMP_STRIDE * dh), lambda b: (b, 0, 0, 0)),
            _resident(pe.shape), _resident(w1.shape), _resident(w2.shape),
        ],
        out_specs=pl.BlockSpec((1, 2 * G, nrow, LANES), lambda b: (b, 0, 0, 0)),
        out_shape=jax.ShapeDtypeStruct((B, 2 * G, nrow, LANES), BF16),
        compiler_params=_cparams(1),
        name="nsa_compress",
    )(kv_rows, pe, w1, w2)


def _nsa_kernel(q_ref, cmp_ref, kx_ref, vt_ref, gt_ref, ovt_ref, o_ref, *, n_sel, n_tiles, bounded_scores):
    def tile(qs):
        gens = [_nsa_tile(qs, bi, g, q_ref, cmp_ref, kx_ref, vt_ref, gt_ref, ovt_ref, o_ref, n_sel, bounded_scores)
                for bi in range(q_ref.shape[0]) for g in range(NSA_GROUPS)]
        while gens:
            gens = [gen for gen in gens if next(gen, "done") != "done"]

    for qs in range(n_tiles):
        pl.when(pl.program_id(1) == qs)(functools.partial(tile, qs))


def _round_up(x, m):
    return -(-x // m) * m


def _nsa_tile(qb, bi, g, q_ref, cmp_ref, kx_ref, vt_ref, gt_ref, ovt_ref, o_ref, n_sel, bounded_scores):
    G, R, TQ, dh = NSA_GROUPS, NSA_REP, Q_TILE, NSA_DH
    cols = R * TQ
    s0 = qb * TQ
    kc_ref, vc_ref = cmp_ref.at[bi, g], cmp_ref.at[bi, G + g]
    ks_ref, kw_ref = kx_ref.at[bi, g], kx_ref.at[bi, G + g]
    vst_ref, vwt_ref = vt_ref.at[bi, g], vt_ref.at[bi, G + g]
    q = q_ref[bi, g * R:(g + 1) * R].reshape(cols, LANES)
    t_lane = s0 + (lax.broadcasted_iota(jnp.int32, (1, cols), 1) % TQ)
    kidx = lax.broadcasted_iota(jnp.int32, (K_CHUNK, cols), 0)
    tq = lax.broadcasted_iota(jnp.int32, (K_CHUNK, cols), 1) % TQ
    causal = kidx <= tq

    def branch_blocks(k_ref, vt_ref, c0, n, band_first):
        return [(k_ref, vt_ref, c0 + b0, min(ATT_BLOCK, n - b0), band_first and b0 == 0, b0 + ATT_BLOCK >= n)
                for b0 in range(0, n, ATT_BLOCK)]

    def score_block(blk):
        k_ref, _, c, n, band, diag = blk
        s = _dot_nt(k_ref[c * K_CHUNK:(c + n) * K_CHUNK, :], q)
        parts = [s[i * K_CHUNK:(i + 1) * K_CHUNK] for i in range(n)]
        if band:
            parts[0] = jnp.where(tq < kidx, parts[0], NEG_BIG)
        if diag:
            parts[-1] = jnp.where(causal, parts[-1], NEG_BIG)
        return jnp.concatenate(parts, axis=0) if n > 1 else parts[0]

    def finish_block(s, blk, block_bias=None):
        _, vt_ref, c, n, _, _ = blk
        nk = n * K_CHUNK
        m = None
        if block_bias is not None:
            j0 = c * K_CHUNK // SEL_LEN
            subs = [s[i * SEL_LEN:(i + 1) * SEL_LEN] for i in range(nk // SEL_LEN)]
            bias = [block_bias[j0 + i:j0 + i + 1, :] for i in range(nk // SEL_LEN)]
        if bounded_scores:
            p = jnp.exp2(s) if block_bias is None else jnp.concatenate(
                [jnp.exp2(s_i + b_i) for s_i, b_i in zip(subs, bias)], axis=0)
        elif block_bias is None:
            m = jnp.max(s, axis=0, keepdims=True)
            p = jnp.exp2(s - m)
        else:
            for s_i, b_i in zip(subs, bias):
                m_i = jnp.max(s_i, axis=0, keepdims=True) + b_i
                m = m_i if m is None else jnp.maximum(m, m_i)
            shift = jnp.where(m < 0.5 * NEG_BIG, 0.0, m)
            p = jnp.concatenate([jnp.exp2(s_i + (b_i - shift)) for s_i, b_i in zip(subs, bias)], axis=0)
        vt = vt_ref[:, c * K_CHUNK:c * K_CHUNK + nk]
        return m, _dot(vt, p.astype(BF16))

    def combine(stats):
        total = stats[0][1]
        if bounded_scores:
            for _, acc_i in stats[1:]:
                total = total + acc_i
        elif len(stats) > 1:
            m = stats[0][0]
            for st in stats[1:]:
                m = jnp.maximum(m, st[0])
            total = None
            for m_i, acc_i in stats:
                w = jnp.exp2(m_i - m)
                total = w * acc_i if total is None else total + w * acc_i
        return total[0:dh] * (1.0 / total[dh:dh + 1])

    n_win = WINDOW // K_CHUNK
    win_blocks = branch_blocks(kw_ref, vwt_ref, max(qb - n_win, 0), min(qb, n_win) + 1, qb >= n_win)
    blocks = win_blocks + branch_blocks(ks_ref, vst_ref, 0, qb + 1, False)

    ncb = min(kc_ref.shape[0], _round_up((s0 + TQ - CMP_LEN) // CMP_STRIDE + 1, 2 * SUBLANES))
    sc = _dot_nt(kc_ref[0:ncb, :], q)
    pending = {i: score_block(blocks[i]) for i in range(min(ATT_LOOKAHEAD, len(blocks)))}
    yield
    n_sub = lax.broadcasted_iota(jnp.int32, (ncb, cols), 0)
    mask_c = n_sub * CMP_STRIDE + (CMP_LEN - 1) <= t_lane
    sc = jnp.where(mask_c, sc, NEG_BIG)
    mc = jnp.max(sc, axis=0, keepdims=True)
    ec = jnp.where(mask_c, jnp.exp2(sc - mc), 0.0)
    pc = ec * (1.0 / jnp.maximum(jnp.sum(ec, axis=0, keepdims=True), 1e-30))
    o_c = _dot_tn(vc_ref[0:ncb, :], pc.astype(BF16))[0:dh]

    nb_live = (s0 + TQ) // SEL_LEN
    nb = min(ovt_ref.shape[0], _round_up(nb_live, 2 * SUBLANES))
    psum = pc[:, 0:TQ]
    for r in range(1, R):
        psum = psum + pc[:, r * TQ:(r + 1) * TQ]
    p_hi = psum.astype(BF16)
    p_lo = (psum - p_hi.astype(F32)).astype(BF16)
    ovt = ovt_ref[0:nb, 0:ncb]
    imp = _dot(ovt, p_hi) + _dot(ovt, p_lo)
    yield
    jb =lax.broadcasted_iota(jnp.int32, (nb, TQ), 0)
    tb = (s0 + lax.broadcasted_iota(jnp.int32, (nb, TQ), 1)) // SEL_LEN
    valid = jb <= tb
    forced = jnp.where(valid, jnp.where(jb == 0, 1.0, jnp.where(jb >= tb - 1, 1.0, 0.0)), 0.0)
    score = jnp.where(forced > 0.5, FORCE_SCORE, jnp.where(valid, imp, -1.0))
    rank = jnp.zeros((nb, TQ), F32)
    for i in range(nb_live):
        si = score[i:i + 1, :]
        tie_first = jnp.where(jb > i, 1.0, 0.0)
        rank = rank + jnp.where(si > score, 1.0, jnp.where(si == score, tie_first, 0.0))
    sel_bias = jnp.where(rank < n_sel, 0.0, NEG_BIG)
    sel_bias = jnp.concatenate([sel_bias] * R, axis=1)

    stats = []
    for i, blk in enumerate(blocks):
        is_sel = i >= len(win_blocks)
        stats.append(finish_block(pending.pop(i), blk, sel_bias if is_sel else None))
        if i + ATT_LOOKAHEAD < len(blocks):
            pending[i + ATT_LOOKAHEAD] = score_block(blocks[i + ATT_LOOKAHEAD])
        yield
    o_w = combine(stats[:len(win_blocks)])
    o_s = combine(stats[len(win_blocks):])

    gate = gt_ref[bi, g]
    pieces = []
    for r in range(R):
        sl = slice(r * TQ, (r + 1) * TQ)
        o_r = (gate[3 * r:3 * r + 1, :] * o_c[:, sl] + gate[3 * r + 1:3 * r + 2, :] * o_s[:, sl]
               + gate[3 * r + 2:3 * r + 3, :] * o_w[:, sl])
        pieces.append(o_r.T)
    o_ref[bi, :, g * R * dh:(g + 1) * R * dh] = jnp.concatenate(pieces, axis=1).astype(BF16)


def nsa_attend(q, kx, vt, cmp, gates_t, bounded_scores):
    B, H, S, _ = q.shape
    G, R, dh = NSA_GROUPS, NSA_REP, NSA_DH
    ncb = S // CMP_STRIDE
    nb = S // SEL_LEN
    assert (S % Q_TILE == 0 and Q_TILE == K_CHUNK and WINDOW % K_CHUNK == 0 and K_CHUNK % SEL_LEN == 0
            and 3 * R <= 2 * SUBLANES)
    cstart = np.arange(ncb) * CMP_STRIDE
    sstart = np.arange(nb) * SEL_LEN
    overlap = ((cstart[:, None] < sstart[None, :] + SEL_LEN) & (cstart[:, None] + CMP_LEN > sstart[None, :]))
    ovt = jnp.asarray(overlap.T, BF16)
    nbt = NSA_BATCH if bounded_scores and B % NSA_BATCH == 0 else 1
    per_batch = lambda a: pl.BlockSpec((nbt,) + a.shape[1:], lambda b, i: (b, 0, 0, 0))
    return pl.pallas_call(
        functools.partial(_nsa_kernel, n_sel=min(SEL_TOPK, nb), n_tiles=S // Q_TILE, bounded_scores=bounded_scores),
        grid=(B // nbt, S // Q_TILE),
        in_specs=[
            pl.BlockSpec((nbt, H, Q_TILE, LANES), lambda b, i: (b, 0, i, 0)),
            per_batch(cmp), per_batch(kx), per_batch(vt),
            pl.BlockSpec((nbt, G, 2 * SUBLANES, Q_TILE), lambda b, i: (b, 0, 0, i)),
            pl.BlockSpec(ovt.shape, lambda b, i: (0, 0)),
        ],
        out_specs=pl.BlockSpec((nbt, Q_TILE, H * dh), lambda b, i: (b, i, 0)),
        out_shape=jax.ShapeDtypeStruct((B, S, H * dh), BF16),
        compiler_params=_cparams(2),
        name="nsa_attend",
    )(q, cmp, kx, vt, gates_t, ovt)


def _hgrn_kernel(q_ref, lf_ref, v_ref, g_ref, ng_ref, tri_ref, o_ref, st_ref, *, bounded_decay):
    C, SB = HGRN_CHUNK, HGRN_SUB
    NBK = C // SB
    S = q_ref.shape[1]
    st_ref[...] = jnp.zeros(st_ref.shape, F32)
    row = lax.broadcasted_iota(jnp.int32, (C, HGRN_DK), 0)
    sub_row = lax.broadcasted_iota(jnp.int32, (SB, 1), 0)
    cr = lax.broadcasted_iota(jnp.int32, (C, C), 0)
    cc = lax.broadcasted_iota(jnp.int32, (C, C), 1)
    diag_mask = (cr // SB == cc // SB) & (cc <= cr)

    def head_chunk(h, r0):
        hs = slice(h * HGRN_DK, (h + 1) * HGRN_DK)
        q = q_ref[0, pl.ds(r0, C), hs].astype(F32)
        lf = lf_ref[0, pl.ds(r0, C), hs]
        v_bf = v_ref[0, pl.ds(r0, C), hs]
        v = v_bf.astype(F32)
        kh = 1.0 - jnp.exp(lf)
        tri = tri_ref[...]
        lf0 = lf.astype(BF16)
        lf1 = (lf - lf0.astype(F32)).astype(BF16)
        lf2 = (lf - lf0.astype(F32) - lf1.astype(F32)).astype(BF16)
        b = _dot(tri, lf0) + _dot(tri, lf1) + _dot(tri, lf2)
        yield
        b_last = b[C - 1:C, :]
        st = st_ref[h]
        o = _dot_nt((q * jnp.exp(b)).astype(BF16), st.astype(BF16))
        b_end = jnp.concatenate(
            [jnp.broadcast_to(b[(j + 1) * SB - 1:(j + 1) * SB, :], (SB, HGRN_DK)) for j in range(NBK)], axis=0)
        k_end = kh * jnp.exp(b_end - b)
        q_parts, k_parts = [], []
        for j in range(NBK - 1):
            lo = (j + 1) * SB
            qj = q[lo:] * jnp.exp(b[lo:] - b[lo - 1:lo, :])
            q_parts.append(jnp.concatenate([jnp.zeros((lo, HGRN_DK), F32), qj], axis=0))
            k_parts.append(jnp.where((row >= j * SB) & (row < lo), k_end, 0.0))
        q_cat = jnp.concatenate(q_parts, axis=1).astype(BF16)
        k_cat = jnp.concatenate(k_parts, axis=1).astype(BF16)
        a_off = _dot_nt(q_cat, k_cat)
        k_last = (kh * jnp.exp(b_last - b)).astype(BF16)
        st_ref[h] = st * jnp.exp(b_last) + _dot_tn(v_bf, k_last)
        if bounded_decay:
            b_start = jnp.concatenate([jnp.zeros((SB, HGRN_DK), F32), b_end[:C - SB]], axis=0)
            a_dg = _dot_nt((q * jnp.exp(b - b_start)).astype(BF16), (kh * jnp.exp(b_start - b)).astype(BF16))
            yield
            o = o + _dot(jnp.where(diag_mask, a_dg, a_off).astype(BF16), v_bf)
            yield
        else:
            yield
            o = o + _dot(a_off.astype(BF16), v_bf)
            diag = []
            for j in range(NBK):
                sl = slice(j * SB, (j + 1) * SB)
                qj, bj, kj, vj = q[sl], b[sl], kh[sl], v[sl]
                oj = jnp.zeros((SB, HGRN_DV), F32)
                for s in range(SB):
                    w = jnp.exp(jnp.minimum(bj - bj[s:s + 1, :], 0.0))
                    a = jnp.sum(qj * kj[s:s + 1, :] * w, axis=-1, keepdims=True)
                    a = jnp.where(sub_row >= s, a, 0.0)
                    oj = oj + a * vj[s:s + 1, :]
                diag.append(oj)
            o = o + jnp.concatenate(diag, axis=0)
            yield
        o = o * lax.rsqrt(jnp.mean(o * o, axis=-1, keepdims=True) + RMS_EPS) * ng_ref[...]
        o_ref[0, pl.ds(r0, C), hs] = (o * g_ref[0, pl.ds(r0, C), hs].astype(F32)).astype(BF16)

    def chunk(ci, carry):
        r0 = pl.multiple_of(ci * (C * HGRN_UNROLL), C * HGRN_UNROLL)
        heads = [head_chunk(h, r0 + u * C) for u in range(HGRN_UNROLL) for h in range(HGRN_HEADS)]
        for _ in range(3):
            for gen in heads:
                next(gen)
        for gen in heads:
            next(gen, None)
        return carry

    lax.fori_loop(0, S // (C * HGRN_UNROLL), chunk, 0)


def hgrn2(hq, hlf, hv, hg, norm_g, bounded_decay):
    B, S, _ = hq.shape
    C = HGRN_CHUNK
    assert S % (C * HGRN_UNROLL) == 0
    tri = jnp.asarray(np.tril(np.ones((C, C))), BF16)
    spec = lambda n: pl.BlockSpec((1, S, n), lambda b: (b, 0, 0))
    return pl.pallas_call(
        functools.partial(_hgrn_kernel, bounded_decay=bounded_decay),
        grid=(B,),
        in_specs=[spec(HGRN_WIDTH), spec(HGRN_WIDTH), spec(HGRN_VWIDTH), spec(HGRN_VWIDTH),
                  pl.BlockSpec((1, HGRN_DV), lambda b: (0, 0)),
                  pl.BlockSpec((C, C), lambda b: (0, 0))],
        out_specs=spec(HGRN_VWIDTH),
        out_shape=jax.ShapeDtypeStruct((B, S, HGRN_VWIDTH), BF16),
        scratch_shapes=[pltpu.VMEM((HGRN_HEADS, HGRN_DV, HGRN_DK), F32)],
        compiler_params=_cparams(1),
        name="hgrn2",
    )(hq, hlf, hv, hg, norm_g.reshape(1, HGRN_DV).astype(F32), tri)


MLP_COLS = 1024


def _merge_mlp_kernel(ya_ref, yb_ref, gma_ref, gmb_ref, x_ref, mod_ref, ln_ref,
                      wa_ref, wb_ref, wo_ref, w1_ref, w2_ref, o_ref):
    gt1, sh2, sc2, gt2 = [mod_ref[0, i:i + 1, :] for i in range(4)]
    g1, b1, g2, b2 = [ln_ref[i:i + 1, :] for i in range(4)]
    pa = _dot(ya_ref[0], wa_ref[...])
    pb = _dot(yb_ref[0], wb_ref[...])
    merged = gma_ref[0].astype(F32) * pa + gmb_ref[0].astype(F32) * pb
    y = _dot(merged.astype(BF16), wo_ref[...])
    x = _layer_norm(DEEPNORM_ALPHA * x_ref[0] + (1.0 + gt1) * y, g1, b1)
    u = (x * (1.0 + sc2) + sh2).astype(BF16)
    y = jnp.zeros(x.shape, F32)
    for c in range(MLP_HIDDEN // MLP_COLS):
        h = jnp.maximum(_dot(u, w1_ref[:, c * MLP_COLS:(c + 1) * MLP_COLS]), 0.0)
        y = y + _dot((h * h).astype(BF16), w2_ref[c * MLP_COLS:(c + 1) * MLP_COLS, :])
    o_ref[0] = _layer_norm(DEEPNORM_ALPHA * x + (1.0 + gt2) * y, g2, b2)


def merge_mlp(ya, yb, gma, gmb, x, mod4, ln4, wa, wb, wo, w1, w2):
    B, S, D = x.shape
    tm = min(TOKEN_TILE, S)
    tok = lambda n: pl.BlockSpec((1, tm, n), lambda b, i: (b, i, 0))
    return pl.pallas_call(
        _merge_mlp_kernel,
        grid=(B, S // tm),
        in_specs=[tok(NSA_WIDTH), tok(HGRN_VWIDTH), tok(D), tok(D), tok(D),
                  pl.BlockSpec((1, 4, D), lambda b, i: (b, 0, 0)), _resident(ln4.shape),
                  _resident(wa.shape), _resident(wb.shape), _resident(wo.shape),
                  _resident(w1.shape), _resident(w2.shape)],
        out_specs=tok(D),
        out_shape=jax.ShapeDtypeStruct((B, S, D), F32),
        compiler_params=_cparams(2, vmem=MERGE_MLP_VMEM),
        name="merge_mlp",
    )(ya, yb, gma, gmb, x, mod4, ln4, wa, wb, wo, w1, w2)


def _rope_tables(S):
    inv = 1.0 / (ROPE_THETA ** (jnp.arange(0, NSA_DH, 2, dtype=F32) / NSA_DH))
    ang = jnp.arange(S, dtype=F32)[:, None] * inv[None, :]
    cos, sin = jnp.cos(ang), jnp.sin(ang)
    reps = LANES // NSA_DH
    return (jnp.tile(jnp.concatenate([cos, cos], axis=1), (1, reps)),
            jnp.tile(jnp.concatenate([-sin, sin], axis=1), (1, reps)))


def kernel(x, c, w_in, b_in, cmp_pe_k, cmp_pe_v, cmp_wk1, cmp_wk2, cmp_wv1, cmp_wv2, hgrn_lb_logits, hgrn_norm_g, w_branch_a, w_branch_b, w_out, w_ada, b_ada, ln1_g, ln1_b, w_mlp1, w_mlp2, ln2_g, ln2_b):
    B, S, D = x.shape
    G = NSA_GROUPS
    lb_all = jnp.cumsum(jax.nn.softmax(hgrn_lb_logits.astype(F32), axis=0), axis=0)
    lb_all = lb_all - lb_all[0:1]
    cos_t, sin_t = _rope_tables(S)
    mod = adaln_mod(c, w_ada, b_ada)
    for l in range(DEPTH):
        sh1, sc1, gt1, sh2, sc2, gt2 = [mod[l, :, None, i * D:(i + 1) * D] for i in range(6)]
        wts = _prep_in_proj_weights(w_in[l], b_in[l])
        q, kx, kv, vt, gates_t, nrm, hq, hlf, hv, hg, gma, gmb = in_proj(x, sc1, sh1, cos_t, sin_t, lb_all[l].reshape(1, -1), wts)
        pe = jnp.stack([cmp_pe_k[l].reshape(1, -1), cmp_pe_v[l].reshape(1, -1)])
        w1 = jnp.stack([cmp_wk1[l], cmp_wv1[l]]).astype(BF16)
        w2 = jnp.pad(jnp.stack([cmp_wk2[l], cmp_wv2[l]]), ((0, 0), (0, 0), (0, LANES - NSA_DH))).astype(BF16)
        cmp = nsa_compress(kv, pe, w1, w2)
        bounded = jnp.max(nrm[:, :, 0, 0]) * jnp.max(nrm[:, :, 1, 0]) <= SCORE_BOUND ** 2
        ya = lax.cond(bounded,
                      functools.partial(nsa_attend, bounded_scores=True),
                      functools.partial(nsa_attend, bounded_scores=False),
                      q, kx, vt, cmp, gates_t)
        yb = lax.cond(jnp.min(nrm[:, :, 2, 0]) >= -HGRN_MAX_STEP_DECAY,
                      functools.partial(hgrn2, bounded_decay=True),
                      functools.partial(hgrn2, bounded_decay=False),
                      hq, hlf, hv, hg, hgrn_norm_g[l])
        mod4 = jnp.concatenate([gt1, sh2, sc2, gt2], axis=1)
        ln4 = jnp.stack([ln1_g[l], ln1_b[l], ln2_g[l], ln2_b[l]])
        x = merge_mlp(ya, yb, gma, gmb, x, mod4, ln4,
                      w_branch_a[l].astype(BF16), w_branch_b[l].astype(BF16), w_out[l].astype(BF16),
                      w_mlp1[l].astype(BF16), w_mlp2[l].astype(BF16))
    return x
```

```python
import functools

import numpy as np
import jax
import jax.numpy as jnp
from jax import lax
from jax.experimental import pallas as pl
from jax.experimental.pallas import tpu as pltpu

D_MODEL = 1024
DEPTH = 2
NSA_HEADS = 8
NSA_GROUPS = 2
NSA_REP = NSA_HEADS // NSA_GROUPS
NSA_DH = 64
NSA_WIDTH = NSA_HEADS * NSA_DH
NSA_KV_WIDTH = NSA_GROUPS * NSA_DH
CMP_LEN = 32
CMP_STRIDE = 16
CMP_HIDDEN = 2 * NSA_DH
SEL_LEN = 64
SEL_TOPK = 8
FORCE_SCORE = 1.0e4
WINDOW = 512
HGRN_HEADS = 4
HGRN_DK = 128
HGRN_DV = 128
HGRN_WIDTH = HGRN_HEADS * HGRN_DK
HGRN_VWIDTH = HGRN_HEADS * HGRN_DV
MLP_HIDDEN = 4 * D_MODEL
ROPE_THETA = 10000.0
LN_EPS = 1e-5
RMS_EPS = 1e-6
DEEPNORM_ALPHA = (2 * DEPTH) ** 0.25
IN_SIZES = (NSA_WIDTH,) + (NSA_KV_WIDTH,) * 6 + (3 * NSA_HEADS,) + (HGRN_WIDTH, HGRN_WIDTH, HGRN_VWIDTH, HGRN_VWIDTH) + (D_MODEL, D_MODEL)
IN_OFFSETS = [0] + [int(v) for v in np.cumsum(IN_SIZES)]

LANES = 128
SUBLANES = 8
VMEM_LIMIT = 48 * 1024 * 1024
MERGE_MLP_VMEM = 56 * 1024 * 1024
TOKEN_TILE = 512
Q_TILE = 128
K_CHUNK = 128
ATT_BLOCK = 4
ATT_LOOKAHEAD = 2
HGRN_CHUNK = 64
HGRN_UNROLL = 4
HGRN_SUB = 8
HGRN_MAX_STEP_DECAY = 7.5
NEG_BIG = -1e30
LOG2E = 1.4426950408889634
Q_SCALE = NSA_DH ** -0.5 * LOG2E
SCORE_BOUND = 96.0
VT_ROWS = NSA_DH + 16

F32 = jnp.float32
BF16 = jnp.bfloat16


def _cparams(n_grid, vmem=VMEM_LIMIT):
    return pltpu.CompilerParams(dimension_semantics=("arbitrary",) * n_grid, vmem_limit_bytes=vmem)


def _resident(shape):
    nd = len(shape)
    return pl.BlockSpec(shape, lambda *_: (0,) * nd, pipeline_mode=pl.Buffered(1))


def _dot(a, b):
    return jnp.dot(a, b, preferred_element_type=F32)


def _dot_nt(a, b):
    return lax.dot_general(a, b, (((1,), (1,)), ((), ())), preferred_element_type=F32)


def _dot_tn(a, b):
    return lax.dot_general(a, b, (((0,), (0,)), ((), ())), preferred_element_type=F32)


def _sigmoid(x):
    return 1.0 / (1.0 + jnp.exp(-x))


def _silu(x):
    return x * _sigmoid(x)


def _layer_norm(z, g, b):
    mu = jnp.mean(z, axis=-1, keepdims=True)
    zc = z - mu
    var = jnp.mean(zc * zc, axis=-1, keepdims=True)
    return zc * lax.rsqrt(var + LN_EPS) * g + b


def _adaln_kernel(c_ref, w_ref, b_ref, o_ref):
    cond = _silu(c_ref[...]).astype(BF16)
    o_ref[0] = _dot(cond, w_ref[0]) + b_ref[0]


def adaln_mod(c, w_ada, b_ada):
    L, D, N = w_ada.shape
    B = c.shape[0]
    tn = D
    return pl.pallas_call(
        _adaln_kernel,
        grid=(L, N // tn),
        in_specs=[
            pl.BlockSpec((B, D), lambda l, j: (0, 0)),
            pl.BlockSpec((1, D, tn), lambda l, j: (l, 0, j)),
            pl.BlockSpec((1, 1, tn), lambda l, j: (l, 0, j)),
        ],
        out_specs=pl.BlockSpec((1, B, tn), lambda l, j: (l, 0, j)),
        out_shape=jax.ShapeDtypeStruct((L, B, N), F32),
        compiler_params=_cparams(2),
        name="adaln_mod",
    )(c, w_ada.astype(BF16), b_ada.reshape(L, 1, N))


N_ROPE = NSA_WIDTH + 3 * NSA_KV_WIDTH
N_NSA = N_ROPE + 3 * NSA_KV_WIDTH
N_GATE = NSA_GROUPS * LANES
N_HGRN = 2 * HGRN_WIDTH + 2 * HGRN_VWIDTH
N_MERGE = 2 * D_MODEL


def _in_proj_kernel(x_ref, sc_ref, sh_ref, cos_ref, sin_ref, lb_ref,
                    wn_ref, bn_ref, wg_ref, bg_ref, wh_ref, bh_ref, wm_ref, bm_ref,
                    q_ref, kx_ref, kv_ref, vt_ref, ga_ref, nrm_ref, hq_ref, hlf_ref, hv_ref, hg_ref, gma_ref, gmb_ref):
    u = (x_ref[0] * (1.0 + sc_ref[0]) + sh_ref[0]).astype(BF16)
    cos = cos_ref[...]
    sin = sin_ref[...]
    lane = lax.broadcasted_iota(jnp.int32, cos.shape, 1)
    first_half = (lane % NSA_DH) < (NSA_DH // 2)
    low = lane < NSA_DH

    def heads(t, upper):
        return jnp.where(low, t, upper), jnp.where(low, pltpu.roll(t, NSA_DH, 1), upper)

    n_q, n_rope = NSA_HEADS // 2, N_ROPE // LANES
    tm = u.shape[0]
    ones_rows = jnp.where(lax.broadcasted_iota(jnp.int32, (VT_ROWS - NSA_DH, tm), 0) == 0, 1.0, 0.0).astype(BF16)

    def max_sq_norm(t, acc):
        n = jnp.max(jnp.sum(t * t, axis=1, keepdims=True), axis=0, keepdims=True)
        return n if acc is None else jnp.maximum(acc, n)

    stat = {"q_sq": None, "k_sq": None}

    def attn_pair(i2):
        t2 = _dot(u, wn_ref[:, i2 * LANES:(i2 + 2) * LANES]) + bn_ref[:, i2 * LANES:(i2 + 2) * LANES]
        for i in (i2, i2 + 1):
            t = t2[:, (i - i2) * LANES:(i - i2 + 1) * LANES]
            if i < n_rope:
                rot = jnp.where(first_half, pltpu.roll(t, LANES - NSA_DH // 2, 1), pltpu.roll(t, NSA_DH // 2, 1))
                t = t * cos + rot * sin
            if i < n_q:
                t = t * Q_SCALE
                stat["q_sq"] = max_sq_norm(t, stat["q_sq"])
                for j, piece in enumerate(heads(t, 0.0)):
                    q_ref[0, 2 * i + j] = piece.astype(BF16)
            elif i == n_q or i == n_rope:
                for j in range(2):
                    kv_ref[0, (2 if i == n_rope else 0) + j] = t[:, j * NSA_DH:(j + 1) * NSA_DH].astype(BF16)
            elif i < n_rope:
                kind = i - n_q - 1
                stat["k_sq"] = max_sq_norm(t, stat["k_sq"])
                for j, piece in enumerate(heads(t, 0.0)):
                    kx_ref[0, 2 * kind + j] = piece.astype(BF16)
            else:
                kind = i - n_rope - 1
                tt = t.T.astype(BF16)
                for j in range(2):
                    vt_ref[0, 2 * kind + j] = jnp.concatenate([tt[j * NSA_DH:(j + 1) * NSA_DH], ones_rows], axis=0)

    def branch_gates():
        gates = _sigmoid(_dot(u, wg_ref[...]) + bg_ref[...])
        for g in range(NSA_GROUPS):
            ga_ref[0, g] = gates[:, g * LANES:(g + 1) * LANES].T[0:2 * SUBLANES]

    W = HGRN_WIDTH
    hgrn_cols = lambda i: _dot(u, wh_ref[:, i * W:(i + 1) * W]) + bh_ref[:, i * W:(i + 1) * W]

    def hgrn_q():
        hq_ref[0] = (_silu(hgrn_cols(0)) * (HGRN_DK ** -0.5)).astype(BF16)

    def hgrn_f():
        z = hgrn_cols(1)
        lb = lb_ref[...]
        log_sig = jnp.minimum(z, 0.0) - jnp.log(1.0 + jnp.exp(-jnp.abs(z)))
        a = jnp.log(lb)
        bb = jnp.log1p(-lb) + log_sig
        log_f = jnp.maximum(a, bb) + jnp.log(1.0 + jnp.exp(-jnp.abs(a - bb)))
        hlf_ref[0] = log_f
        stat["lf_min"] = jnp.min(jnp.min(log_f, axis=1, keepdims=True), axis=0, keepdims=True)

    def hgrn_v():
        hv_ref[0] = hgrn_cols(2).astype(BF16)

    def hgrn_g():
        hg_ref[0] = _silu(hgrn_cols(3)).astype(BF16)

    def merge_gate(k):
        ref, half = (gma_ref, gmb_ref)[k // 2], D_MODEL // 2
        c0 = k * half
        ref[0, :, (k % 2) * half:(k % 2 + 1) * half] = _sigmoid(
            _dot(u, wm_ref[:, c0:c0 + half]) + bm_ref[:, c0:c0 + half]).astype(BF16)

    P = functools.partial
    for group in (P(attn_pair, 0), P(merge_gate, 0), P(attn_pair, 2), P(merge_gate, 1), P(attn_pair, 4), hgrn_q,
                  hgrn_f, P(merge_gate, 2), P(attn_pair, 6), hgrn_g, P(attn_pair, 8), P(merge_gate, 3),
                  branch_gates, hgrn_v):
        group()
    srow = lax.broadcasted_iota(jnp.int32, (SUBLANES, LANES), 0)
    nrm_ref[0, 0] = jnp.where(srow == 0, stat["q_sq"], jnp.where(srow == 1, stat["k_sq"],
                                                                 jnp.where(srow == 2, stat["lf_min"], 0.0)))


def in_proj(x, sc, sh, cos_t, sin_t, lb, wts):
    B, S, D = x.shape
    tm = min(TOKEN_TILE, S)
    assert S // SEL_LEN <= LANES - NSA_DH
    wn, bn, wg, bg, wh, bh, wm, bm = wts
    tok = lambda n: pl.BlockSpec((1, tm, n), lambda b, i: (b, i, 0))
    per_b = pl.BlockSpec((1, 1, D), lambda b, i: (b, 0, 0))
    tab = pl.BlockSpec((tm, LANES), lambda b, i: (i, 0))
    out_shape = (
        jax.ShapeDtypeStruct((B, NSA_HEADS, S, LANES), BF16),
        jax.ShapeDtypeStruct((B, 2 * NSA_GROUPS, S, LANES), BF16),
        jax.ShapeDtypeStruct((B, 2 * NSA_GROUPS, S, NSA_DH), BF16),
        jax.ShapeDtypeStruct((B, 2 * NSA_GROUPS, VT_ROWS, S), BF16),
        jax.ShapeDtypeStruct((B, NSA_GROUPS, 2 * SUBLANES, S), F32),
        jax.ShapeDtypeStruct((B, S // tm, SUBLANES, LANES), F32),
        jax.ShapeDtypeStruct((B, S, HGRN_WIDTH), BF16),
        jax.ShapeDtypeStruct((B, S, HGRN_WIDTH), F32),
        jax.ShapeDtypeStruct((B, S, HGRN_VWIDTH), BF16),
        jax.ShapeDtypeStruct((B, S, HGRN_VWIDTH), BF16),
        jax.ShapeDtypeStruct((B, S, D), BF16),
        jax.ShapeDtypeStruct((B, S, D), BF16),
    )
    out_specs = (
        pl.BlockSpec((1, NSA_HEADS, tm, LANES), lambda b, i: (b, 0, i, 0)),
        pl.BlockSpec((1, 2 * NSA_GROUPS, tm, LANES), lambda b, i: (b, 0, i, 0)),
        pl.BlockSpec((1, 2 * NSA_GROUPS, tm, NSA_DH), lambda b, i: (b, 0, i, 0)),
        pl.BlockSpec((1, 2 * NSA_GROUPS, VT_ROWS, tm), lambda b, i: (b, 0, 0, i)),
        pl.BlockSpec((1, NSA_GROUPS, 2 * SUBLANES, tm), lambda b, i: (b, 0, 0, i)),
        pl.BlockSpec((1, 1, SUBLANES, LANES), lambda b, i: (b, i, 0, 0)),
        tok(HGRN_WIDTH), tok(HGRN_WIDTH), tok(HGRN_VWIDTH), tok(HGRN_VWIDTH), tok(D), tok(D),
    )
    return pl.pallas_call(
        _in_proj_kernel,
        grid=(B, S // tm),
        in_specs=[tok(D), per_b, per_b, tab, tab, _resident(lb.shape),
                  _resident(wn.shape), _resident(bn.shape), _resident(wg.shape), _resident(bg.shape),
                  _resident(wh.shape), _resident(bh.shape), _resident(wm.shape), _resident(bm.shape)],
        out_specs=out_specs,
        out_shape=out_shape,
        compiler_params=_cparams(2),
        name="in_proj",
    )(x, sc, sh, cos_t, sin_t, lb, wn, bn, wg, bg, wh, bh, wm, bm)


def _prep_in_proj_weights(w_in_l, b_in_l):
    o = IN_OFFSETS
    col = lambda i: (w_in_l[:, o[i]:o[i + 1]], b_in_l[o[i]:o[i + 1]])
    q_a, k_c, v_c, k_s, v_s, k_w, v_w, g_a, q_b, f_b, i_b, g_b, gm_a, gm_b = [col(i) for i in range(14)]

    def cat(parts):
        return (jnp.concatenate([p[0] for p in parts], axis=1).astype(BF16),
                jnp.concatenate([p[1] for p in parts], axis=0).reshape(1, -1).astype(F32))

    wn, bn = cat([q_a, k_c, k_s, k_w, v_c, v_s, v_w])
    per_group = 3 * NSA_REP
    gw = jnp.zeros((w_in_l.shape[0], N_GATE), w_in_l.dtype)
    gb = jnp.zeros((N_GATE,), b_in_l.dtype)
    for g in range(NSA_GROUPS):
        gw = gw.at[:, g * LANES:g * LANES + per_group].set(g_a[0][:, g * per_group:(g + 1) * per_group])
        gb = gb.at[g * LANES:g * LANES + per_group].set(g_a[1][g * per_group:(g + 1) * per_group])
    wg, bg = gw.astype(BF16), gb.reshape(1, -1).astype(F32)
    wh, bh = cat([q_b, f_b, i_b, g_b])
    wm, bm = cat([gm_a, gm_b])
    return wn, bn, wg, bg, wh, bh, wm, bm


def _compress_kernel(t_ref, pe_ref, w1_ref, w2_ref, o_ref):
    half = CMP_STRIDE * NSA_DH
    for s in range(2):
        pe = jnp.broadcast_to(pe_ref[s], (8, 2 * half)).astype(BF16)
        c = _dot(pe, w1_ref[s])[0:1]
        for g in range(NSA_GROUPS):
            t = t_ref[0, s * NSA_GROUPS + g]
            nrow = t.shape[0]
            a = _dot(t, w1_ref[s, 0:half, :])
            b = _dot(t, w1_ref[s, half:2 * half, :])
            h = a + pltpu.roll(b, nrow - 1, 0) + c
            o_ref[0, s * NSA_GROUPS + g] = _dot(_silu(h).astype(BF16), w2_ref[s]).astype(BF16)


def nsa_compress(kv, pe, w1, w2):
    B, _, S, dh = kv.shape
    nrow = S // CMP_STRIDE
    G = NSA_GROUPS
    kv_rows = kv.reshape(B, 2 * G, nrow, CMP_STRIDE * dh)
    return pl.pallas_call(
        _compress_kernel,
        grid=(B,),
        in_specs=[
            pl.BlockSpec((1, 2 * G, nrow, CMP_STRIDE * dh), lambda b: (b, 0, 0, 0)),
            _resident(pe.shape), _resident(w1.shape), _resident(w2.shape),
        ],
        out_specs=pl.BlockSpec((1, 2 * G, nrow, LANES), lambda b: (b, 0, 0, 0)),
        out_shape=jax.ShapeDtypeStruct((B, 2 * G, nrow, LANES), BF16),
        compiler_params=_cparams(1),
        name="nsa_compress",
    )(kv_rows, pe, w1, w2)


def _nsa_kernel(q_ref, cmp_ref, kx_ref, vt_ref, gt_ref, ovt_ref, o_ref, *, n_sel, n_tiles, bounded_scores):
    def tile(qs):
        gens = [_nsa_tile(qs, bi, g, q_ref, cmp_ref, kx_ref, vt_ref, gt_ref, ovt_ref, o_ref, n_sel, bounded_scores)
                for bi in range(q_ref.shape[0]) for g in range(NSA_GROUPS)]
        while gens:
            gens = [gen for gen in gens if next(gen, "done") != "done"]

    for qs in range(n_tiles):
        pl.when(pl.program_id(1) == qs)(functools.partial(tile, qs))


def _round_up(x, m):
    return -(-x // m) * m


def _nsa_tile(qb, bi, g, q_ref, cmp_ref, kx_ref, vt_ref, gt_ref, ovt_ref, o_ref, n_sel, bounded_scores):
    G, R, TQ, dh = NSA_GROUPS, NSA_REP, Q_TILE, NSA_DH
    cols = R * TQ
    s0 = qb * TQ
    kc_ref, vc_ref = cmp_ref.at[bi, g], cmp_ref.at[bi, G + g]
    ks_ref, kw_ref = kx_ref.at[bi, g], kx_ref.at[bi, G + g]
    vst_ref, vwt_ref = vt_ref.at[bi, g], vt_ref.at[bi, G + g]
    q = q_ref[bi, g * R:(g + 1) * R].reshape(cols, LANES)
    t_lane = s0 + (lax.broadcasted_iota(jnp.int32, (1, cols), 1) % TQ)
    kidx = lax.broadcasted_iota(jnp.int32, (K_CHUNK, cols), 0)
    tq = lax.broadcasted_iota(jnp.int32, (K_CHUNK, cols), 1) % TQ
    causal = kidx <= tq

    def branch_blocks(k_ref, vt_ref, c0, n, band_first):
        return [(k_ref, vt_ref, c0 + b0, min(ATT_BLOCK, n - b0), band_first and b0 == 0, b0 + ATT_BLOCK >= n)
                for b0 in range(0, n, ATT_BLOCK)]

    def score_block(blk):
        k_ref, _, c, n, band, diag = blk
        s = _dot_nt(k_ref[c * K_CHUNK:(c + n) * K_CHUNK, :], q)
        parts = [s[i * K_CHUNK:(i + 1) * K_CHUNK] for i in range(n)]
        if band:
            parts[0] = jnp.where(tq < kidx, parts[0], NEG_BIG)
        if diag:
            parts[-1] = jnp.where(causal, parts[-1], NEG_BIG)
        return jnp.concatenate(parts, axis=0) if n > 1 else parts[0]

    def finish_block(s, blk, block_bias=None):
        _, vt_ref, c, n, _, _ = blk
        nk = n * K_CHUNK
        m = None
        if block_bias is not None:
            j0 = c * K_CHUNK // SEL_LEN
            subs = [s[i * SEL_LEN:(i + 1) * SEL_LEN] for i in range(nk // SEL_LEN)]
            bias = [block_bias[j0 + i:j0 + i + 1, :] for i in range(nk // SEL_LEN)]
        if bounded_scores:
            p = jnp.exp2(s) if block_bias is None else jnp.concatenate(
                [jnp.exp2(s_i + b_i) for s_i, b_i in zip(subs, bias)], axis=0)
        elif block_bias is None:
            m = jnp.max(s, axis=0, keepdims=True)
            p = jnp.exp2(s - m)
        else:
            for s_i, b_i in zip(subs, bias):
                m_i = jnp.max(s_i, axis=0, keepdims=True) + b_i
                m = m_i if m is None else jnp.maximum(m, m_i)
            shift = jnp.where(m < 0.5 * NEG_BIG, 0.0, m)
            p = jnp.concatenate([jnp.exp2(s_i + (b_i - shift)) for s_i, b_i in zip(subs, bias)], axis=0)
        vt = vt_ref[:, c * K_CHUNK:c * K_CHUNK + nk]
        return m, _dot(vt, p.astype(BF16))

    def combine(stats):
        total = stats[0][1]
        if bounded_scores:
            for _, acc_i in stats[1:]:
                total = total + acc_i
        elif len(stats) > 1:
            m = stats[0][0]
            for st in stats[1:]:
                m = jnp.maximum(m, st[0])
            total = None
            for m_i, acc_i in stats:
                w = jnp.exp2(m_i - m)
                total = w * acc_i if total is None else total + w * acc_i
        return total[0:dh] * (1.0 / total[dh:dh + 1])

    n_win = WINDOW // K_CHUNK
    win_blocks = branch_blocks(kw_ref, vwt_ref, max(qb - n_win, 0), min(qb, n_win) + 1, qb >= n_win)
    blocks = win_blocks + branch_blocks(ks_ref, vst_ref, 0, qb + 1, False)

    ncb = min(kc_ref.shape[0], _round_up((s0 + TQ - CMP_LEN) // CMP_STRIDE + 1, 2 * SUBLANES))
    sc = _dot_nt(kc_ref[0:ncb, :], q)
    pending = {i: score_block(blocks[i]) for i in range(min(ATT_LOOKAHEAD, len(blocks)))}
    yield
    n_sub = lax.broadcasted_iota(jnp.int32, (ncb, cols), 0)
    mask_c = n_sub * CMP_STRIDE + (CMP_LEN - 1) <= t_lane
    sc = jnp.where(mask_c, sc, NEG_BIG)
    mc = jnp.max(sc, axis=0, keepdims=True)
    ec = jnp.where(mask_c, jnp.exp2(sc - mc), 0.0)
    pc = ec * (1.0 / jnp.maximum(jnp.sum(ec, axis=0, keepdims=True), 1e-30))
    o_c = _dot_tn(vc_ref[0:ncb, :], pc.astype(BF16))[0:dh]

    nb_live = (s0 + TQ) // SEL_LEN
    nb = min(ovt_ref.shape[0], _round_up(nb_live, 2 * SUBLANES))
    psum = pc[:, 0:TQ]
    for r in range(1, R):
        psum = psum + pc[:, r * TQ:(r + 1) * TQ]
    p_hi = psum.astype(BF16)
    p_lo = (psum - p_hi.astype(F32)).astype(BF16)
    ovt = ovt_ref[0:nb, 0:ncb]
    imp = _dot(ovt, p_hi) + _dot(ovt, p_lo)
    yield
    jb =lax.broadcasted_iota(jnp.int32, (nb, TQ), 0)
    tb = (s0 + lax.broadcasted_iota(jnp.int32, (nb, TQ), 1)) // SEL_LEN
    valid = jb <= tb
    forced = jnp.where(valid, jnp.where(jb == 0, 1.0, jnp.where(jb >= tb - 1, 1.0, 0.0)), 0.0)
    score = jnp.where(forced > 0.5, FORCE_SCORE, jnp.where(valid, imp, -1.0))
    rank = jnp.zeros((nb, TQ), F32)
    for i in range(nb_live):
        si = score[i:i + 1, :]
        tie_first = jnp.where(jb > i, 1.0, 0.0)
        rank = rank + jnp.where(si > score, 1.0, jnp.where(si == score, tie_first, 0.0))
    sel_bias = jnp.where(rank < n_sel, 0.0, NEG_BIG)
    sel_bias = jnp.concatenate([sel_bias] * R, axis=1)

    stats = []
    for i, blk in enumerate(blocks):
        is_sel = i >= len(win_blocks)
        stats.append(finish_block(pending.pop(i), blk, sel_bias if is_sel else None))
        if i + ATT_LOOKAHEAD < len(blocks):
            pending[i + ATT_LOOKAHEAD] = score_block(blocks[i + ATT_LOOKAHEAD])
        yield
    o_w = combine(stats[:len(win_blocks)])
    o_s = combine(stats[len(win_blocks):])

    gate = gt_ref[bi, g]
    pieces = []
    for r in range(R):
        sl = slice(r * TQ, (r + 1) * TQ)
        o_r = (gate[3 * r:3 * r + 1, :] * o_c[:, sl] + gate[3 * r + 1:3 * r + 2, :] * o_s[:, sl]
               + gate[3 * r + 2:3 * r + 3, :] * o_w[:, sl])
        pieces.append(o_r.T)
    o_ref[bi, :, g * R * dh:(g + 1) * R * dh] = jnp.concatenate(pieces, axis=1).astype(BF16)


def nsa_attend(q, kx, vt, cmp, gates_t, bounded_scores):
    B, H, S, _ = q.shape
    G, R, dh = NSA_GROUPS, NSA_REP, NSA_DH
    ncb = S // CMP_STRIDE
    nb = S // SEL_LEN
    assert (S % Q_TILE == 0 and Q_TILE == K_CHUNK and WINDOW % K_CHUNK == 0 and K_CHUNK % SEL_LEN == 0
            and 3 * R <= 2 * SUBLANES)
    cstart = np.arange(ncb) * CMP_STRIDE
    sstart = np.arange(nb) * SEL_LEN
    overlap = ((cstart[:, None] < sstart[None, :] + SEL_LEN) & (cstart[:, None] + CMP_LEN > sstart[None, :]))
    ovt = jnp.asarray(overlap.T, BF16)
    nbt = 1
    per_batch = lambda a: pl.BlockSpec((nbt,) + a.shape[1:], lambda b, i: (b, 0, 0, 0))
    return pl.pallas_call(
        functools.partial(_nsa_kernel, n_sel=min(SEL_TOPK, nb), n_tiles=S // Q_TILE, bounded_scores=bounded_scores),
        grid=(B // nbt, S // Q_TILE),
        in_specs=[
            pl.BlockSpec((nbt, H, Q_TILE, LANES), lambda b, i: (b, 0, i, 0)),
            per_batch(cmp), per_batch(kx), per_batch(vt),
            pl.BlockSpec((nbt, G, 2 * SUBLANES, Q_TILE), lambda b, i: (b, 0, 0, i)),
            pl.BlockSpec(ovt.shape, lambda b, i: (0, 0)),
        ],
        out_specs=pl.BlockSpec((nbt, Q_TILE, H * dh), lambda b, i: (b, i, 0)),
        out_shape=jax.ShapeDtypeStruct((B, S, H * dh), BF16),
        compiler_params=_cparams(2),
        name="nsa_attend",
    )(q, cmp, kx, vt, gates_t, ovt)


def _hgrn_kernel(q_ref, lf_ref, v_ref, g_ref, ng_ref, tri_ref, o_ref, st_ref, *, bounded_decay):
    C, SB = HGRN_CHUNK, HGRN_SUB
    NBK = C // SB
    S = q_ref.shape[1]
    st_ref[...] = jnp.zeros(st_ref.shape, F32)
    row = lax.broadcasted_iota(jnp.int32, (C, HGRN_DK), 0)
    sub_row = lax.broadcasted_iota(jnp.int32, (SB, 1), 0)
    cr = lax.broadcasted_iota(jnp.int32, (C, C), 0)
    cc = lax.broadcasted_iota(jnp.int32, (C, C), 1)
    diag_mask = (cr // SB == cc // SB) & (cc <= cr)

    def head_chunk(h, r0):
        hs = slice(h * HGRN_DK, (h + 1) * HGRN_DK)
        q = q_ref[0, pl.ds(r0, C), hs].astype(F32)
        lf = lf_ref[0, pl.ds(r0, C), hs]
        v_bf = v_ref[0, pl.ds(r0, C), hs]
        v = v_bf.astype(F32)
        kh = 1.0 - jnp.exp(lf)
        tri = tri_ref[...]
        lf0 = lf.astype(BF16)
        lf1 = (lf - lf0.astype(F32)).astype(BF16)
        lf2 = (lf - lf0.astype(F32) - lf1.astype(F32)).astype(BF16)
        b = _dot(tri, lf0) + _dot(tri, lf1) + _dot(tri, lf2)
        yield
        b_last = b[C - 1:C, :]
        st = st_ref[h]
        o = _dot_nt((q * jnp.exp(b)).astype(BF16), st.astype(BF16))
        b_end = jnp.concatenate(
            [jnp.broadcast_to(b[(j + 1) * SB - 1:(j + 1) * SB, :], (SB, HGRN_DK)) for j in range(NBK)], axis=0)
        k_end = kh * jnp.exp(b_end - b)
        q_parts, k_parts = [], []
        for j in range(NBK - 1):
            lo = (j + 1) * SB
            qj = q[lo:] * jnp.exp(b[lo:] - b[lo - 1:lo, :])
            q_parts.append(jnp.concatenate([jnp.zeros((lo, HGRN_DK), F32), qj], axis=0))
            k_parts.append(jnp.where((row >= j * SB) & (row < lo), k_end, 0.0))
        q_cat = jnp.concatenate(q_parts, axis=1).astype(BF16)
        k_cat = jnp.concatenate(k_parts, axis=1).astype(BF16)
        a_off = _dot_nt(q_cat, k_cat)
        k_last = (kh * jnp.exp(b_last - b)).astype(BF16)
        st_ref[h] = st * jnp.exp(b_last) + _dot_tn(v_bf, k_last)
        if bounded_decay:
            b_start = jnp.concatenate([jnp.zeros((SB, HGRN_DK), F32), b_end[:C - SB]], axis=0)
            a_dg = _dot_nt((q * jnp.exp(b - b_start)).astype(BF16), (kh * jnp.exp(b_start - b)).astype(BF16))
            yield
            o = o + _dot(jnp.where(diag_mask, a_dg, a_off).astype(BF16), v_bf)
            yield
        else:
            yield
            o = o + _dot(a_off.astype(BF16), v_bf)
            diag = []
            for j in range(NBK):
                sl = slice(j * SB, (j + 1) * SB)
                qj, bj, kj, vj = q[sl], b[sl], kh[sl], v[sl]
                oj = jnp.zeros((SB, HGRN_DV), F32)
                for s in range(SB):
                    w = jnp.exp(jnp.minimum(bj - bj[s:s + 1, :], 0.0))
                    a = jnp.sum(qj * kj[s:s + 1, :] * w, axis=-1, keepdims=True)
                    a = jnp.where(sub_row >= s, a, 0.0)
                    oj = oj + a * vj[s:s + 1, :]
                diag.append(oj)
            o = o + jnp.concatenate(diag, axis=0)
            yield
        o = o * lax.rsqrt(jnp.mean(o * o, axis=-1, keepdims=True) + RMS_EPS) * ng_ref[...]
        o_ref[0, pl.ds(r0, C), hs] = (o * g_ref[0, pl.ds(r0, C), hs].astype(F32)).astype(BF16)

    def chunk(ci, carry):
        r0 = pl.multiple_of(ci * (C * HGRN_UNROLL), C * HGRN_UNROLL)
        heads = [head_chunk(h, r0 + u * C) for u in range(HGRN_UNROLL) for h in range(HGRN_HEADS)]
        for _ in range(3):
            for gen in heads:
                next(gen)
        for gen in heads:
            next(gen, None)
        return carry

    lax.fori_loop(0, S // (C * HGRN_UNROLL), chunk, 0)


def hgrn2(hq, hlf, hv, hg, norm_g, bounded_decay):
    B, S, _ = hq.shape
    C = HGRN_CHUNK
    assert S % (C * HGRN_UNROLL) == 0
    tri = jnp.asarray(np.tril(np.ones((C, C))), BF16)
    spec = lambda n: pl.BlockSpec((1, S, n), lambda b: (b, 0, 0))
    return pl.pallas_call(
        functools.partial(_hgrn_kernel, bounded_decay=bounded_decay),
        grid=(B,),
        in_specs=[spec(HGRN_WIDTH), spec(HGRN_WIDTH), spec(HGRN_VWIDTH), spec(HGRN_VWIDTH),
                  pl.BlockSpec((1, HGRN_DV), lambda b: (0, 0)),
                  pl.BlockSpec((C, C), lambda b: (0, 0))],
        out_specs=spec(HGRN_VWIDTH),
        out_shape=jax.ShapeDtypeStruct((B, S, HGRN_VWIDTH), BF16),
        scratch_shapes=[pltpu.VMEM((HGRN_HEADS, HGRN_DV, HGRN_DK), F32)],
        compiler_params=_cparams(1),
        name="hgrn2",
    )(hq, hlf, hv, hg, norm_g.reshape(1, HGRN_DV).astype(F32), tri)


MLP_COLS = 1024


def _merge_mlp_kernel(ya_ref, yb_ref, gma_ref, gmb_ref, x_ref, mod_ref, ln_ref,
                      wa_ref, wb_ref, wo_ref, w1_ref, w2_ref, o_ref):
    gt1, sh2, sc2, gt2 = [mod_ref[0, i:i + 1, :] for i in range(4)]
    g1, b1, g2, b2 = [ln_ref[i:i + 1, :] for i in range(4)]
    pa = _dot(ya_ref[0], wa_ref[...])
    pb = _dot(yb_ref[0], wb_ref[...])
    merged = gma_ref[0].astype(F32) * pa + gmb_ref[0].astype(F32) * pb
    y = _dot(merged.astype(BF16), wo_ref[...])
    x = _layer_norm(DEEPNORM_ALPHA * x_ref[0] + (1.0 + gt1) * y, g1, b1)
    u = (x * (1.0 + sc2) + sh2).astype(BF16)
    y = jnp.zeros(x.shape, F32)
    for c in range(MLP_HIDDEN // MLP_COLS):
        h = jnp.maximum(_dot(u, w1_ref[:, c * MLP_COLS:(c + 1) * MLP_COLS]), 0.0)
        y = y + _dot((h * h).astype(BF16), w2_ref[c * MLP_COLS:(c + 1) * MLP_COLS, :])
    o_ref[0] = _layer_norm(DEEPNORM_ALPHA * x + (1.0 + gt2) * y, g2, b2)


def merge_mlp(ya, yb, gma, gmb, x, mod4, ln4, wa, wb, wo, w1, w2):
    B, S, D = x.shape
    tm = min(TOKEN_TILE, S)
    tok = lambda n: pl.BlockSpec((1, tm, n), lambda b, i: (b, i, 0))
    return pl.pallas_call(
        _merge_mlp_kernel,
        grid=(B, S // tm),
        in_specs=[tok(NSA_WIDTH), tok(HGRN_VWIDTH), tok(D), tok(D), tok(D),
                  pl.BlockSpec((1, 4, D), lambda b, i: (b, 0, 0)), _resident(ln4.shape),
                  _resident(wa.shape), _resident(wb.shape), _resident(wo.shape),
                  _resident(w1.shape), _resident(w2.shape)],
        out_specs=tok(D),
        out_shape=jax.ShapeDtypeStruct((B, S, D), F32),
        compiler_params=_cparams(2, vmem=MERGE_MLP_VMEM),
        name="merge_mlp",
    )(ya, yb, gma, gmb, x, mod4, ln4, wa, wb, wo, w1, w2)


def _rope_tables(S):
    inv = 1.0 / (ROPE_THETA ** (jnp.arange(0, NSA_DH, 2, dtype=F32) / NSA_DH))
    ang = jnp.arange(S, dtype=F32)[:, None] * inv[None, :]
    cos, sin = jnp.cos(ang), jnp.sin(ang)
    reps = LANES // NSA_DH
    return (jnp.tile(jnp.concatenate([cos, cos], axis=1), (1, reps)),
            jnp.tile(jnp.concatenate([-sin, sin], axis=1), (1, reps)))


def kernel(x, c, w_in, b_in, cmp_pe_k, cmp_pe_v, cmp_wk1, cmp_wk2, cmp_wv1, cmp_wv2, hgrn_lb_logits, hgrn_norm_g, w_branch_a, w_branch_b, w_out, w_ada, b_ada, ln1_g, ln1_b, w_mlp1, w_mlp2, ln2_g, ln2_b):
    B, S, D = x.shape
    G = NSA_GROUPS
    lb_all = jnp.cumsum(jax.nn.softmax(hgrn_lb_logits.astype(F32), axis=0), axis=0)
    lb_all = lb_all - lb_all[0:1]
    cos_t, sin_t = _rope_tables(S)
    mod = adaln_mod(c, w_ada, b_ada)
    for l in range(DEPTH):
        sh1, sc1, gt1, sh2, sc2, gt2 = [mod[l, :, None, i * D:(i + 1) * D] for i in range(6)]
        wts = _prep_in_proj_weights(w_in[l], b_in[l])
        q, kx, kv, vt, gates_t, nrm, hq, hlf, hv, hg, gma, gmb = in_proj(x, sc1, sh1, cos_t, sin_t, lb_all[l].reshape(1, -1), wts)
        pe = jnp.stack([cmp_pe_k[l].reshape(1, -1), cmp_pe_v[l].reshape(1, -1)])
        w1 = jnp.stack([cmp_wk1[l], cmp_wv1[l]]).astype(BF16)
        w2 = jnp.pad(jnp.stack([cmp_wk2[l], cmp_wv2[l]]), ((0, 0), (0, 0), (0, LANES - NSA_DH))).astype(BF16)
        cmp = nsa_compress(kv, pe, w1, w2)
        bounded = jnp.max(nrm[:, :, 0, 0]) * jnp.max(nrm[:, :, 1, 0]) <= SCORE_BOUND ** 2
        ya = lax.cond(bounded,
                      functools.partial(nsa_attend, bounded_scores=True),
                      functools.partial(nsa_attend, bounded_scores=False),
                      q, kx, vt, cmp, gates_t)
        yb = lax.cond(jnp.min(nrm[:, :, 2, 0]) >= -HGRN_MAX_STEP_DECAY,
                      functools.partial(hgrn2, bounded_decay=True),
                      functools.partial(hgrn2, bounded_decay=False),
                      hq, hlf, hv, hg, hgrn_norm_g[l])
        mod4 = jnp.concatenate([gt1, sh2, sc2, gt2], axis=1)
        ln4 = jnp.stack([ln1_g[l], ln1_b[l], ln2_g[l], ln2_b[l]])
        x = merge_mlp(ya, yb, gma, gmb, x, mod4, ln4,
                      w_branch_a[l].astype(BF16), w_branch_b[l].astype(BF16), w_out[l].astype(BF16),
                      w_mlp1[l].astype(BF16), w_mlp2[l].astype(BF16))
    return x
```

```python
import functools

import numpy as np
import jax
import jax.numpy as jnp
from jax import lax
from jax.experimental import pallas as pl
from jax.experimental.pallas import tpu as pltpu

D_MODEL = 1024
DEPTH = 2
NSA_HEADS = 8
NSA_GROUPS = 2
NSA_REP = NSA_HEADS // NSA_GROUPS
NSA_DH = 64
NSA_WIDTH = NSA_HEADS * NSA_DH
NSA_KV_WIDTH = NSA_GROUPS * NSA_DH
CMP_LEN = 32
CMP_STRIDE = 16
CMP_HIDDEN = 2 * NSA_DH
SEL_LEN = 64
SEL_TOPK = 8
FORCE_SCORE = 1.0e4
WINDOW = 512
HGRN_HEADS = 4
HGRN_DK = 128
HGRN_DV = 128
HGRN_WIDTH = HGRN_HEADS * HGRN_DK
HGRN_VWIDTH = HGRN_HEADS * HGRN_DV
MLP_HIDDEN = 4 * D_MODEL
ROPE_THETA = 10000.0
LN_EPS = 1e-5
RMS_EPS = 1e-6
DEEPNORM_ALPHA = (2 * DEPTH) ** 0.25
IN_SIZES = (NSA_WIDTH,) + (NSA_KV_WIDTH,) * 6 + (3 * NSA_HEADS,) + (HGRN_WIDTH, HGRN_WIDTH, HGRN_VWIDTH, HGRN_VWIDTH) + (D_MODEL, D_MODEL)
IN_OFFSETS = [0] + [int(v) for v in np.cumsum(IN_SIZES)]

LANES = 128
SUBLANES = 8
VMEM_LIMIT = 48 * 1024 * 1024
MERGE_MLP_VMEM = 56 * 1024 * 1024
TOKEN_TILE = 512
Q_TILE = 128
K_CHUNK = 128
ATT_BLOCK = 4
NSA_STEP_TILES = 2
ATT_LOOKAHEAD = 2
HGRN_CHUNK = 64
HGRN_UNROLL = 4
HGRN_SUB = 8
HGRN_MAX_STEP_DECAY = 7.5
NEG_BIG = -1e30
LOG2E = 1.4426950408889634
Q_SCALE = NSA_DH ** -0.5 * LOG2E
SCORE_BOUND = 96.0
VT_ROWS = NSA_DH + 16

F32 = jnp.float32
BF16 = jnp.bfloat16


def _cparams(n_grid, vmem=VMEM_LIMIT):
    return pltpu.CompilerParams(dimension_semantics=("arbitrary",) * n_grid, vmem_limit_bytes=vmem)


def _resident(shape):
    nd = len(shape)
    return pl.BlockSpec(shape, lambda *_: (0,) * nd, pipeline_mode=pl.Buffered(1))


def _dot(a, b):
    return jnp.dot(a, b, preferred_element_type=F32)


def _dot_nt(a, b):
    return lax.dot_general(a, b, (((1,), (1,)), ((), ())), preferred_element_type=F32)


def _dot_tn(a, b):
    return lax.dot_general(a, b, (((0,), (0,)), ((), ())), preferred_element_type=F32)


def _sigmoid(x):
    return 1.0 / (1.0 + jnp.exp(-x))


def _silu(x):
    return x * _sigmoid(x)


def _layer_norm(z, g, b):
    mu = jnp.mean(z, axis=-1, keepdims=True)
    zc = z - mu
    var = jnp.mean(zc * zc, axis=-1, keepdims=True)
    return zc * lax.rsqrt(var + LN_EPS) * g + b


def _adaln_kernel(c_ref, w_ref, b_ref, o_ref):
    cond = _silu(c_ref[...]).astype(BF16)
    o_ref[0] = _dot(cond, w_ref[0]) + b_ref[0]


def adaln_mod(c, w_ada, b_ada):
    L, D, N = w_ada.shape
    B = c.shape[0]
    tn = D
    return pl.pallas_call(
        _adaln_kernel,
        grid=(L, N // tn),
        in_specs=[
            pl.BlockSpec((B, D), lambda l, j: (0, 0)),
            pl.BlockSpec((1, D, tn), lambda l, j: (l, 0, j)),
            pl.BlockSpec((1, 1, tn), lambda l, j: (l, 0, j)),
        ],
        out_specs=pl.BlockSpec((1, B, tn), lambda l, j: (l, 0, j)),
        out_shape=jax.ShapeDtypeStruct((L, B, N), F32),
        compiler_params=_cparams(2),
        name="adaln_mod",
    )(c, w_ada.astype(BF16), b_ada.reshape(L, 1, N))


N_ROPE = NSA_WIDTH + 3 * NSA_KV_WIDTH
N_NSA = N_ROPE + 3 * NSA_KV_WIDTH
N_GATE = NSA_GROUPS * LANES
N_HGRN = 2 * HGRN_WIDTH + 2 * HGRN_VWIDTH
N_MERGE = 2 * D_MODEL


def _in_proj_kernel(x_ref, sc_ref, sh_ref, cos_ref, sin_ref, lb_ref,
                    wn_ref, bn_ref, wg_ref, bg_ref, wh_ref, bh_ref, wm_ref, bm_ref,
                    q_ref, kx_ref, kv_ref, vt_ref, ga_ref, nrm_ref, hq_ref, hlf_ref, hv_ref, hg_ref, gma_ref, gmb_ref):
    u = (x_ref[0] * (1.0 + sc_ref[0]) + sh_ref[0]).astype(BF16)
    cos = cos_ref[...]
    sin = sin_ref[...]
    lane = lax.broadcasted_iota(jnp.int32, cos.shape, 1)
    first_half = (lane % NSA_DH) < (NSA_DH // 2)
    low = lane < NSA_DH

    def heads(t, upper):
        return jnp.where(low, t, upper), jnp.where(low, pltpu.roll(t, NSA_DH, 1), upper)

    n_q, n_rope = NSA_HEADS // 2, N_ROPE // LANES
    tm = u.shape[0]
    ones_rows = jnp.where(lax.broadcasted_iota(jnp.int32, (VT_ROWS - NSA_DH, tm), 0) == 0, 1.0, 0.0).astype(BF16)

    def max_sq_norm(t, acc):
        n = jnp.max(jnp.sum(t * t, axis=1, keepdims=True), axis=0, keepdims=True)
        return n if acc is None else jnp.maximum(acc, n)

    stat = {"q_sq": None, "k_sq": None}

    def attn_pair(i2):
        t2 = _dot(u, wn_ref[:, i2 * LANES:(i2 + 2) * LANES]) + bn_ref[:, i2 * LANES:(i2 + 2) * LANES]
        for i in (i2, i2 + 1):
            t = t2[:, (i - i2) * LANES:(i - i2 + 1) * LANES]
            if i < n_rope:
                rot = jnp.where(first_half, pltpu.roll(t, LANES - NSA_DH // 2, 1), pltpu.roll(t, NSA_DH // 2, 1))
                t = t * cos + rot * sin
            if i < n_q:
                t = t * Q_SCALE
                stat["q_sq"] = max_sq_norm(t, stat["q_sq"])
                for j, piece in enumerate(heads(t, 0.0)):
                    q_ref[0, 2 * i + j] = piece.astype(BF16)
            elif i == n_q or i == n_rope:
                for j in range(2):
                    kv_ref[0, (2 if i == n_rope else 0) + j] = t[:, j * NSA_DH:(j + 1) * NSA_DH].astype(BF16)
            elif i < n_rope:
                kind = i - n_q - 1
                stat["k_sq"] = max_sq_norm(t, stat["k_sq"])
                for j, piece in enumerate(heads(t, 0.0)):
                    kx_ref[0, 2 * kind + j] = piece.astype(BF16)
            else:
                kind = i - n_rope - 1
                tt = t.T.astype(BF16)
                for j in range(2):
                    vt_ref[0, 2 * kind + j] = jnp.concatenate([tt[j * NSA_DH:(j + 1) * NSA_DH], ones_rows], axis=0)

    def branch_gates():
        gates = _sigmoid(_dot(u, wg_ref[...]) + bg_ref[...])
        for g in range(NSA_GROUPS):
            ga_ref[0, g] = gates[:, g * LANES:(g + 1) * LANES].T[0:2 * SUBLANES]

    W = HGRN_WIDTH
    hgrn_cols = lambda i: _dot(u, wh_ref[:, i * W:(i + 1) * W]) + bh_ref[:, i * W:(i + 1) * W]

    def hgrn_q():
        hq_ref[0] = (_silu(hgrn_cols(0)) * (HGRN_DK ** -0.5)).astype(BF16)

    def hgrn_f():
        z = hgrn_cols(1)
        lb = lb_ref[...]
        log_sig = jnp.minimum(z, 0.0) - jnp.log(1.0 + jnp.exp(-jnp.abs(z)))
        a = jnp.log(lb)
        bb = jnp.log1p(-lb) + log_sig
        log_f = jnp.maximum(a, bb) + jnp.log(1.0 + jnp.exp(-jnp.abs(a - bb)))
        hlf_ref[0] = log_f
        stat["lf_min"] = jnp.min(jnp.min(log_f, axis=1, keepdims=True), axis=0, keepdims=True)

    def hgrn_v():
        hv_ref[0] = hgrn_cols(2).astype(BF16)

    def hgrn_g():
        hg_ref[0] = _silu(hgrn_cols(3)).astype(BF16)

    def merge_gate(k):
        ref, half = (gma_ref, gmb_ref)[k // 2], D_MODEL // 2
        c0 = k * half
        ref[0, :, (k % 2) * half:(k % 2 + 1) * half] = _sigmoid(
            _dot(u, wm_ref[:, c0:c0 + half]) + bm_ref[:, c0:c0 + half]).astype(BF16)

    P = functools.partial
    for group in (P(attn_pair, 0), P(merge_gate, 0), P(attn_pair, 2), P(merge_gate, 1), P(attn_pair, 4), hgrn_q,
                  hgrn_f, P(merge_gate, 2), P(attn_pair, 6), hgrn_g, P(attn_pair, 8), P(merge_gate, 3),
                  branch_gates, hgrn_v):
        group()
    srow = lax.broadcasted_iota(jnp.int32, (SUBLANES, LANES), 0)
    nrm_ref[0, 0] = jnp.where(srow == 0, stat["q_sq"], jnp.where(srow == 1, stat["k_sq"],
                                                                 jnp.where(srow == 2, stat["lf_min"], 0.0)))


def in_proj(x, sc, sh, cos_t, sin_t, lb, wts):
    B, S, D = x.shape
    tm = min(TOKEN_TILE, S)
    assert S // SEL_LEN <= LANES - NSA_DH
    wn, bn, wg, bg, wh, bh, wm, bm = wts
    tok = lambda n: pl.BlockSpec((1, tm, n), lambda b, i: (b, i, 0))
    per_b = pl.BlockSpec((1, 1, D), lambda b, i: (b, 0, 0))
    tab = pl.BlockSpec((tm, LANES), lambda b, i: (i, 0))
    out_shape = (
        jax.ShapeDtypeStruct((B, NSA_HEADS, S, LANES), BF16),
        jax.ShapeDtypeStruct((B, 2 * NSA_GROUPS, S, LANES), BF16),
        jax.ShapeDtypeStruct((B, 2 * NSA_GROUPS, S, NSA_DH), BF16),
        jax.ShapeDtypeStruct((B, 2 * NSA_GROUPS, VT_ROWS, S), BF16),
        jax.ShapeDtypeStruct((B, NSA_GROUPS, 2 * SUBLANES, S), F32),
        jax.ShapeDtypeStruct((B, S // tm, SUBLANES, LANES), F32),
        jax.ShapeDtypeStruct((B, S, HGRN_WIDTH), BF16),
        jax.ShapeDtypeStruct((B, S, HGRN_WIDTH), F32),
        jax.ShapeDtypeStruct((B, S, HGRN_VWIDTH), BF16),
        jax.ShapeDtypeStruct((B, S, HGRN_VWIDTH), BF16),
        jax.ShapeDtypeStruct((B, S, D), BF16),
        jax.ShapeDtypeStruct((B, S, D), BF16),
    )
    out_specs = (
        pl.BlockSpec((1, NSA_HEADS, tm, LANES), lambda b, i: (b, 0, i, 0)),
        pl.BlockSpec((1, 2 * NSA_GROUPS, tm, LANES), lambda b, i: (b, 0, i, 0)),
        pl.BlockSpec((1, 2 * NSA_GROUPS, tm, NSA_DH), lambda b, i: (b, 0, i, 0)),
        pl.BlockSpec((1, 2 * NSA_GROUPS, VT_ROWS, tm), lambda b, i: (b, 0, 0, i)),
        pl.BlockSpec((1, NSA_GROUPS, 2 * SUBLANES, tm), lambda b, i: (b, 0, 0, i)),
        pl.BlockSpec((1, 1, SUBLANES, LANES), lambda b, i: (b, i, 0, 0)),
        tok(HGRN_WIDTH), tok(HGRN_WIDTH), tok(HGRN_VWIDTH), tok(HGRN_VWIDTH), tok(D), tok(D),
    )
    return pl.pallas_call(
        _in_proj_kernel,
        grid=(B, S // tm),
        in_specs=[tok(D), per_b, per_b, tab, tab, _resident(lb.shape),
                  _resident(wn.shape), _resident(bn.shape), _resident(wg.shape), _resident(bg.shape),
                  _resident(wh.shape), _resident(bh.shape), _resident(wm.shape), _resident(bm.shape)],
        out_specs=out_specs,
        out_shape=out_shape,
        compiler_params=_cparams(2),
        name="in_proj",
    )(x, sc, sh, cos_t, sin_t, lb, wn, bn, wg, bg, wh, bh, wm, bm)


def _prep_in_proj_weights(w_in_l, b_in_l):
    o = IN_OFFSETS
    col = lambda i: (w_in_l[:, o[i]:o[i + 1]], b_in_l[o[i]:o[i + 1]])
    q_a, k_c, v_c, k_s, v_s, k_w, v_w, g_a, q_b, f_b, i_b, g_b, gm_a, gm_b = [col(i) for i in range(14)]

    def cat(parts):
        return (jnp.concatenate([p[0] for p in parts], axis=1).astype(BF16),
                jnp.concatenate([p[1] for p in parts], axis=0).reshape(1, -1).astype(F32))

    wn, bn = cat([q_a, k_c, k_s, k_w, v_c, v_s, v_w])
    per_group = 3 * NSA_REP
    gw = jnp.zeros((w_in_l.shape[0], N_GATE), w_in_l.dtype)
    gb = jnp.zeros((N_GATE,), b_in_l.dtype)
    for g in range(NSA_GROUPS):
        gw = gw.at[:, g * LANES:g * LANES + per_group].set(g_a[0][:, g * per_group:(g + 1) * per_group])
        gb = gb.at[g * LANES:g * LANES + per_group].set(g_a[1][g * per_group:(g + 1) * per_group])
    wg, bg = gw.astype(BF16), gb.reshape(1, -1).astype(F32)
    wh, bh = cat([q_b, f_b, i_b, g_b])
    wm, bm = cat([gm_a, gm_b])
    return wn, bn, wg, bg, wh, bh, wm, bm


def _compress_kernel(t_ref, pe_ref, w1_ref, w2_ref, o_ref):
    half = CMP_STRIDE * NSA_DH
    for s in range(2):
        pe = jnp.broadcast_to(pe_ref[s], (8, 2 * half)).astype(BF16)
        c = _dot(pe, w1_ref[s])[0:1]
        for g in range(NSA_GROUPS):
            t = t_ref[0, s * NSA_GROUPS + g]
            nrow = t.shape[0]
            a = _dot(t, w1_ref[s, 0:half, :])
            b = _dot(t, w1_ref[s, half:2 * half, :])
            h = a + pltpu.roll(b, nrow - 1, 0) + c
            o_ref[0, s * NSA_GROUPS + g] = _dot(_silu(h).astype(BF16), w2_ref[s]).astype(BF16)


def nsa_compress(kv, pe, w1, w2):
    B, _, S, dh = kv.shape
    nrow = S // CMP_STRIDE
    G = NSA_GROUPS
    kv_rows = kv.reshape(B, 2 * G, nrow, CMP_STRIDE * dh)
    return pl.pallas_call(
        _compress_kernel,
        grid=(B,),
        in_specs=[
            pl.BlockSpec((1, 2 * G, nrow, CMP_STRIDE * dh), lambda b: (b, 0, 0, 0)),
            _resident(pe.shape), _resident(w1.shape), _resident(w2.shape),
        ],
        out_specs=pl.BlockSpec((1, 2 * G, nrow, LANES), lambda b: (b, 0, 0, 0)),
        out_shape=jax.ShapeDtypeStruct((B, 2 * G, nrow, LANES), BF16),
        compiler_params=_cparams(1),
        name="nsa_compress",
    )(kv_rows, pe, w1, w2)


def _nsa_kernel(q_ref, cmp_ref, kx_ref, vt_ref, gt_ref, ovt_ref, o_ref, *, n_sel, n_tiles, bounded_scores):
    def step(i):
        gens = [_nsa_tile(i * NSA_STEP_TILES + qi, qi, g, q_ref, cmp_ref, kx_ref, vt_ref, gt_ref, ovt_ref, o_ref,
                          n_sel, bounded_scores)
                for qi in range(NSA_STEP_TILES) for g in range(NSA_GROUPS)]
        while gens:
            gens = [gen for gen in gens if next(gen, "done") != "done"]

    for i in range(n_tiles // NSA_STEP_TILES):
        pl.when(pl.program_id(1) == i)(functools.partial(step, i))


def _round_up(x, m):
    return -(-x // m) * m


def _nsa_tile(qb, qi, g, q_ref, cmp_ref, kx_ref, vt_ref, gt_ref, ovt_ref, o_ref, n_sel, bounded_scores):
    G, R, TQ, dh = NSA_GROUPS, NSA_REP, Q_TILE, NSA_DH
    cols = R * TQ
    s0 = qb * TQ
    tile_rows = slice(qi * TQ, (qi + 1) * TQ)
    kc_ref, vc_ref = cmp_ref.at[0, g], cmp_ref.at[0, G + g]
    ks_ref, kw_ref = kx_ref.at[0, g], kx_ref.at[0, G + g]
    vst_ref, vwt_ref = vt_ref.at[0, g], vt_ref.at[0, G + g]
    q = q_ref[0, g * R:(g + 1) * R, tile_rows].reshape(cols, LANES)
    t_lane = s0 + (lax.broadcasted_iota(jnp.int32, (1, cols), 1) % TQ)
    kidx = lax.broadcasted_iota(jnp.int32, (K_CHUNK, cols), 0)
    tq = lax.broadcasted_iota(jnp.int32, (K_CHUNK, cols), 1) % TQ
    causal = kidx <= tq

    def branch_blocks(k_ref, vt_ref, c0, n, band_first):
        return [(k_ref, vt_ref, c0 + b0, min(ATT_BLOCK, n - b0), band_first and b0 == 0, b0 + ATT_BLOCK >= n)
                for b0 in range(0, n, ATT_BLOCK)]

    def score_block(blk):
        k_ref, _, c, n, band, diag = blk
        s = _dot_nt(k_ref[c * K_CHUNK:(c + n) * K_CHUNK, :], q)
        parts = [s[i * K_CHUNK:(i + 1) * K_CHUNK] for i in range(n)]
        if band:
            parts[0] = jnp.where(tq < kidx, parts[0], NEG_BIG)
        if diag:
            parts[-1] = jnp.where(causal, parts[-1], NEG_BIG)
        return jnp.concatenate(parts, axis=0) if n > 1 else parts[0]

    def finish_block(s, blk, block_bias=None):
        _, vt_ref, c, n, _, _ = blk
        nk = n * K_CHUNK
        m = None
        if block_bias is not None:
            j0 = c * K_CHUNK // SEL_LEN
            subs = [s[i * SEL_LEN:(i + 1) * SEL_LEN] for i in range(nk // SEL_LEN)]
            bias = [block_bias[j0 + i:j0 + i + 1, :] for i in range(nk // SEL_LEN)]
        if bounded_scores:
            p = jnp.exp2(s) if block_bias is None else jnp.concatenate(
                [jnp.exp2(s_i + b_i) for s_i, b_i in zip(subs, bias)], axis=0)
        elif block_bias is None:
            m = jnp.max(s, axis=0, keepdims=True)
            p = jnp.exp2(s - m)
        else:
            for s_i, b_i in zip(subs, bias):
                m_i = jnp.max(s_i, axis=0, keepdims=True) + b_i
                m = m_i if m is None else jnp.maximum(m, m_i)
            shift = jnp.where(m < 0.5 * NEG_BIG, 0.0, m)
            p = jnp.concatenate([jnp.exp2(s_i + (b_i - shift)) for s_i, b_i in zip(subs, bias)], axis=0)
        vt = vt_ref[:, c * K_CHUNK:c * K_CHUNK + nk]
        return m, _dot(vt, p.astype(BF16))

    def combine(stats):
        total = stats[0][1]
        if bounded_scores:
            for _, acc_i in stats[1:]:
                total = total + acc_i
        elif len(stats) > 1:
            m = stats[0][0]
            for st in stats[1:]:
                m = jnp.maximum(m, st[0])
            total = None
            for m_i, acc_i in stats:
                w = jnp.exp2(m_i - m)
                total = w * acc_i if total is None else total + w * acc_i
        return total[0:dh] * (1.0 / total[dh:dh + 1])

    n_win = WINDOW // K_CHUNK
    win_blocks = branch_blocks(kw_ref, vwt_ref, max(qb - n_win, 0), min(qb, n_win) + 1, qb >= n_win)
    blocks = win_blocks + branch_blocks(ks_ref, vst_ref, 0, qb + 1, False)

    ncb = min(kc_ref.shape[0], _round_up((s0 + TQ - CMP_LEN) // CMP_STRIDE + 1, 2 * SUBLANES))
    sc = _dot_nt(kc_ref[0:ncb, :], q)
    pending = {i: score_block(blocks[i]) for i in range(min(ATT_LOOKAHEAD, len(blocks)))}
    yield
    n_sub = lax.broadcasted_iota(jnp.int32, (ncb, cols), 0)
    mask_c = n_sub * CMP_STRIDE + (CMP_LEN - 1) <= t_lane
    sc = jnp.where(mask_c, sc, NEG_BIG)
    mc = jnp.max(sc, axis=0, keepdims=True)
    ec = jnp.where(mask_c, jnp.exp2(sc - mc), 0.0)
    pc = ec * (1.0 / jnp.maximum(jnp.sum(ec, axis=0, keepdims=True), 1e-30))
    o_c = _dot_tn(vc_ref[0:ncb, :], pc.astype(BF16))[0:dh]

    nb_live = (s0 + TQ) // SEL_LEN
    nb = min(ovt_ref.shape[0], _round_up(nb_live, 2 * SUBLANES))
    psum = pc[:, 0:TQ]
    for r in range(1, R):
        psum = psum + pc[:, r * TQ:(r + 1) * TQ]
    p_hi = psum.astype(BF16)
    p_lo = (psum - p_hi.astype(F32)).astype(BF16)
    ovt = ovt_ref[0:nb, 0:ncb]
    imp = _dot(ovt, p_hi) + _dot(ovt, p_lo)
    yield
    jb =lax.broadcasted_iota(jnp.int32, (nb, TQ), 0)
    tb = (s0 + lax.broadcasted_iota(jnp.int32, (nb, TQ), 1)) // SEL_LEN
    valid = jb <= tb
    forced = jnp.where(valid, jnp.where(jb == 0, 1.0, jnp.where(jb >= tb - 1, 1.0, 0.0)), 0.0)
    score = jnp.where(forced > 0.5, FORCE_SCORE, jnp.where(valid, imp, -1.0))
    rank = jnp.zeros((nb, TQ), F32)
    for i in range(nb_live):
        si = score[i:i + 1, :]
        tie_first = jnp.where(jb > i, 1.0, 0.0)
        rank = rank + jnp.where(si > score, 1.0, jnp.where(si == score, tie_first, 0.0))
    sel_bias = jnp.where(rank < n_sel, 0.0, NEG_BIG)
    sel_bias = jnp.concatenate([sel_bias] * R, axis=1)

    stats = []
    for i, blk in enumerate(blocks):
        is_sel = i >= len(win_blocks)
        stats.append(finish_block(pending.pop(i), blk, sel_bias if is_sel else None))
        if i + ATT_LOOKAHEAD < len(blocks):
            pending[i + ATT_LOOKAHEAD] = score_block(blocks[i + ATT_LOOKAHEAD])
        yield
    o_w = combine(stats[:len(win_blocks)])
    o_s = combine(stats[len(win_blocks):])

    gate = gt_ref[0, g, :, tile_rows]
    pieces = []
    for r in range(R):
        sl = slice(r * TQ, (r + 1) * TQ)
        o_r = (gate[3 * r:3 * r + 1, :] * o_c[:, sl] + gate[3 * r + 1:3 * r + 2, :] * o_s[:, sl]
               + gate[3 * r + 2:3 * r + 3, :] * o_w[:, sl])
        pieces.append(o_r.T)
    o_ref[0, tile_rows, g * R * dh:(g + 1) * R * dh] = jnp.concatenate(pieces, axis=1).astype(BF16)


def nsa_attend(q, kx, vt, cmp, gates_t, bounded_scores):
    B, H, S, _ = q.shape
    G, R, dh = NSA_GROUPS, NSA_REP, NSA_DH
    ncb = S // CMP_STRIDE
    nb = S // SEL_LEN
    assert (S % Q_TILE == 0 and Q_TILE == K_CHUNK and WINDOW % K_CHUNK == 0 and K_CHUNK % SEL_LEN == 0
            and 3 * R <= 2 * SUBLANES)
    cstart = np.arange(ncb) * CMP_STRIDE
    sstart = np.arange(nb) * SEL_LEN
    overlap = ((cstart[:, None] < sstart[None, :] + SEL_LEN) & (cstart[:, None] + CMP_LEN > sstart[None, :]))
    ovt = jnp.asarray(overlap.T, BF16)
    assert (S // Q_TILE) % NSA_STEP_TILES == 0
    tq = NSA_STEP_TILES * Q_TILE
    per_batch = lambda a: pl.BlockSpec((1,) + a.shape[1:], lambda b, i: (b, 0, 0, 0))
    return pl.pallas_call(
        functools.partial(_nsa_kernel, n_sel=min(SEL_TOPK, nb), n_tiles=S // Q_TILE, bounded_scores=bounded_scores),
        grid=(B, S // tq),
        in_specs=[
            pl.BlockSpec((1, H, tq, LANES), lambda b, i: (b, 0, i, 0)),
            per_batch(cmp), per_batch(kx), per_batch(vt),
            pl.BlockSpec((1, G, 2 * SUBLANES, tq), lambda b, i: (b, 0, 0, i)),
            pl.BlockSpec(ovt.shape, lambda b, i: (0, 0)),
        ],
        out_specs=pl.BlockSpec((1, tq, H * dh), lambda b, i: (b, i, 0)),
        out_shape=jax.ShapeDtypeStruct((B, S, H * dh), BF16),
        compiler_params=_cparams(2),
        name="nsa_attend",
    )(q, cmp, kx, vt, gates_t, ovt)


def _hgrn_kernel(q_ref, lf_ref, v_ref, g_ref, ng_ref, tri_ref, o_ref, st_ref, *, bounded_decay):
    C, SB = HGRN_CHUNK, HGRN_SUB
    NBK = C // SB
    S = q_ref.shape[1]
    st_ref[...] = jnp.zeros(st_ref.shape, F32)
    row = lax.broadcasted_iota(jnp.int32, (C, HGRN_DK), 0)
    sub_row = lax.broadcasted_iota(jnp.int32, (SB, 1), 0)
    cr = lax.broadcasted_iota(jnp.int32, (C, C), 0)
    cc = lax.broadcasted_iota(jnp.int32, (C, C), 1)
    diag_mask = (cr // SB == cc // SB) & (cc <= cr)

    def head_chunk(h, r0):
        hs = slice(h * HGRN_DK, (h + 1) * HGRN_DK)
        q = q_ref[0, pl.ds(r0, C), hs].astype(F32)
        lf = lf_ref[0, pl.ds(r0, C), hs]
        v_bf = v_ref[0, pl.ds(r0, C), hs]
        v = v_bf.astype(F32)
        kh = 1.0 - jnp.exp(lf)
        tri = tri_ref[...]
        lf0 = lf.astype(BF16)
        lf1 = (lf - lf0.astype(F32)).astype(BF16)
        lf2 = (lf - lf0.astype(F32) - lf1.astype(F32)).astype(BF16)
        b = _dot(tri, lf0) + _dot(tri, lf1) + _dot(tri, lf2)
        yield
        b_last = b[C - 1:C, :]
        st = st_ref[h]
        o = _dot_nt((q * jnp.exp(b)).astype(BF16), st.astype(BF16))
        b_end = jnp.concatenate(
            [jnp.broadcast_to(b[(j + 1) * SB - 1:(j + 1) * SB, :], (SB, HGRN_DK)) for j in range(NBK)], axis=0)
        k_end = kh * jnp.exp(b_end - b)
        q_parts, k_parts = [], []
        for j in range(NBK - 1):
            lo = (j + 1) * SB
            qj = q[lo:] * jnp.exp(b[lo:] - b[lo - 1:lo, :])
            q_parts.append(jnp.concatenate([jnp.zeros((lo, HGRN_DK), F32), qj], axis=0))
            k_parts.append(jnp.where((row >= j * SB) & (row < lo), k_end, 0.0))
        q_cat = jnp.concatenate(q_parts, axis=1).astype(BF16)
        k_cat = jnp.concatenate(k_parts, axis=1).astype(BF16)
        a_off = _dot_nt(q_cat, k_cat)
        k_last = (kh * jnp.exp(b_last - b)).astype(BF16)
        st_ref[h] = st * jnp.exp(b_last) + _dot_tn(v_bf, k_last)
        if bounded_decay:
            b_start = jnp.concatenate([jnp.zeros((SB, HGRN_DK), F32), b_end[:C - SB]], axis=0)
            a_dg = _dot_nt((q * jnp.exp(b - b_start)).astype(BF16), (kh * jnp.exp(b_start - b)).astype(BF16))
            yield
            o = o + _dot(jnp.where(diag_mask, a_dg, a_off).astype(BF16), v_bf)
            yield
        else:
            yield
            o = o + _dot(a_off.astype(BF16), v_bf)
            diag = []
            for j in range(NBK):
                sl = slice(j * SB, (j + 1) * SB)
                qj, bj, kj, vj = q[sl], b[sl], kh[sl], v[sl]
                oj = jnp.zeros((SB, HGRN_DV), F32)
                for s in range(SB):
                    w = jnp.exp(jnp.minimum(bj - bj[s:s + 1, :], 0.0))
                    a = jnp.sum(qj * kj[s:s + 1, :] * w, axis=-1, keepdims=True)
                    a = jnp.where(sub_row >= s, a, 0.0)
                    oj = oj + a * vj[s:s + 1, :]
                diag.append(oj)
            o = o + jnp.concatenate(diag, axis=0)
            yield
        o = o * lax.rsqrt(jnp.mean(o * o, axis=-1, keepdims=True) + RMS_EPS) * ng_ref[...]
        o_ref[0, pl.ds(r0, C), hs] = (o * g_ref[0, pl.ds(r0, C), hs].astype(F32)).astype(BF16)

    def chunk(ci, carry):
        r0 = pl.multiple_of(ci * (C * HGRN_UNROLL), C * HGRN_UNROLL)
        heads = [head_chunk(h, r0 + u * C) for u in range(HGRN_UNROLL) for h in range(HGRN_HEADS)]
        for _ in range(3):
            for gen in heads:
                next(gen)
        for gen in heads:
            next(gen, None)
        return carry

    lax.fori_loop(0, S // (C * HGRN_UNROLL), chunk, 0)


def hgrn2(hq, hlf, hv, hg, norm_g, bounded_decay):
    B, S, _ = hq.shape
    C = HGRN_CHUNK
    assert S % (C * HGRN_UNROLL) == 0
    tri = jnp.asarray(np.tril(np.ones((C, C))), BF16)
    spec = lambda n: pl.BlockSpec((1, S, n), lambda b: (b, 0, 0))
    return pl.pallas_call(
        functools.partial(_hgrn_kernel, bounded_decay=bounded_decay),
        grid=(B,),
        in_specs=[spec(HGRN_WIDTH), spec(HGRN_WIDTH), spec(HGRN_VWIDTH), spec(HGRN_VWIDTH),
                  pl.BlockSpec((1, HGRN_DV), lambda b: (0, 0)),
                  pl.BlockSpec((C, C), lambda b: (0, 0))],
        out_specs=spec(HGRN_VWIDTH),
        out_shape=jax.ShapeDtypeStruct((B, S, HGRN_VWIDTH), BF16),
        scratch_shapes=[pltpu.VMEM((HGRN_HEADS, HGRN_DV, HGRN_DK), F32)],
        compiler_params=_cparams(1),
        name="hgrn2",
    )(hq, hlf, hv, hg, norm_g.reshape(1, HGRN_DV).astype(F32), tri)


MLP_COLS = 1024


def _merge_mlp_kernel(ya_ref, yb_ref, gma_ref, gmb_ref, x_ref, mod_ref, ln_ref,
                      wa_ref, wb_ref, wo_ref, w1_ref, w2_ref, o_ref):
    gt1, sh2, sc2, gt2 = [mod_ref[0, i:i + 1, :] for i in range(4)]
    g1, b1, g2, b2 = [ln_ref[i:i + 1, :] for i in range(4)]
    pa = _dot(ya_ref[0], wa_ref[...])
    pb = _dot(yb_ref[0], wb_ref[...])
    merged = gma_ref[0].astype(F32) * pa + gmb_ref[0].astype(F32) * pb
    y = _dot(merged.astype(BF16), wo_ref[...])
    x = _layer_norm(DEEPNORM_ALPHA * x_ref[0] + (1.0 + gt1) * y, g1, b1)
    u = (x * (1.0 + sc2) + sh2).astype(BF16)
    y = jnp.zeros(x.shape, F32)
    for c in range(MLP_HIDDEN // MLP_COLS):
        h = jnp.maximum(_dot(u, w1_ref[:, c * MLP_COLS:(c + 1) * MLP_COLS]), 0.0)
        y = y + _dot((h * h).astype(BF16), w2_ref[c * MLP_COLS:(c + 1) * MLP_COLS, :])
    o_ref[0] = _layer_norm(DEEPNORM_ALPHA * x + (1.0 + gt2) * y, g2, b2)


def merge_mlp(ya, yb, gma, gmb, x, mod4, ln4, wa, wb, wo, w1, w2):
    B, S, D = x.shape
    tm = min(TOKEN_TILE, S)
    tok = lambda n: pl.BlockSpec((1, tm, n), lambda b, i: (b, i, 0))
    return pl.pallas_call(
        _merge_mlp_kernel,
        grid=(B, S // tm),
        in_specs=[tok(NSA_WIDTH), tok(HGRN_VWIDTH), tok(D), tok(D), tok(D),
                  pl.BlockSpec((1, 4, D), lambda b, i: (b, 0, 0)), _resident(ln4.shape),
                  _resident(wa.shape), _resident(wb.shape), _resident(wo.shape),
                  _resident(w1.shape), _resident(w2.shape)],
        out_specs=tok(D),
        out_shape=jax.ShapeDtypeStruct((B, S, D), F32),
        compiler_params=_cparams(2, vmem=MERGE_MLP_VMEM),
        name="merge_mlp",
    )(ya, yb, gma, gmb, x, mod4, ln4, wa, wb, wo, w1, w2)


def _rope_tables(S):
    inv = 1.0 / (ROPE_THETA ** (jnp.arange(0, NSA_DH, 2, dtype=F32) / NSA_DH))
    ang = jnp.arange(S, dtype=F32)[:, None] * inv[None, :]
    cos, sin = jnp.cos(ang), jnp.sin(ang)
    reps = LANES // NSA_DH
    return (jnp.tile(jnp.concatenate([cos, cos], axis=1), (1, reps)),
            jnp.tile(jnp.concatenate([-sin, sin], axis=1), (1, reps)))


def kernel(x, c, w_in, b_in, cmp_pe_k, cmp_pe_v, cmp_wk1, cmp_wk2, cmp_wv1, cmp_wv2, hgrn_lb_logits, hgrn_norm_g, w_branch_a, w_branch_b, w_out, w_ada, b_ada, ln1_g, ln1_b, w_mlp1, w_mlp2, ln2_g, ln2_b):
    B, S, D = x.shape
    G = NSA_GROUPS
    lb_all = jnp.cumsum(jax.nn.softmax(hgrn_lb_logits.astype(F32), axis=0), axis=0)
    lb_all = lb_all - lb_all[0:1]
    cos_t, sin_t = _rope_tables(S)
    mod = adaln_mod(c, w_ada, b_ada)
    for l in range(DEPTH):
        sh1, sc1, gt1, sh2, sc2, gt2 = [mod[l, :, None, i * D:(i + 1) * D] for i in range(6)]
        wts = _prep_in_proj_weights(w_in[l], b_in[l])
        q, kx, kv, vt, gates_t, nrm, hq, hlf, hv, hg, gma, gmb = in_proj(x, sc1, sh1, cos_t, sin_t, lb_all[l].reshape(1, -1), wts)
        pe = jnp.stack([cmp_pe_k[l].reshape(1, -1), cmp_pe_v[l].reshape(1, -1)])
        w1 = jnp.stack([cmp_wk1[l], cmp_wv1[l]]).astype(BF16)
        w2 = jnp.pad(jnp.stack([cmp_wk2[l], cmp_wv2[l]]), ((0, 0), (0, 0), (0, LANES - NSA_DH))).astype(BF16)
        cmp = nsa_compress(kv, pe, w1, w2)
        bounded = jnp.max(nrm[:, :, 0, 0]) * jnp.max(nrm[:, :, 1, 0]) <= SCORE_BOUND ** 2
        ya = lax.cond(bounded,
                      functools.partial(nsa_attend, bounded_scores=True),
                      functools.partial(nsa_attend, bounded_scores=False),
                      q, kx, vt, cmp, gates_t)
        yb = lax.cond(jnp.min(nrm[:, :, 2, 0]) >= -HGRN_MAX_STEP_DECAY,
                      functools.partial(hgrn2, bounded_decay=True),
                      functools.partial(hgrn2, bounded_decay=False),
                      hq, hlf, hv, hg, hgrn_norm_g[l])
        mod4 = jnp.concatenate([gt1, sh2, sc2, gt2], axis=1)
        ln4 = jnp.stack([ln1_g[l], ln1_b[l], ln2_g[l], ln2_b[l]])
        x = merge_mlp(ya, yb, gma, gmb, x, mod4, ln4,
                      w_branch_a[l].astype(BF16), w_branch_b[l].astype(BF16), w_out[l].astype(BF16),
                      w_mlp1[l].astype(BF16), w_mlp2[l].astype(BF16))
    return x
```

```python
import functools

import numpy as np
import jax
import jax.numpy as jnp
from jax import lax
from jax.experimental import pallas as pl
from jax.experimental.pallas import tpu as pltpu

D_MODEL = 1024
DEPTH = 2
NSA_HEADS = 8
NSA_GROUPS = 2
NSA_REP = NSA_HEADS // NSA_GROUPS
NSA_DH = 64
NSA_WIDTH = NSA_HEADS * NSA_DH
NSA_KV_WIDTH = NSA_GROUPS * NSA_DH
CMP_LEN = 32
CMP_STRIDE = 16
CMP_HIDDEN = 2 * NSA_DH
SEL_LEN = 64
SEL_TOPK = 8
FORCE_SCORE = 1.0e4
WINDOW = 512
HGRN_HEADS = 4
HGRN_DK = 128
HGRN_DV = 128
HGRN_WIDTH = HGRN_HEADS * HGRN_DK
HGRN_VWIDTH = HGRN_HEADS * HGRN_DV
MLP_HIDDEN = 4 * D_MODEL
ROPE_THETA = 10000.0
LN_EPS = 1e-5
RMS_EPS = 1e-6
DEEPNORM_ALPHA = (2 * DEPTH) ** 0.25
IN_SIZES = (NSA_WIDTH,) + (NSA_KV_WIDTH,) * 6 + (3 * NSA_HEADS,) + (HGRN_WIDTH, HGRN_WIDTH, HGRN_VWIDTH, HGRN_VWIDTH) + (D_MODEL, D_MODEL)
IN_OFFSETS = [0] + [int(v) for v in np.cumsum(IN_SIZES)]

LANES = 128
SUBLANES = 8
VMEM_LIMIT = 48 * 1024 * 1024
MERGE_MLP_VMEM = 56 * 1024 * 1024
TOKEN_TILE = 512
Q_TILE = 128
K_CHUNK = 128
ATT_BLOCK = 4
NSA_STEP_TILES = 4
ATT_LOOKAHEAD = 2
HGRN_CHUNK = 64
HGRN_UNROLL = 4
HGRN_SUB = 8
HGRN_MAX_STEP_DECAY = 7.5
NEG_BIG = -1e30
LOG2E = 1.4426950408889634
Q_SCALE = NSA_DH ** -0.5 * LOG2E
SCORE_BOUND = 96.0
VT_ROWS = NSA_DH + 16

F32 = jnp.float32
BF16 = jnp.bfloat16


def _cparams(n_grid, vmem=VMEM_LIMIT):
    return pltpu.CompilerParams(dimension_semantics=("arbitrary",) * n_grid, vmem_limit_bytes=vmem)


def _resident(shape):
    nd = len(shape)
    return pl.BlockSpec(shape, lambda *_: (0,) * nd, pipeline_mode=pl.Buffered(1))


def _dot(a, b):
    return jnp.dot(a, b, preferred_element_type=F32)


def _dot_nt(a, b):
    return lax.dot_general(a, b, (((1,), (1,)), ((), ())), preferred_element_type=F32)


def _dot_tn(a, b):
    return lax.dot_general(a, b, (((0,), (0,)), ((), ())), preferred_element_type=F32)


def _sigmoid(x):
    return 1.0 / (1.0 + jnp.exp(-x))


def _silu(x):
    return x * _sigmoid(x)


def _layer_norm(z, g, b):
    mu = jnp.mean(z, axis=-1, keepdims=True)
    zc = z - mu
    var = jnp.mean(zc * zc, axis=-1, keepdims=True)
    return zc * lax.rsqrt(var + LN_EPS) * g + b


def _adaln_kernel(c_ref, w_ref, b_ref, o_ref):
    cond = _silu(c_ref[...]).astype(BF16)
    o_ref[0] = _dot(cond, w_ref[0]) + b_ref[0]


def adaln_mod(c, w_ada, b_ada):
    L, D, N = w_ada.shape
    B = c.shape[0]
    tn = D
    return pl.pallas_call(
        _adaln_kernel,
        grid=(L, N // tn),
        in_specs=[
            pl.BlockSpec((B, D), lambda l, j: (0, 0)),
            pl.BlockSpec((1, D, tn), lambda l, j: (l, 0, j)),
            pl.BlockSpec((1, 1, tn), lambda l, j: (l, 0, j)),
        ],
        out_specs=pl.BlockSpec((1, B, tn), lambda l, j: (l, 0, j)),
        out_shape=jax.ShapeDtypeStruct((L, B, N), F32),
        compiler_params=_cparams(2),
        name="adaln_mod",
    )(c, w_ada.astype(BF16), b_ada.reshape(L, 1, N))


N_ROPE = NSA_WIDTH + 3 * NSA_KV_WIDTH
N_NSA = N_ROPE + 3 * NSA_KV_WIDTH
N_GATE = NSA_GROUPS * LANES
N_HGRN = 2 * HGRN_WIDTH + 2 * HGRN_VWIDTH
N_MERGE = 2 * D_MODEL


def _in_proj_kernel(x_ref, sc_ref, sh_ref, cos_ref, sin_ref, lb_ref,
                    wn_ref, bn_ref, wg_ref, bg_ref, wh_ref, bh_ref, wm_ref, bm_ref,
                    q_ref, kx_ref, kv_ref, vt_ref, ga_ref, nrm_ref, hq_ref, hlf_ref, hv_ref, hg_ref, gma_ref, gmb_ref):
    u = (x_ref[0] * (1.0 + sc_ref[0]) + sh_ref[0]).astype(BF16)
    cos = cos_ref[...]
    sin = sin_ref[...]
    lane = lax.broadcasted_iota(jnp.int32, cos.shape, 1)
    first_half = (lane % NSA_DH) < (NSA_DH // 2)
    low = lane < NSA_DH

    def heads(t, upper):
        return jnp.where(low, t, upper), jnp.where(low, pltpu.roll(t, NSA_DH, 1), upper)

    n_q, n_rope = NSA_HEADS // 2, N_ROPE // LANES
    tm = u.shape[0]
    ones_rows = jnp.where(lax.broadcasted_iota(jnp.int32, (VT_ROWS - NSA_DH, tm), 0) == 0, 1.0, 0.0).astype(BF16)

    def max_sq_norm(t, acc):
        n = jnp.max(jnp.sum(t * t, axis=1, keepdims=True), axis=0, keepdims=True)
        return n if acc is None else jnp.maximum(acc, n)

    stat = {"q_sq": None, "k_sq": None}

    def attn_pair(i2):
        t2 = _dot(u, wn_ref[:, i2 * LANES:(i2 + 2) * LANES]) + bn_ref[:, i2 * LANES:(i2 + 2) * LANES]
        for i in (i2, i2 + 1):
            t = t2[:, (i - i2) * LANES:(i - i2 + 1) * LANES]
            if i < n_rope:
                rot = jnp.where(first_half, pltpu.roll(t, LANES - NSA_DH // 2, 1), pltpu.roll(t, NSA_DH // 2, 1))
                t = t * cos + rot * sin
            if i < n_q:
                t = t * Q_SCALE
                stat["q_sq"] = max_sq_norm(t, stat["q_sq"])
                for j, piece in enumerate(heads(t, 0.0)):
                    q_ref[0, 2 * i + j] = piece.astype(BF16)
            elif i == n_q or i == n_rope:
                for j in range(2):
                    kv_ref[0, (2 if i == n_rope else 0) + j] = t[:, j * NSA_DH:(j + 1) * NSA_DH].astype(BF16)
            elif i < n_rope:
                kind = i - n_q - 1
                stat["k_sq"] = max_sq_norm(t, stat["k_sq"])
                for j, piece in enumerate(heads(t, 0.0)):
                    kx_ref[0, 2 * kind + j] = piece.astype(BF16)
            else:
                kind = i - n_rope - 1
                tt = t.T.astype(BF16)
                for j in range(2):
                    vt_ref[0, 2 * kind + j] = jnp.concatenate([tt[j * NSA_DH:(j + 1) * NSA_DH], ones_rows], axis=0)

    def branch_gates():
        gates = _sigmoid(_dot(u, wg_ref[...]) + bg_ref[...])
        for g in range(NSA_GROUPS):
            ga_ref[0, g] = gates[:, g * LANES:(g + 1) * LANES].T[0:2 * SUBLANES]

    W = HGRN_WIDTH
    hgrn_cols = lambda i: _dot(u, wh_ref[:, i * W:(i + 1) * W]) + bh_ref[:, i * W:(i + 1) * W]

    def hgrn_q():
        hq_ref[0] = (_silu(hgrn_cols(0)) * (HGRN_DK ** -0.5)).astype(BF16)

    def hgrn_f():
        z = hgrn_cols(1)
        lb = lb_ref[...]
        log_sig = jnp.minimum(z, 0.0) - jnp.log(1.0 + jnp.exp(-jnp.abs(z)))
        a = jnp.log(lb)
        bb = jnp.log1p(-lb) + log_sig
        log_f = jnp.maximum(a, bb) + jnp.log(1.0 + jnp.exp(-jnp.abs(a - bb)))
        hlf_ref[0] = log_f
        stat["lf_min"] = jnp.min(jnp.min(log_f, axis=1, keepdims=True), axis=0, keepdims=True)

    def hgrn_v():
        hv_ref[0] = hgrn_cols(2).astype(BF16)

    def hgrn_g():
        hg_ref[0] = _silu(hgrn_cols(3)).astype(BF16)

    def merge_gate(k):
        ref, half = (gma_ref, gmb_ref)[k // 2], D_MODEL // 2
        c0 = k * half
        ref[0, :, (k % 2) * half:(k % 2 + 1) * half] = _sigmoid(
            _dot(u, wm_ref[:, c0:c0 + half]) + bm_ref[:, c0:c0 + half]).astype(BF16)

    P = functools.partial
    for group in (P(attn_pair, 0), P(merge_gate, 0), P(attn_pair, 2), P(merge_gate, 1), P(attn_pair, 4), hgrn_q,
                  hgrn_f, P(merge_gate, 2), P(attn_pair, 6), hgrn_g, P(attn_pair, 8), P(merge_gate, 3),
                  branch_gates, hgrn_v):
        group()
    srow = lax.broadcasted_iota(jnp.int32, (SUBLANES, LANES), 0)
    nrm_ref[0, 0] = jnp.where(srow == 0, stat["q_sq"], jnp.where(srow == 1, stat["k_sq"],
                                                                 jnp.where(srow == 2, stat["lf_min"], 0.0)))


def in_proj(x, sc, sh, cos_t, sin_t, lb, wts):
    B, S, D = x.shape
    tm = min(TOKEN_TILE, S)
    assert S // SEL_LEN <= LANES - NSA_DH
    wn, bn, wg, bg, wh, bh, wm, bm = wts
    tok = lambda n: pl.BlockSpec((1, tm, n), lambda b, i: (b, i, 0))
    per_b = pl.BlockSpec((1, 1, D), lambda b, i: (b, 0, 0))
    tab = pl.BlockSpec((tm, LANES), lambda b, i: (i, 0))
    out_shape = (
        jax.ShapeDtypeStruct((B, NSA_HEADS, S, LANES), BF16),
        jax.ShapeDtypeStruct((B, 2 * NSA_GROUPS, S, LANES), BF16),
        jax.ShapeDtypeStruct((B, 2 * NSA_GROUPS, S, NSA_DH), BF16),
        jax.ShapeDtypeStruct((B, 2 * NSA_GROUPS, VT_ROWS, S), BF16),
        jax.ShapeDtypeStruct((B, NSA_GROUPS, 2 * SUBLANES, S), F32),
        jax.ShapeDtypeStruct((B, S // tm, SUBLANES, LANES), F32),
        jax.ShapeDtypeStruct((B, S, HGRN_WIDTH), BF16),
        jax.ShapeDtypeStruct((B, S, HGRN_WIDTH), F32),
        jax.ShapeDtypeStruct((B, S, HGRN_VWIDTH), BF16),
        jax.ShapeDtypeStruct((B, S, HGRN_VWIDTH), BF16),
        jax.ShapeDtypeStruct((B, S, D), BF16),
        jax.ShapeDtypeStruct((B, S, D), BF16),
    )
    out_specs = (
        pl.BlockSpec((1, NSA_HEADS, tm, LANES), lambda b, i: (b, 0, i, 0)),
        pl.BlockSpec((1, 2 * NSA_GROUPS, tm, LANES), lambda b, i: (b, 0, i, 0)),
        pl.BlockSpec((1, 2 * NSA_GROUPS, tm, NSA_DH), lambda b, i: (b, 0, i, 0)),
        pl.BlockSpec((1, 2 * NSA_GROUPS, VT_ROWS, tm), lambda b, i: (b, 0, 0, i)),
        pl.BlockSpec((1, NSA_GROUPS, 2 * SUBLANES, tm), lambda b, i: (b, 0, 0, i)),
        pl.BlockSpec((1, 1, SUBLANES, LANES), lambda b, i: (b, i, 0, 0)),
        tok(HGRN_WIDTH), tok(HGRN_WIDTH), tok(HGRN_VWIDTH), tok(HGRN_VWIDTH), tok(D), tok(D),
    )
    return pl.pallas_call(
        _in_proj_kernel,
        grid=(B, S // tm),
        in_specs=[tok(D), per_b, per_b, tab, tab, _resident(lb.shape),
                  _resident(wn.shape), _resident(bn.shape), _resident(wg.shape), _resident(bg.shape),
                  _resident(wh.shape), _resident(bh.shape), _resident(wm.shape), _resident(bm.shape)],
        out_specs=out_specs,
        out_shape=out_shape,
        compiler_params=_cparams(2),
        name="in_proj",
    )(x, sc, sh, cos_t, sin_t, lb, wn, bn, wg, bg, wh, bh, wm, bm)


def _prep_in_proj_weights(w_in_l, b_in_l):
    o = IN_OFFSETS
    col = lambda i: (w_in_l[:, o[i]:o[i + 1]], b_in_l[o[i]:o[i + 1]])
    q_a, k_c, v_c, k_s, v_s, k_w, v_w, g_a, q_b, f_b, i_b, g_b, gm_a, gm_b = [col(i) for i in range(14)]

    def cat(parts):
        return (jnp.concatenate([p[0] for p in parts], axis=1).astype(BF16),
                jnp.concatenate([p[1] for p in parts], axis=0).reshape(1, -1).astype(F32))

    wn, bn = cat([q_a, k_c, k_s, k_w, v_c, v_s, v_w])
    per_group = 3 * NSA_REP
    gw = jnp.zeros((w_in_l.shape[0], N_GATE), w_in_l.dtype)
    gb = jnp.zeros((N_GATE,), b_in_l.dtype)
    for g in range(NSA_GROUPS):
        gw = gw.at[:, g * LANES:g * LANES + per_group].set(g_a[0][:, g * per_group:(g + 1) * per_group])
        gb = gb.at[g * LANES:g * LANES + per_group].set(g_a[1][g * per_group:(g + 1) * per_group])
    wg, bg = gw.astype(BF16), gb.reshape(1, -1).astype(F32)
    wh, bh = cat([q_b, f_b, i_b, g_b])
    wm, bm = cat([gm_a, gm_b])
    return wn, bn, wg, bg, wh, bh, wm, bm


def _compress_kernel(t_ref, pe_ref, w1_ref, w2_ref, o_ref):
    half = CMP_STRIDE * NSA_DH
    for s in range(2):
        pe = jnp.broadcast_to(pe_ref[s], (8, 2 * half)).astype(BF16)
        c = _dot(pe, w1_ref[s])[0:1]
        for g in range(NSA_GROUPS):
            t = t_ref[0, s * NSA_GROUPS + g]
            nrow = t.shape[0]
            a = _dot(t, w1_ref[s, 0:half, :])
            b = _dot(t, w1_ref[s, half:2 * half, :])
            h = a + pltpu.roll(b, nrow - 1, 0) + c
            o_ref[0, s * NSA_GROUPS + g] = _dot(_silu(h).astype(BF16), w2_ref[s]).astype(BF16)


def nsa_compress(kv, pe, w1, w2):
    B, _, S, dh = kv.shape
    nrow = S // CMP_STRIDE
    G = NSA_GROUPS
    kv_rows = kv.reshape(B, 2 * G, nrow, CMP_STRIDE * dh)
    return pl.pallas_call(
        _compress_kernel,
        grid=(B,),
        in_specs=[
            pl.BlockSpec((1, 2 * G, nrow, CMP_STRIDE * dh), lambda b: (b, 0, 0, 0)),
            _resident(pe.shape), _resident(w1.shape), _resident(w2.shape),
        ],
        out_specs=pl.BlockSpec((1, 2 * G, nrow, LANES), lambda b: (b, 0, 0, 0)),
        out_shape=jax.ShapeDtypeStruct((B, 2 * G, nrow, LANES), BF16),
        compiler_params=_cparams(1),
        name="nsa_compress",
    )(kv_rows, pe, w1, w2)


def _nsa_kernel(q_ref, cmp_ref, kx_ref, vt_ref, gt_ref, ovt_ref, o_ref, *, n_sel, n_tiles, bounded_scores):
    def step(i):
        gens = [_nsa_tile(i * NSA_STEP_TILES + qi, qi, g, q_ref, cmp_ref, kx_ref, vt_ref, gt_ref, ovt_ref, o_ref,
                          n_sel, bounded_scores)
                for qi in range(NSA_STEP_TILES) for g in range(NSA_GROUPS)]
        while gens:
            gens = [gen for gen in gens if next(gen, "done") != "done"]

    for i in range(n_tiles // NSA_STEP_TILES):
        pl.when(pl.program_id(1) == i)(functools.partial(step, i))


def _round_up(x, m):
    return -(-x // m) * m


def _nsa_tile(qb, qi, g, q_ref, cmp_ref, kx_ref, vt_ref, gt_ref, ovt_ref, o_ref, n_sel, bounded_scores):
    G, R, TQ, dh = NSA_GROUPS, NSA_REP, Q_TILE, NSA_DH
    cols = R * TQ
    s0 = qb * TQ
    tile_rows = slice(qi * TQ, (qi + 1) * TQ)
    kc_ref, vc_ref = cmp_ref.at[0, g], cmp_ref.at[0, G + g]
    ks_ref, kw_ref = kx_ref.at[0, g], kx_ref.at[0, G + g]
    vst_ref, vwt_ref = vt_ref.at[0, g], vt_ref.at[0, G + g]
    q = q_ref[0, g * R:(g + 1) * R, tile_rows].reshape(cols, LANES)
    t_lane = s0 + (lax.broadcasted_iota(jnp.int32, (1, cols), 1) % TQ)
    kidx = lax.broadcasted_iota(jnp.int32, (K_CHUNK, cols), 0)
    tq = lax.broadcasted_iota(jnp.int32, (K_CHUNK, cols), 1) % TQ
    causal = kidx <= tq

    def branch_blocks(k_ref, vt_ref, c0, n, band_first):
        return [(k_ref, vt_ref, c0 + b0, min(ATT_BLOCK, n - b0), band_first and b0 == 0, b0 + ATT_BLOCK >= n)
                for b0 in range(0, n, ATT_BLOCK)]

    def score_block(blk):
        k_ref, _, c, n, band, diag = blk
        s = _dot_nt(k_ref[c * K_CHUNK:(c + n) * K_CHUNK, :], q)
        parts = [s[i * K_CHUNK:(i + 1) * K_CHUNK] for i in range(n)]
        if band:
            parts[0] = jnp.where(tq < kidx, parts[0], NEG_BIG)
        if diag:
            parts[-1] = jnp.where(causal, parts[-1], NEG_BIG)
        return jnp.concatenate(parts, axis=0) if n > 1 else parts[0]

    def finish_block(s, blk, block_bias=None):
        _, vt_ref, c, n, _, _ = blk
        nk = n * K_CHUNK
        m = None
        if block_bias is not None:
            j0 = c * K_CHUNK // SEL_LEN
            subs = [s[i * SEL_LEN:(i + 1) * SEL_LEN] for i in range(nk // SEL_LEN)]
            bias = [block_bias[j0 + i:j0 + i + 1, :] for i in range(nk // SEL_LEN)]
        if bounded_scores:
            p = jnp.exp2(s) if block_bias is None else jnp.concatenate(
                [jnp.exp2(s_i + b_i) for s_i, b_i in zip(subs, bias)], axis=0)
        elif block_bias is None:
            m = jnp.max(s, axis=0, keepdims=True)
            p = jnp.exp2(s - m)
        else:
            for s_i, b_i in zip(subs, bias):
                m_i = jnp.max(s_i, axis=0, keepdims=True) + b_i
                m = m_i if m is None else jnp.maximum(m, m_i)
            shift = jnp.where(m < 0.5 * NEG_BIG, 0.0, m)
            p = jnp.concatenate([jnp.exp2(s_i + (b_i - shift)) for s_i, b_i in zip(subs, bias)], axis=0)
        vt = vt_ref[:, c * K_CHUNK:c * K_CHUNK + nk]
        return m, _dot(vt, p.astype(BF16))

    def combine(stats):
        total = stats[0][1]
        if bounded_scores:
            for _, acc_i in stats[1:]:
                total = total + acc_i
        elif len(stats) > 1:
            m = stats[0][0]
            for st in stats[1:]:
                m = jnp.maximum(m, st[0])
            total = None
            for m_i, acc_i in stats:
                w = jnp.exp2(m_i - m)
                total = w * acc_i if total is None else total + w * acc_i
        return total[0:dh] * (1.0 / total[dh:dh + 1])

    n_win = WINDOW // K_CHUNK
    win_blocks = branch_blocks(kw_ref, vwt_ref, max(qb - n_win, 0), min(qb, n_win) + 1, qb >= n_win)
    blocks = win_blocks + branch_blocks(ks_ref, vst_ref, 0, qb + 1, False)

    ncb = min(kc_ref.shape[0], _round_up((s0 + TQ - CMP_LEN) // CMP_STRIDE + 1, 2 * SUBLANES))
    sc = _dot_nt(kc_ref[0:ncb, :], q)
    pending = {i: score_block(blocks[i]) for i in range(min(ATT_LOOKAHEAD, len(blocks)))}
    yield
    n_sub = lax.broadcasted_iota(jnp.int32, (ncb, cols), 0)
    mask_c = n_sub * CMP_STRIDE + (CMP_LEN - 1) <= t_lane
    sc = jnp.where(mask_c, sc, NEG_BIG)
    mc = jnp.max(sc, axis=0, keepdims=True)
    ec = jnp.where(mask_c, jnp.exp2(sc - mc), 0.0)
    pc = ec * (1.0 / jnp.maximum(jnp.sum(ec, axis=0, keepdims=True), 1e-30))
    o_c = _dot_tn(vc_ref[0:ncb, :], pc.astype(BF16))[0:dh]

    nb_live = (s0 + TQ) // SEL_LEN
    nb = min(ovt_ref.shape[0], _round_up(nb_live, 2 * SUBLANES))
    psum = pc[:, 0:TQ]
    for r in range(1, R):
        psum = psum + pc[:, r * TQ:(r + 1) * TQ]
    p_hi = psum.astype(BF16)
    p_lo = (psum - p_hi.astype(F32)).astype(BF16)
    ovt = ovt_ref[0:nb, 0:ncb]
    imp = _dot(ovt, p_hi) + _dot(ovt, p_lo)
    yield
    jb =lax.broadcasted_iota(jnp.int32, (nb, TQ), 0)
    tb = (s0 + lax.broadcasted_iota(jnp.int32, (nb, TQ), 1)) // SEL_LEN
    valid = jb <= tb
    forced = jnp.where(valid, jnp.where(jb == 0, 1.0, jnp.where(jb >= tb - 1, 1.0, 0.0)), 0.0)
    score = jnp.where(forced > 0.5, FORCE_SCORE, jnp.where(valid, imp, -1.0))
    rank = jnp.zeros((nb, TQ), F32)
    for i in range(nb_live):
        si = score[i:i + 1, :]
        tie_first = jnp.where(jb > i, 1.0, 0.0)
        rank = rank + jnp.where(si > score, 1.0, jnp.where(si == score, tie_first, 0.0))
    sel_bias = jnp.where(rank < n_sel, 0.0, NEG_BIG)
    sel_bias = jnp.concatenate([sel_bias] * R, axis=1)

    stats = []
    for i, blk in enumerate(blocks):
        is_sel = i >= len(win_blocks)
        stats.append(finish_block(pending.pop(i), blk, sel_bias if is_sel else None))
        if i + ATT_LOOKAHEAD < len(blocks):
            pending[i + ATT_LOOKAHEAD] = score_block(blocks[i + ATT_LOOKAHEAD])
        yield
    o_w = combine(stats[:len(win_blocks)])
    o_s = combine(stats[len(win_blocks):])

    gate = gt_ref[0, g, :, tile_rows]
    pieces = []
    for r in range(R):
        sl = slice(r * TQ, (r + 1) * TQ)
        o_r = (gate[3 * r:3 * r + 1, :] * o_c[:, sl] + gate[3 * r + 1:3 * r + 2, :] * o_s[:, sl]
               + gate[3 * r + 2:3 * r + 3, :] * o_w[:, sl])
        pieces.append(o_r.T)
    o_ref[0, tile_rows, g * R * dh:(g + 1) * R * dh] = jnp.concatenate(pieces, axis=1).astype(BF16)


def nsa_attend(q, kx, vt, cmp, gates_t, bounded_scores):
    B, H, S, _ = q.shape
    G, R, dh = NSA_GROUPS, NSA_REP, NSA_DH
    ncb = S // CMP_STRIDE
    nb = S // SEL_LEN
    assert (S % Q_TILE == 0 and Q_TILE == K_CHUNK and WINDOW % K_CHUNK == 0 and K_CHUNK % SEL_LEN == 0
            and 3 * R <= 2 * SUBLANES)
    cstart = np.arange(ncb) * CMP_STRIDE
    sstart = np.arange(nb) * SEL_LEN
    overlap = ((cstart[:, None] < sstart[None, :] + SEL_LEN) & (cstart[:, None] + CMP_LEN > sstart[None, :]))
    ovt = jnp.asarray(overlap.T, BF16)
    assert (S // Q_TILE) % NSA_STEP_TILES == 0
    tq = NSA_STEP_TILES * Q_TILE
    per_batch = lambda a: pl.BlockSpec((1,) + a.shape[1:], lambda b, i: (b, 0, 0, 0))
    return pl.pallas_call(
        functools.partial(_nsa_kernel, n_sel=min(SEL_TOPK, nb), n_tiles=S // Q_TILE, bounded_scores=bounded_scores),
        grid=(B, S // tq),
        in_specs=[
            pl.BlockSpec((1, H, tq, LANES), lambda b, i: (b, 0, i, 0)),
            per_batch(cmp), per_batch(kx), per_batch(vt),
            pl.BlockSpec((1, G, 2 * SUBLANES, tq), lambda b, i: (b, 0, 0, i)),
            pl.BlockSpec(ovt.shape, lambda b, i: (0, 0)),
        ],
        out_specs=pl.BlockSpec((1, tq, H * dh), lambda b, i: (b, i, 0)),
        out_shape=jax.ShapeDtypeStruct((B, S, H * dh), BF16),
        compiler_params=_cparams(2),
        name="nsa_attend",
    )(q, cmp, kx, vt, gates_t, ovt)


def _hgrn_kernel(q_ref, lf_ref, v_ref, g_ref, ng_ref, tri_ref, o_ref, st_ref, *, bounded_decay):
    C, SB = HGRN_CHUNK, HGRN_SUB
    NBK = C // SB
    S = q_ref.shape[1]
    st_ref[...] = jnp.zeros(st_ref.shape, F32)
    row = lax.broadcasted_iota(jnp.int32, (C, HGRN_DK), 0)
    sub_row = lax.broadcasted_iota(jnp.int32, (SB, 1), 0)
    cr = lax.broadcasted_iota(jnp.int32, (C, C), 0)
    cc = lax.broadcasted_iota(jnp.int32, (C, C), 1)
    diag_mask = (cr // SB == cc // SB) & (cc <= cr)

    def head_chunk(h, r0):
        hs = slice(h * HGRN_DK, (h + 1) * HGRN_DK)
        q = q_ref[0, pl.ds(r0, C), hs].astype(F32)
        lf = lf_ref[0, pl.ds(r0, C), hs]
        v_bf = v_ref[0, pl.ds(r0, C), hs]
        v = v_bf.astype(F32)
        kh = 1.0 - jnp.exp(lf)
        tri = tri_ref[...]
        lf0 = lf.astype(BF16)
        lf1 = (lf - lf0.astype(F32)).astype(BF16)
        lf2 = (lf - lf0.astype(F32) - lf1.astype(F32)).astype(BF16)
        b = _dot(tri, lf0) + _dot(tri, lf1) + _dot(tri, lf2)
        yield
        b_last = b[C - 1:C, :]
        st = st_ref[h]
        o = _dot_nt((q * jnp.exp(b)).astype(BF16), st.astype(BF16))
        b_end = jnp.concatenate(
            [jnp.broadcast_to(b[(j + 1) * SB - 1:(j + 1) * SB, :], (SB, HGRN_DK)) for j in range(NBK)], axis=0)
        k_end = kh * jnp.exp(b_end - b)
        q_parts, k_parts = [], []
        for j in range(NBK - 1):
            lo = (j + 1) * SB
            qj = q[lo:] * jnp.exp(b[lo:] - b[lo - 1:lo, :])
            q_parts.append(jnp.concatenate([jnp.zeros((lo, HGRN_DK), F32), qj], axis=0))
            k_parts.append(jnp.where((row >= j * SB) & (row < lo), k_end, 0.0))
        q_cat = jnp.concatenate(q_parts, axis=1).astype(BF16)
        k_cat = jnp.concatenate(k_parts, axis=1).astype(BF16)
        a_off = _dot_nt(q_cat, k_cat)
        k_last = (kh * jnp.exp(b_last - b)).astype(BF16)
        st_ref[h] = st * jnp.exp(b_last) + _dot_tn(v_bf, k_last)
        if bounded_decay:
            b_start = jnp.concatenate([jnp.zeros((SB, HGRN_DK), F32), b_end[:C - SB]], axis=0)
            a_dg = _dot_nt((q * jnp.exp(b - b_start)).astype(BF16), (kh * jnp.exp(b_start - b)).astype(BF16))
            yield
            o = o + _dot(jnp.where(diag_mask, a_dg, a_off).astype(BF16), v_bf)
            yield
        else:
            yield
            o = o + _dot(a_off.astype(BF16), v_bf)
            diag = []
            for j in range(NBK):
                sl = slice(j * SB, (j + 1) * SB)
                qj, bj, kj, vj = q[sl], b[sl], kh[sl], v[sl]
                oj = jnp.zeros((SB, HGRN_DV), F32)
                for s in range(SB):
                    w = jnp.exp(jnp.minimum(bj - bj[s:s + 1, :], 0.0))
                    a = jnp.sum(qj * kj[s:s + 1, :] * w, axis=-1, keepdims=True)
                    a = jnp.where(sub_row >= s, a, 0.0)
                    oj = oj + a * vj[s:s + 1, :]
                diag.append(oj)
            o = o + jnp.concatenate(diag, axis=0)
            yield
        o = o * lax.rsqrt(jnp.mean(o * o, axis=-1, keepdims=True) + RMS_EPS) * ng_ref[...]
        o_ref[0, pl.ds(r0, C), hs] = (o * g_ref[0, pl.ds(r0, C), hs].astype(F32)).astype(BF16)

    def chunk(ci, carry):
        r0 = pl.multiple_of(ci * (C * HGRN_UNROLL), C * HGRN_UNROLL)
        heads = [head_chunk(h, r0 + u * C) for u in range(HGRN_UNROLL) for h in range(HGRN_HEADS)]
        for _ in range(3):
            for gen in heads:
                next(gen)
        for gen in heads:
            next(gen, None)
        return carry

    lax.fori_loop(0, S // (C * HGRN_UNROLL), chunk, 0)


def hgrn2(hq, hlf, hv, hg, norm_g, bounded_decay):
    B, S, _ = hq.shape
    C = HGRN_CHUNK
    assert S % (C * HGRN_UNROLL) == 0
    tri = jnp.asarray(np.tril(np.ones((C, C))), BF16)
    spec = lambda n: pl.BlockSpec((1, S, n), lambda b: (b, 0, 0))
    return pl.pallas_call(
        functools.partial(_hgrn_kernel, bounded_decay=bounded_decay),
        grid=(B,),
        in_specs=[spec(HGRN_WIDTH), spec(HGRN_WIDTH), spec(HGRN_VWIDTH), spec(HGRN_VWIDTH),
                  pl.BlockSpec((1, HGRN_DV), lambda b: (0, 0)),
                  pl.BlockSpec((C, C), lambda b: (0, 0))],
        out_specs=spec(HGRN_VWIDTH),
        out_shape=jax.ShapeDtypeStruct((B, S, HGRN_VWIDTH), BF16),
        scratch_shapes=[pltpu.VMEM((HGRN_HEADS, HGRN_DV, HGRN_DK), F32)],
        compiler_params=_cparams(1),
        name="hgrn2",
    )(hq, hlf, hv, hg, norm_g.reshape(1, HGRN_DV).astype(F32), tri)


MLP_COLS = 1024


def _merge_mlp_kernel(ya_ref, yb_ref, gma_ref, gmb_ref, x_ref, mod_ref, ln_ref,
                      wa_ref, wb_ref, wo_ref, w1_ref, w2_ref, o_ref):
    gt1, sh2, sc2, gt2 = [mod_ref[0, i:i + 1, :] for i in range(4)]
    g1, b1, g2, b2 = [ln_ref[i:i + 1, :] for i in range(4)]
    pa = _dot(ya_ref[0], wa_ref[...])
    pb = _dot(yb_ref[0], wb_ref[...])
    merged = gma_ref[0].astype(F32) * pa + gmb_ref[0].astype(F32) * pb
    y = _dot(merged.astype(BF16), wo_ref[...])
    x = _layer_norm(DEEPNORM_ALPHA * x_ref[0] + (1.0 + gt1) * y, g1, b1)
    u = (x * (1.0 + sc2) + sh2).astype(BF16)
    y = jnp.zeros(x.shape, F32)
    for c in range(MLP_HIDDEN // MLP_COLS):
        h = jnp.maximum(_dot(u, w1_ref[:, c * MLP_COLS:(c + 1) * MLP_COLS]), 0.0)
        y = y + _dot((h * h).astype(BF16), w2_ref[c * MLP_COLS:(c + 1) * MLP_COLS, :])
    o_ref[0] = _layer_norm(DEEPNORM_ALPHA * x + (1.0 + gt2) * y, g2, b2)


def merge_mlp(ya, yb, gma, gmb, x, mod4, ln4, wa, wb, wo, w1, w2):
    B, S, D = x.shape
    tm = min(TOKEN_TILE, S)
    tok = lambda n: pl.BlockSpec((1, tm, n), lambda b, i: (b, i, 0))
    return pl.pallas_call(
        _merge_mlp_kernel,
        grid=(B, S // tm),
        in_specs=[tok(NSA_WIDTH), tok(HGRN_VWIDTH), tok(D), tok(D), tok(D),
                  pl.BlockSpec((1, 4, D), lambda b, i: (b, 0, 0)), _resident(ln4.shape),
                  _resident(wa.shape), _resident(wb.shape), _resident(wo.shape),
                  _resident(w1.shape), _resident(w2.shape)],
        out_specs=tok(D),
        out_shape=jax.ShapeDtypeStruct((B, S, D), F32),
        compiler_params=_cparams(2, vmem=MERGE_MLP_VMEM),
        name="merge_mlp",
    )(ya, yb, gma, gmb, x, mod4, ln4, wa, wb, wo, w1, w2)


def _rope_tables(S):
    inv = 1.0 / (ROPE_THETA ** (jnp.arange(0, NSA_DH, 2, dtype=F32) / NSA_DH))
    ang = jnp.arange(S, dtype=F32)[:, None] * inv[None, :]
    cos, sin = jnp.cos(ang), jnp.sin(ang)
    reps = LANES // NSA_DH
    return (jnp.tile(jnp.concatenate([cos, cos], axis=1), (1, reps)),
            jnp.tile(jnp.concatenate([-sin, sin], axis=1), (1, reps)))


def kernel(x, c, w_in, b_in, cmp_pe_k, cmp_pe_v, cmp_wk1, cmp_wk2, cmp_wv1, cmp_wv2, hgrn_lb_logits, hgrn_norm_g, w_branch_a, w_branch_b, w_out, w_ada, b_ada, ln1_g, ln1_b, w_mlp1, w_mlp2, ln2_g, ln2_b):
    B, S, D = x.shape
    G = NSA_GROUPS
    lb_all = jnp.cumsum(jax.nn.softmax(hgrn_lb_logits.astype(F32), axis=0), axis=0)
    lb_all = lb_all - lb_all[0:1]
    cos_t, sin_t = _rope_tables(S)
    mod = adaln_mod(c, w_ada, b_ada)
    for l in range(DEPTH):
        sh1, sc1, gt1, sh2, sc2, gt2 = [mod[l, :, None, i * D:(i + 1) * D] for i in range(6)]
        wts = _prep_in_proj_weights(w_in[l], b_in[l])
        q, kx, kv, vt, gates_t, nrm, hq, hlf, hv, hg, gma, gmb = in_proj(x, sc1, sh1, cos_t, sin_t, lb_all[l].reshape(1, -1), wts)
        pe = jnp.stack([cmp_pe_k[l].reshape(1, -1), cmp_pe_v[l].reshape(1, -1)])
        w1 = jnp.stack([cmp_wk1[l], cmp_wv1[l]]).astype(BF16)
        w2 = jnp.pad(jnp.stack([cmp_wk2[l], cmp_wv2[l]]), ((0, 0), (0, 0), (0, LANES - NSA_DH))).astype(BF16)
        cmp = nsa_compress(kv, pe, w1, w2)
        bounded = jnp.max(nrm[:, :, 0, 0]) * jnp.max(nrm[:, :, 1, 0]) <= SCORE_BOUND ** 2
        ya = lax.cond(bounded,
                      functools.partial(nsa_attend, bounded_scores=True),
                      functools.partial(nsa_attend, bounded_scores=False),
                      q, kx, vt, cmp, gates_t)
        yb = lax.cond(jnp.min(nrm[:, :, 2, 0]) >= -HGRN_MAX_STEP_DECAY,
                      functools.partial(hgrn2, bounded_decay=True),
                      functools.partial(hgrn2, bounded_decay=False),
                      hq, hlf, hv, hg, hgrn_norm_g[l])
        mod4 = jnp.concatenate([gt1, sh2, sc2, gt2], axis=1)
        ln4 = jnp.stack([ln1_g[l], ln1_b[l], ln2_g[l], ln2_b[l]])
        x = merge_mlp(ya, yb, gma, gmb, x, mod4, ln4,
                      w_branch_a[l].astype(BF16), w_branch_b[l].astype(BF16), w_out[l].astype(BF16),
                      w_mlp1[l].astype(BF16), w_mlp2[l].astype(BF16))
    return x
```

```python
import functools

import numpy as np
import jax
import jax.numpy as jnp
from jax import lax
from jax.experimental import pallas as pl
from jax.experimental.pallas import tpu as pltpu

D_MODEL = 1024
DEPTH = 2
NSA_HEADS = 8
NSA_GROUPS = 2
NSA_REP = NSA_HEADS // NSA_GROUPS
NSA_DH = 64
NSA_WIDTH = NSA_HEADS * NSA_DH
NSA_KV_WIDTH = NSA_GROUPS * NSA_DH
CMP_LEN = 32
CMP_STRIDE = 16
CMP_HIDDEN = 2 * NSA_DH
SEL_LEN = 64
SEL_TOPK = 8
FORCE_SCORE = 1.0e4
WINDOW = 512
HGRN_HEADS = 4
HGRN_DK = 128
HGRN_DV = 128
HGRN_WIDTH = HGRN_HEADS * HGRN_DK
HGRN_VWIDTH = HGRN_HEADS * HGRN_DV
MLP_HIDDEN = 4 * D_MODEL
ROPE_THETA = 10000.0
LN_EPS = 1e-5
RMS_EPS = 1e-6
DEEPNORM_ALPHA = (2 * DEPTH) ** 0.25
IN_SIZES = (NSA_WIDTH,) + (NSA_KV_WIDTH,) * 6 + (3 * NSA_HEADS,) + (HGRN_WIDTH, HGRN_WIDTH, HGRN_VWIDTH, HGRN_VWIDTH) + (D_MODEL, D_MODEL)
IN_OFFSETS = [0] + [int(v) for v in np.cumsum(IN_SIZES)]

LANES = 128
SUBLANES = 8
VMEM_LIMIT = 48 * 1024 * 1024
MERGE_MLP_VMEM = 56 * 1024 * 1024
TOKEN_TILE = 512
Q_TILE = 128
K_CHUNK = 128
ATT_BLOCK = 4
NSA_STEP_TILES = 4
ATT_LOOKAHEAD = 2
HGRN_CHUNK = 64
HGRN_UNROLL = 4
HGRN_SUB = 8
HGRN_MAX_STEP_DECAY = 7.5
NEG_BIG = -1e30
LOG2E = 1.4426950408889634
Q_SCALE = NSA_DH ** -0.5 * LOG2E
SCORE_BOUND = 96.0
VT_ROWS = NSA_DH + 16

F32 = jnp.float32
BF16 = jnp.bfloat16


def _cparams(n_grid, vmem=VMEM_LIMIT):
    return pltpu.CompilerParams(dimension_semantics=("arbitrary",) * n_grid, vmem_limit_bytes=vmem)


def _resident(shape):
    nd = len(shape)
    return pl.BlockSpec(shape, lambda *_: (0,) * nd, pipeline_mode=pl.Buffered(1))


def _dot(a, b):
    return jnp.dot(a, b, preferred_element_type=F32)


def _dot_nt(a, b):
    return lax.dot_general(a, b, (((1,), (1,)), ((), ())), preferred_element_type=F32)


def _dot_tn(a, b):
    return lax.dot_general(a, b, (((0,), (0,)), ((), ())), preferred_element_type=F32)


def _sigmoid(x):
    return 1.0 / (1.0 + jnp.exp(-x))


def _silu(x):
    return x * _sigmoid(x)


def _layer_norm(z, g, b):
    mu = jnp.mean(z, axis=-1, keepdims=True)
    zc = z - mu
    var = jnp.mean(zc * zc, axis=-1, keepdims=True)
    return zc * lax.rsqrt(var + LN_EPS) * g + b


def _adaln_kernel(c_ref, w_ref, b_ref, o_ref):
    cond = _silu(c_ref[...]).astype(BF16)
    o_ref[0] = _dot(cond, w_ref[0]) + b_ref[0]


def adaln_mod(c, w_ada, b_ada):
    L, D, N = w_ada.shape
    B = c.shape[0]
    tn = D
    return pl.pallas_call(
        _adaln_kernel,
        grid=(L, N // tn),
        in_specs=[
            pl.BlockSpec((B, D), lambda l, j: (0, 0)),
            pl.BlockSpec((1, D, tn), lambda l, j: (l, 0, j)),
            pl.BlockSpec((1, 1, tn), lambda l, j: (l, 0, j)),
        ],
        out_specs=pl.BlockSpec((1, B, tn), lambda l, j: (l, 0, j)),
        out_shape=jax.ShapeDtypeStruct((L, B, N), F32),
        compiler_params=_cparams(2),
        name="adaln_mod",
    )(c, w_ada.astype(BF16), b_ada.reshape(L, 1, N))


N_ROPE = NSA_WIDTH + 3 * NSA_KV_WIDTH
N_NSA = N_ROPE + 3 * NSA_KV_WIDTH
N_GATE = NSA_GROUPS * LANES
N_HGRN = 2 * HGRN_WIDTH + 2 * HGRN_VWIDTH
N_MERGE = 2 * D_MODEL


def _in_proj_kernel(x_ref, sc_ref, sh_ref, cos_ref, sin_ref, lb_ref,
                    wn_ref, bn_ref, wg_ref, bg_ref, wh_ref, bh_ref, wm_ref, bm_ref, perm_ref,
                    q_ref, kx_ref, kv_ref, vt_ref, ga_ref, nrm_ref, hq_ref, hlf_ref, hv_ref, hg_ref, gma_ref, gmb_ref):
    u = (x_ref[0] * (1.0 + sc_ref[0]) + sh_ref[0]).astype(BF16)
    cos = cos_ref[...]
    sin = sin_ref[...]
    lane = lax.broadcasted_iota(jnp.int32, cos.shape, 1)
    first_half = (lane % NSA_DH) < (NSA_DH // 2)
    low = lane < NSA_DH

    def heads(t, upper):
        return jnp.where(low, t, upper), jnp.where(low, pltpu.roll(t, NSA_DH, 1), upper)

    n_q, n_rope = NSA_HEADS // 2, N_ROPE // LANES
    tm = u.shape[0]
    ones_rows = jnp.where(lax.broadcasted_iota(jnp.int32, (VT_ROWS - NSA_DH, tm), 0) == 0, 1.0, 0.0).astype(BF16)

    def max_sq_norm(t, acc):
        n = jnp.max(jnp.sum(t * t, axis=1, keepdims=True), axis=0, keepdims=True)
        return n if acc is None else jnp.maximum(acc, n)

    stat = {"q_sq": None, "k_sq": None}

    def attn_pair(i2):
        t2 = _dot(u, wn_ref[:, i2 * LANES:(i2 + 2) * LANES]) + bn_ref[:, i2 * LANES:(i2 + 2) * LANES]
        for i in (i2, i2 + 1):
            t = t2[:, (i - i2) * LANES:(i - i2 + 1) * LANES]
            if i < n_rope:
                rot = jnp.where(first_half, pltpu.roll(t, LANES - NSA_DH // 2, 1), pltpu.roll(t, NSA_DH // 2, 1))
                t = t * cos + rot * sin
            if i < n_q:
                t = t * Q_SCALE
                stat["q_sq"] = max_sq_norm(t, stat["q_sq"])
                for j, piece in enumerate(heads(t, 0.0)):
                    q_ref[0, 2 * i + j] = piece.astype(BF16)
            elif i == n_q or i == n_rope:
                nr = tm // CMP_STRIDE
                r = _dot(perm_ref[...], t.astype(BF16))
                low_r = lax.broadcasted_iota(jnp.int32, (nr, LANES), 1) < NSA_DH
                for m in range(CMP_STRIDE // 2):
                    ra, rb = r[2 * m * nr:(2 * m + 1) * nr], r[(2 * m + 1) * nr:(2 * m + 2) * nr]
                    pair = (jnp.where(low_r, ra, pltpu.roll(rb, NSA_DH, 1)),
                            jnp.where(low_r, pltpu.roll(ra, NSA_DH, 1), rb))
                    for g in range(NSA_GROUPS):
                        kv_ref[0, (2 if i == n_rope else 0) + g, :, m * LANES:(m + 1) * LANES] = pair[g].astype(BF16)
            elif i < n_rope:
                kind = i - n_q - 1
                stat["k_sq"] = max_sq_norm(t, stat["k_sq"])
                for j, piece in enumerate(heads(t, 0.0)):
                    kx_ref[0, 2 * kind + j] = piece.astype(BF16)
            else:
                kind = i - n_rope - 1
                tt = t.T.astype(BF16)
                for j in range(2):
                    vt_ref[0, 2 * kind + j] = jnp.concatenate([tt[j * NSA_DH:(j + 1) * NSA_DH], ones_rows], axis=0)

    def branch_gates():
        gates = _sigmoid(_dot(u, wg_ref[...]) + bg_ref[...])
        for g in range(NSA_GROUPS):
            ga_ref[0, g] = gates[:, g * LANES:(g + 1) * LANES].T[0:2 * SUBLANES]

    W = HGRN_WIDTH
    hgrn_cols = lambda i: _dot(u, wh_ref[:, i * W:(i + 1) * W]) + bh_ref[:, i * W:(i + 1) * W]

    def hgrn_q():
        hq_ref[0] = (_silu(hgrn_cols(0)) * (HGRN_DK ** -0.5)).astype(BF16)

    def hgrn_f():
        z = hgrn_cols(1)
        lb = lb_ref[...]
        log_sig = jnp.minimum(z, 0.0) - jnp.log(1.0 + jnp.exp(-jnp.abs(z)))
        a = jnp.log(lb)
        bb = jnp.log1p(-lb) + log_sig
        log_f = jnp.maximum(a, bb) + jnp.log(1.0 + jnp.exp(-jnp.abs(a - bb)))
        hlf_ref[0] = log_f
        stat["lf_min"] = jnp.min(jnp.min(log_f, axis=1, keepdims=True), axis=0, keepdims=True)

    def hgrn_v():
        hv_ref[0] = hgrn_cols(2).astype(BF16)

    def hgrn_g():
        hg_ref[0] = _silu(hgrn_cols(3)).astype(BF16)

    def merge_gate(k):
        ref, half = (gma_ref, gmb_ref)[k // 2], D_MODEL // 2
        c0 = k * half
        ref[0, :, (k % 2) * half:(k % 2 + 1) * half] = _sigmoid(
            _dot(u, wm_ref[:, c0:c0 + half]) + bm_ref[:, c0:c0 + half]).astype(BF16)

    P = functools.partial
    for group in (P(attn_pair, 0), P(merge_gate, 0), P(attn_pair, 2), P(merge_gate, 1), P(attn_pair, 4), hgrn_q,
                  hgrn_f, P(merge_gate, 2), P(attn_pair, 6), hgrn_g, P(attn_pair, 8), P(merge_gate, 3),
                  branch_gates, hgrn_v):
        group()
    srow = lax.broadcasted_iota(jnp.int32, (SUBLANES, LANES), 0)
    nrm_ref[0, 0] = jnp.where(srow == 0, stat["q_sq"], jnp.where(srow == 1, stat["k_sq"],
                                                                 jnp.where(srow == 2, stat["lf_min"], 0.0)))


def in_proj(x, sc, sh, cos_t, sin_t, lb, wts):
    B, S, D = x.shape
    tm = min(TOKEN_TILE, S)
    wn, bn, wg, bg, wh, bh, wm, bm = wts
    nr = tm // CMP_STRIDE
    src = (np.arange(tm) % nr) * CMP_STRIDE + np.arange(tm) // nr
    perm = jnp.asarray(np.arange(tm)[None, :] == src[:, None], BF16)
    tok = lambda n: pl.BlockSpec((1, tm, n), lambda b, i: (b, i, 0))
    per_b = pl.BlockSpec((1, 1, D), lambda b, i: (b, 0, 0))
    tab = pl.BlockSpec((tm, LANES), lambda b, i: (i, 0))
    out_shape = (
        jax.ShapeDtypeStruct((B, NSA_HEADS, S, LANES), BF16),
        jax.ShapeDtypeStruct((B, 2 * NSA_GROUPS, S, LANES), BF16),
        jax.ShapeDtypeStruct((B, 2 * NSA_GROUPS, S // CMP_STRIDE, CMP_STRIDE * NSA_DH), BF16),
        jax.ShapeDtypeStruct((B, 2 * NSA_GROUPS, VT_ROWS, S), BF16),
        jax.ShapeDtypeStruct((B, NSA_GROUPS, 2 * SUBLANES, S), F32),
        jax.ShapeDtypeStruct((B, S // tm, SUBLANES, LANES), F32),
        jax.ShapeDtypeStruct((B, S, HGRN_WIDTH), BF16),
        jax.ShapeDtypeStruct((B, S, HGRN_WIDTH), F32),
        jax.ShapeDtypeStruct((B, S, HGRN_VWIDTH), BF16),
        jax.ShapeDtypeStruct((B, S, HGRN_VWIDTH), BF16),
        jax.ShapeDtypeStruct((B, S, D), BF16),
        jax.ShapeDtypeStruct((B, S, D), BF16),
    )
    out_specs = (
        pl.BlockSpec((1, NSA_HEADS, tm, LANES), lambda b, i: (b, 0, i, 0)),
        pl.BlockSpec((1, 2 * NSA_GROUPS, tm, LANES), lambda b, i: (b, 0, i, 0)),
        pl.BlockSpec((1, 2 * NSA_GROUPS, tm // CMP_STRIDE, CMP_STRIDE * NSA_DH), lambda b, i: (b, 0, i, 0)),
        pl.BlockSpec((1, 2 * NSA_GROUPS, VT_ROWS, tm), lambda b, i: (b, 0, 0, i)),
        pl.BlockSpec((1, NSA_GROUPS, 2 * SUBLANES, tm), lambda b, i: (b, 0, 0, i)),
        pl.BlockSpec((1, 1, SUBLANES, LANES), lambda b, i: (b, i, 0, 0)),
        tok(HGRN_WIDTH), tok(HGRN_WIDTH), tok(HGRN_VWIDTH), tok(HGRN_VWIDTH), tok(D), tok(D),
    )
    return pl.pallas_call(
        _in_proj_kernel,
        grid=(B, S // tm),
        in_specs=[tok(D), per_b, per_b, tab, tab, _resident(lb.shape),
                  _resident(wn.shape), _resident(bn.shape), _resident(wg.shape), _resident(bg.shape),
                  _resident(wh.shape), _resident(bh.shape), _resident(wm.shape), _resident(bm.shape),
                  _resident(perm.shape)],
        out_specs=out_specs,
        out_shape=out_shape,
        compiler_params=_cparams(2),
        name="in_proj",
    )(x, sc, sh, cos_t, sin_t, lb, wn, bn, wg, bg, wh, bh, wm, bm, perm)


def _prep_in_proj_weights(w_in_l, b_in_l):
    o = IN_OFFSETS
    col = lambda i: (w_in_l[:, o[i]:o[i + 1]], b_in_l[o[i]:o[i + 1]])
    q_a, k_c, v_c, k_s, v_s, k_w, v_w, g_a, q_b, f_b, i_b, g_b, gm_a, gm_b = [col(i) for i in range(14)]

    def cat(parts):
        return (jnp.concatenate([p[0] for p in parts], axis=1).astype(BF16),
                jnp.concatenate([p[1] for p in parts], axis=0).reshape(1, -1).astype(F32))

    wn, bn = cat([q_a, k_c, k_s, k_w, v_c, v_s, v_w])
    per_group = 3 * NSA_REP
    gw = jnp.zeros((w_in_l.shape[0], N_GATE), w_in_l.dtype)
    gb = jnp.zeros((N_GATE,), b_in_l.dtype)
    for g in range(NSA_GROUPS):
        gw = gw.at[:, g * LANES:g * LANES + per_group].set(g_a[0][:, g * per_group:(g + 1) * per_group])
        gb = gb.at[g * LANES:g * LANES + per_group].set(g_a[1][g * per_group:(g + 1) * per_group])
    wg, bg = gw.astype(BF16), gb.reshape(1, -1).astype(F32)
    wh, bh = cat([q_b, f_b, i_b, g_b])
    wm, bm = cat([gm_a, gm_b])
    return wn, bn, wg, bg, wh, bh, wm, bm


def _compress_kernel(t_ref, pe_ref, w1_ref, w2_ref, o_ref):
    half = CMP_STRIDE * NSA_DH
    for s in range(2):
        pe = jnp.broadcast_to(pe_ref[s], (8, 2 * half)).astype(BF16)
        c = _dot(pe, w1_ref[s])[0:1]
        for g in range(NSA_GROUPS):
            t = t_ref[0, s * NSA_GROUPS + g]
            nrow = t.shape[0]
            a = _dot(t, w1_ref[s, 0:half, :])
            b = _dot(t, w1_ref[s, half:2 * half, :])
            h = a + pltpu.roll(b, nrow - 1, 0) + c
            o_ref[0, s * NSA_GROUPS + g] = _dot(_silu(h).astype(BF16), w2_ref[s]).astype(BF16)


def nsa_compress(kv, pe, w1, w2):
    B, _, nrow, _ = kv.shape
    dh = NSA_DH
    G = NSA_GROUPS
    kv_rows = kv
    return pl.pallas_call(
        _compress_kernel,
        grid=(B,),
        in_specs=[
            pl.BlockSpec((1, 2 * G, nrow, CMP_STRIDE * dh), lambda b: (b, 0, 0, 0)),
            _resident(pe.shape), _resident(w1.shape), _resident(w2.shape),
        ],
        out_specs=pl.BlockSpec((1, 2 * G, nrow, LANES), lambda b: (b, 0, 0, 0)),
        out_shape=jax.ShapeDtypeStruct((B, 2 * G, nrow, LANES), BF16),
        compiler_params=_cparams(1),
        name="nsa_compress",
    )(kv_rows, pe, w1, w2)


def _nsa_kernel(q_ref, cmp_ref, kx_ref, vt_ref, gt_ref, ovt_ref, o_ref, *, n_sel, n_tiles, bounded_scores):
    def step(i):
        gens = [_nsa_tile(i * NSA_STEP_TILES + qi, qi, g, q_ref, cmp_ref, kx_ref, vt_ref, gt_ref, ovt_ref, o_ref,
                          n_sel, bounded_scores)
                for qi in range(NSA_STEP_TILES) for g in range(NSA_GROUPS)]
        while gens:
            gens = [gen for gen in gens if next(gen, "done") != "done"]

    for i in range(n_tiles // NSA_STEP_TILES):
        pl.when(pl.program_id(1) == i)(functools.partial(step, i))


def _round_up(x, m):
    return -(-x // m) * m


def _nsa_tile(qb, qi, g, q_ref, cmp_ref, kx_ref, vt_ref, gt_ref, ovt_ref, o_ref, n_sel, bounded_scores):
    G, R, TQ, dh = NSA_GROUPS, NSA_REP, Q_TILE, NSA_DH
    cols = R * TQ
    s0 = qb * TQ
    tile_rows = slice(qi * TQ, (qi + 1) * TQ)
    kc_ref, vc_ref = cmp_ref.at[0, g], cmp_ref.at[0, G + g]
    ks_ref, kw_ref = kx_ref.at[0, g], kx_ref.at[0, G + g]
    vst_ref, vwt_ref = vt_ref.at[0, g], vt_ref.at[0, G + g]
    q = q_ref[0, g * R:(g + 1) * R, tile_rows].reshape(cols, LANES)
    t_lane = s0 + (lax.broadcasted_iota(jnp.int32, (1, cols), 1) % TQ)
    kidx = lax.broadcasted_iota(jnp.int32, (K_CHUNK, cols), 0)
    tq = lax.broadcasted_iota(jnp.int32, (K_CHUNK, cols), 1) % TQ
    causal = kidx <= tq

    def branch_blocks(k_ref, vt_ref, c0, n, band_first):
        return [(k_ref, vt_ref, c0 + b0, min(ATT_BLOCK, n - b0), band_first and b0 == 0, b0 + ATT_BLOCK >= n)
                for b0 in range(0, n, ATT_BLOCK)]

    def score_block(blk):
        k_ref, _, c, n, band, diag = blk
        s = _dot_nt(k_ref[c * K_CHUNK:(c + n) * K_CHUNK, :], q)
        parts = [s[i * K_CHUNK:(i + 1) * K_CHUNK] for i in range(n)]
        if band:
            parts[0] = jnp.where(tq < kidx, parts[0], NEG_BIG)
        if diag:
            parts[-1] = jnp.where(causal, parts[-1], NEG_BIG)
        return jnp.concatenate(parts, axis=0) if n > 1 else parts[0]

    def finish_block(s, blk, block_bias=None):
        _, vt_ref, c, n, _, _ = blk
        nk = n * K_CHUNK
        m = None
        if block_bias is not None:
            j0 = c * K_CHUNK // SEL_LEN
            subs = [s[i * SEL_LEN:(i + 1) * SEL_LEN] for i in range(nk // SEL_LEN)]
            bias = [block_bias[j0 + i:j0 + i + 1, :] for i in range(nk // SEL_LEN)]
        if bounded_scores:
            p = jnp.exp2(s) if block_bias is None else jnp.concatenate(
                [jnp.exp2(s_i + b_i) for s_i, b_i in zip(subs, bias)], axis=0)
        elif block_bias is None:
            m = jnp.max(s, axis=0, keepdims=True)
            p = jnp.exp2(s - m)
        else:
            for s_i, b_i in zip(subs, bias):
                m_i = jnp.max(s_i, axis=0, keepdims=True) + b_i
                m = m_i if m is None else jnp.maximum(m, m_i)
            shift = jnp.where(m < 0.5 * NEG_BIG, 0.0, m)
            p = jnp.concatenate([jnp.exp2(s_i + (b_i - shift)) for s_i, b_i in zip(subs, bias)], axis=0)
        vt = vt_ref[:, c * K_CHUNK:c * K_CHUNK + nk]
        return m, _dot(vt, p.astype(BF16))

    def combine(stats):
        total = stats[0][1]
        if bounded_scores:
            for _, acc_i in stats[1:]:
                total = total + acc_i
        elif len(stats) > 1:
            m = stats[0][0]
            for st in stats[1:]:
                m = jnp.maximum(m, st[0])
            total = None
            for m_i, acc_i in stats:
                w = jnp.exp2(m_i - m)
                total = w * acc_i if total is None else total + w * acc_i
        return total[0:dh] * (1.0 / total[dh:dh + 1])

    n_win = WINDOW // K_CHUNK
    win_blocks = branch_blocks(kw_ref, vwt_ref, max(qb - n_win, 0), min(qb, n_win) + 1, qb >= n_win)
    blocks = win_blocks + branch_blocks(ks_ref, vst_ref, 0, qb + 1, False)

    ncb = min(kc_ref.shape[0], _round_up((s0 + TQ - CMP_LEN) // CMP_STRIDE + 1, 2 * SUBLANES))
    sc = _dot_nt(kc_ref[0:ncb, :], q)
    pending = {i: score_block(blocks[i]) for i in range(min(ATT_LOOKAHEAD, len(blocks)))}
    yield
    n_sub = lax.broadcasted_iota(jnp.int32, (ncb, cols), 0)
    mask_c = n_sub * CMP_STRIDE + (CMP_LEN - 1) <= t_lane
    sc = jnp.where(mask_c, sc, NEG_BIG)
    mc = jnp.max(sc, axis=0, keepdims=True)
    ec = jnp.where(mask_c, jnp.exp2(sc - mc), 0.0)
    pc = ec * (1.0 / jnp.maximum(jnp.sum(ec, axis=0, keepdims=True), 1e-30))
    o_c = _dot_tn(vc_ref[0:ncb, :], pc.astype(BF16))[0:dh]

    nb_live = (s0 + TQ) // SEL_LEN
    nb = min(ovt_ref.shape[0], _round_up(nb_live, 2 * SUBLANES))
    psum = pc[:, 0:TQ]
    for r in range(1, R):
        psum = psum + pc[:, r * TQ:(r + 1) * TQ]
    p_hi = psum.astype(BF16)
    p_lo = (psum - p_hi.astype(F32)).astype(BF16)
    ovt = ovt_ref[0:nb, 0:ncb]
    imp = _dot(ovt, p_hi) + _dot(ovt, p_lo)
    yield
    jb =lax.broadcasted_iota(jnp.int32, (nb, TQ), 0)
    tb = (s0 + lax.broadcasted_iota(jnp.int32, (nb, TQ), 1)) // SEL_LEN
    valid = jb <= tb
    forced = jnp.where(valid, jnp.where(jb == 0, 1.0, jnp.where(jb >= tb - 1, 1.0, 0.0)), 0.0)
    score = jnp.where(forced > 0.5, FORCE_SCORE, jnp.where(valid, imp, -1.0))
    rank = jnp.zeros((nb, TQ), F32)
    for i in range(nb_live):
        si = score[i:i + 1, :]
        tie_first = jnp.where(jb > i, 1.0, 0.0)
        rank = rank + jnp.where(si > score, 1.0, jnp.where(si == score, tie_first, 0.0))
    sel_bias = jnp.where(rank < n_sel, 0.0, NEG_BIG)
    sel_bias = jnp.concatenate([sel_bias] * R, axis=1)

    stats = []
    for i, blk in enumerate(blocks):
        is_sel = i >= len(win_blocks)
        stats.append(finish_block(pending.pop(i), blk, sel_bias if is_sel else None))
        if i + ATT_LOOKAHEAD < len(blocks):
            pending[i + ATT_LOOKAHEAD] = score_block(blocks[i + ATT_LOOKAHEAD])
        yield
    o_w = combine(stats[:len(win_blocks)])
    o_s = combine(stats[len(win_blocks):])

    gate = gt_ref[0, g, :, tile_rows]
    pieces = []
    for r in range(R):
        sl = slice(r * TQ, (r + 1) * TQ)
        o_r = (gate[3 * r:3 * r + 1, :] * o_c[:, sl] + gate[3 * r + 1:3 * r + 2, :] * o_s[:, sl]
               + gate[3 * r + 2:3 * r + 3, :] * o_w[:, sl])
        pieces.append(o_r.T)
    o_ref[0, tile_rows, g * R * dh:(g + 1) * R * dh] = jnp.concatenate(pieces, axis=1).astype(BF16)


def nsa_attend(q, kx, vt, cmp, gates_t, bounded_scores):
    B, H, S, _ = q.shape
    G, R, dh = NSA_GROUPS, NSA_REP, NSA_DH
    ncb = S // CMP_STRIDE
    nb = S // SEL_LEN
    assert (S % Q_TILE == 0 and Q_TILE == K_CHUNK and WINDOW % K_CHUNK == 0 and K_CHUNK % SEL_LEN == 0
            and 3 * R <= 2 * SUBLANES)
    cstart = np.arange(ncb) * CMP_STRIDE
    sstart = np.arange(nb) * SEL_LEN
    overlap = ((cstart[:, None] < sstart[None, :] + SEL_LEN) & (cstart[:, None] + CMP_LEN > sstart[None, :]))
    ovt = jnp.asarray(overlap.T, BF16)
    assert (S // Q_TILE) % NSA_STEP_TILES == 0
    tq = NSA_STEP_TILES * Q_TILE
    per_batch = lambda a: pl.BlockSpec((1,) + a.shape[1:], lambda b, i: (b, 0, 0, 0))
    return pl.pallas_call(
        functools.partial(_nsa_kernel, n_sel=min(SEL_TOPK, nb), n_tiles=S // Q_TILE, bounded_scores=bounded_scores),
        grid=(B, S // tq),
        in_specs=[
            pl.BlockSpec((1, H, tq, LANES), lambda b, i: (b, 0, i, 0)),
            per_batch(cmp), per_batch(kx), per_batch(vt),
            pl.BlockSpec((1, G, 2 * SUBLANES, tq), lambda b, i: (b, 0, 0, i)),
            pl.BlockSpec(ovt.shape, lambda b, i: (0, 0)),
        ],
        out_specs=pl.BlockSpec((1, tq, H * dh), lambda b, i: (b, i, 0)),
        out_shape=jax.ShapeDtypeStruct((B, S, H * dh), BF16),
        compiler_params=_cparams(2),
        name="nsa_attend",
    )(q, cmp, kx, vt, gates_t, ovt)


def _hgrn_kernel(q_ref, lf_ref, v_ref, g_ref, ng_ref, tri_ref, o_ref, st_ref, *, bounded_decay):
    C, SB = HGRN_CHUNK, HGRN_SUB
    NBK = C // SB
    S = q_ref.shape[1]
    st_ref[...] = jnp.zeros(st_ref.shape, F32)
    row = lax.broadcasted_iota(jnp.int32, (C, HGRN_DK), 0)
    sub_row = lax.broadcasted_iota(jnp.int32, (SB, 1), 0)
    cr = lax.broadcasted_iota(jnp.int32, (C, C), 0)
    cc = lax.broadcasted_iota(jnp.int32, (C, C), 1)
    diag_mask = (cr // SB == cc // SB) & (cc <= cr)

    def head_chunk(h, r0):
        hs = slice(h * HGRN_DK, (h + 1) * HGRN_DK)
        q = q_ref[0, pl.ds(r0, C), hs].astype(F32)
        lf = lf_ref[0, pl.ds(r0, C), hs]
        v_bf = v_ref[0, pl.ds(r0, C), hs]
        v = v_bf.astype(F32)
        kh = 1.0 - jnp.exp(lf)
        tri = tri_ref[...]
        lf0 = lf.astype(BF16)
        lf1 = (lf - lf0.astype(F32)).astype(BF16)
        lf2 = (lf - lf0.astype(F32) - lf1.astype(F32)).astype(BF16)
        b = _dot(tri, lf0) + _dot(tri, lf1) + _dot(tri, lf2)
        yield
        b_last = b[C - 1:C, :]
        st = st_ref[h]
        o = _dot_nt((q * jnp.exp(b)).astype(BF16), st.astype(BF16))
        b_end = jnp.concatenate(
            [jnp.broadcast_to(b[(j + 1) * SB - 1:(j + 1) * SB, :], (SB, HGRN_DK)) for j in range(NBK)], axis=0)
        k_end = kh * jnp.exp(b_end - b)
        q_parts, k_parts = [], []
        for j in range(NBK - 1):
            lo = (j + 1) * SB
            qj = q[lo:] * jnp.exp(b[lo:] - b[lo - 1:lo, :])
            q_parts.append(jnp.concatenate([jnp.zeros((lo, HGRN_DK), F32), qj], axis=0))
            k_parts.append(jnp.where((row >= j * SB) & (row < lo), k_end, 0.0))
        q_cat = jnp.concatenate(q_parts, axis=1).astype(BF16)
        k_cat = jnp.concatenate(k_parts, axis=1).astype(BF16)
        a_off = _dot_nt(q_cat, k_cat)
        k_last = (kh * jnp.exp(b_last - b)).astype(BF16)
        st_ref[h] = st * jnp.exp(b_last) + _dot_tn(v_bf, k_last)
        if bounded_decay:
            b_start = jnp.concatenate([jnp.zeros((SB, HGRN_DK), F32), b_end[:C - SB]], axis=0)
            a_dg = _dot_nt((q * jnp.exp(b - b_start)).astype(BF16), (kh * jnp.exp(b_start - b)).astype(BF16))
            yield
            o = o + _dot(jnp.where(diag_mask, a_dg, a_off).astype(BF16), v_bf)
            yield
        else:
            yield
            o = o + _dot(a_off.astype(BF16), v_bf)
            diag = []
            for j in range(NBK):
                sl = slice(j * SB, (j + 1) * SB)
                qj, bj, kj, vj = q[sl], b[sl], kh[sl], v[sl]
                oj = jnp.zeros((SB, HGRN_DV), F32)
                for s in range(SB):
                    w = jnp.exp(jnp.minimum(bj - bj[s:s + 1, :], 0.0))
                    a = jnp.sum(qj * kj[s:s + 1, :] * w, axis=-1, keepdims=True)
                    a = jnp.where(sub_row >= s, a, 0.0)
                    oj = oj + a * vj[s:s + 1, :]
                diag.append(oj)
            o = o + jnp.concatenate(diag, axis=0)
            yield
        o = o * lax.rsqrt(jnp.mean(o * o, axis=-1, keepdims=True) + RMS_EPS) * ng_ref[...]
        o_ref[0, pl.ds(r0, C), hs] = (o * g_ref[0, pl.ds(r0, C), hs].astype(F32)).astype(BF16)

    def chunk(ci, carry):
        r0 = pl.multiple_of(ci * (C * HGRN_UNROLL), C * HGRN_UNROLL)
        heads = [head_chunk(h, r0 + u * C) for u in range(HGRN_UNROLL) for h in range(HGRN_HEADS)]
        for _ in range(3):
            for gen in heads:
                next(gen)
        for gen in heads:
            next(gen, None)
        return carry

    lax.fori_loop(0, S // (C * HGRN_UNROLL), chunk, 0)


def hgrn2(hq, hlf, hv, hg, norm_g, bounded_decay):
    B, S, _ = hq.shape
    C = HGRN_CHUNK
    assert S % (C * HGRN_UNROLL) == 0
    tri = jnp.asarray(np.tril(np.ones((C, C))), BF16)
    spec = lambda n: pl.BlockSpec((1, S, n), lambda b: (b, 0, 0))
    return pl.pallas_call(
        functools.partial(_hgrn_kernel, bounded_decay=bounded_decay),
        grid=(B,),
        in_specs=[spec(HGRN_WIDTH), spec(HGRN_WIDTH), spec(HGRN_VWIDTH), spec(HGRN_VWIDTH),
                  pl.BlockSpec((1, HGRN_DV), lambda b: (0, 0)),
                  pl.BlockSpec((C, C), lambda b: (0, 0))],
        out_specs=spec(HGRN_VWIDTH),
        out_shape=jax.ShapeDtypeStruct((B, S, HGRN_VWIDTH), BF16),
        scratch_shapes=[pltpu.VMEM((HGRN_HEADS, HGRN_DV, HGRN_DK), F32)],
        compiler_params=_cparams(1),
        name="hgrn2",
    )(hq, hlf, hv, hg, norm_g.reshape(1, HGRN_DV).astype(F32), tri)


MLP_COLS = 1024


def _merge_mlp_kernel(ya_ref, yb_ref, gma_ref, gmb_ref, x_ref, mod_ref, ln_ref,
                      wa_ref, wb_ref, wo_ref, w1_ref, w2_ref, o_ref):
    gt1, sh2, sc2, gt2 = [mod_ref[0, i:i + 1, :] for i in range(4)]
    g1, b1, g2, b2 = [ln_ref[i:i + 1, :] for i in range(4)]
    pa = _dot(ya_ref[0], wa_ref[...])
    pb = _dot(yb_ref[0], wb_ref[...])
    merged = gma_ref[0].astype(F32) * pa + gmb_ref[0].astype(F32) * pb
    y = _dot(merged.astype(BF16), wo_ref[...])
    x = _layer_norm(DEEPNORM_ALPHA * x_ref[0] + (1.0 + gt1) * y, g1, b1)
    u = (x * (1.0 + sc2) + sh2).astype(BF16)
    y = jnp.zeros(x.shape, F32)
    for c in range(MLP_HIDDEN // MLP_COLS):
        h = jnp.maximum(_dot(u, w1_ref[:, c * MLP_COLS:(c + 1) * MLP_COLS]), 0.0)
        y = y + _dot((h * h).astype(BF16), w2_ref[c * MLP_COLS:(c + 1) * MLP_COLS, :])
    o_ref[0] = _layer_norm(DEEPNORM_ALPHA * x + (1.0 + gt2) * y, g2, b2)


def merge_mlp(ya, yb, gma, gmb, x, mod4, ln4, wa, wb, wo, w1, w2):
    B, S, D = x.shape
    tm = min(TOKEN_TILE, S)
    tok = lambda n: pl.BlockSpec((1, tm, n), lambda b, i: (b, i, 0))
    return pl.pallas_call(
        _merge_mlp_kernel,
        grid=(B, S // tm),
        in_specs=[tok(NSA_WIDTH), tok(HGRN_VWIDTH), tok(D), tok(D), tok(D),
                  pl.BlockSpec((1, 4, D), lambda b, i: (b, 0, 0)), _resident(ln4.shape),
                  _resident(wa.shape), _resident(wb.shape), _resident(wo.shape),
                  _resident(w1.shape), _resident(w2.shape)],
        out_specs=tok(D),
        out_shape=jax.ShapeDtypeStruct((B, S, D), F32),
        compiler_params=_cparams(2, vmem=MERGE_MLP_VMEM),
        name="merge_mlp",
    )(ya, yb, gma, gmb, x, mod4, ln4, wa, wb, wo, w1, w2)


def _rope_tables(S):
    inv = 1.0 / (ROPE_THETA ** (jnp.arange(0, NSA_DH, 2, dtype=F32) / NSA_DH))
    ang = jnp.arange(S, dtype=F32)[:, None] * inv[None, :]
    cos, sin = jnp.cos(ang), jnp.sin(ang)
    reps = LANES // NSA_DH
    return (jnp.tile(jnp.concatenate([cos, cos], axis=1), (1, reps)),
            jnp.tile(jnp.concatenate([-sin, sin], axis=1), (1, reps)))


def kernel(x, c, w_in, b_in, cmp_pe_k, cmp_pe_v, cmp_wk1, cmp_wk2, cmp_wv1, cmp_wv2, hgrn_lb_logits, hgrn_norm_g, w_branch_a, w_branch_b, w_out, w_ada, b_ada, ln1_g, ln1_b, w_mlp1, w_mlp2, ln2_g, ln2_b):
    B, S, D = x.shape
    G = NSA_GROUPS
    lb_all = jnp.cumsum(jax.nn.softmax(hgrn_lb_logits.astype(F32), axis=0), axis=0)
    lb_all = lb_all - lb_all[0:1]
    cos_t, sin_t = _rope_tables(S)
    mod = adaln_mod(c, w_ada, b_ada)
    for l in range(DEPTH):
        sh1, sc1, gt1, sh2, sc2, gt2 = [mod[l, :, None, i * D:(i + 1) * D] for i in range(6)]
        wts = _prep_in_proj_weights(w_in[l], b_in[l])
        q, kx, kv, vt, gates_t, nrm, hq, hlf, hv, hg, gma, gmb = in_proj(x, sc1, sh1, cos_t, sin_t, lb_all[l].reshape(1, -1), wts)
        pe = jnp.stack([cmp_pe_k[l].reshape(1, -1), cmp_pe_v[l].reshape(1, -1)])
        w1 = jnp.stack([cmp_wk1[l], cmp_wv1[l]]).astype(BF16)
        w2 = jnp.pad(jnp.stack([cmp_wk2[l], cmp_wv2[l]]), ((0, 0), (0, 0), (0, LANES - NSA_DH))).astype(BF16)
        cmp = nsa_compress(kv, pe, w1, w2)
        bounded = jnp.max(nrm[:, :, 0, 0]) * jnp.max(nrm[:, :, 1, 0]) <= SCORE_BOUND ** 2
        ya = lax.cond(bounded,
                      functools.partial(nsa_attend, bounded_scores=True),
                      functools.partial(nsa_attend, bounded_scores=False),
                      q, kx, vt, cmp, gates_t)
        yb = lax.cond(jnp.min(nrm[:, :, 2, 0]) >= -HGRN_MAX_STEP_DECAY,
                      functools.partial(hgrn2, bounded_decay=True),
                      functools.partial(hgrn2, bounded_decay=False),
                      hq, hlf, hv, hg, hgrn_norm_g[l])
        mod4 = jnp.concatenate([gt1, sh2, sc2, gt2], axis=1)
        ln4 = jnp.stack([ln1_g[l], ln1_b[l], ln2_g[l], ln2_b[l]])
        x = merge_mlp(ya, yb, gma, gmb, x, mod4, ln4,
                      w_branch_a[l].astype(BF16), w_branch_b[l].astype(BF16), w_out[l].astype(BF16),
                      w_mlp1[l].astype(BF16), w_mlp2[l].astype(BF16))
    return x
```
